```python
import jax, jax.numpy as jnp
from jax import lax
import numpy as np

D_MODEL = 1024
BATCH = 1
SEQ = 16384
DEPTH = 4
DEC_BATCH = 16
DEC_SEQ = 16
PAST_LEN = 1024

CHUNK = 64
N_PAST_CHUNKS = 8
PAST_BAND = N_PAST_CHUNKS * CHUNK
BAND = (N_PAST_CHUNKS + 1) * CHUNK
MIX_WIDTH = 3 * D_MODEL // 4
X_WIDTH = D_MODEL // 4
D_INNER = MIX_WIDTH + X_WIDTH
HD_A = 64
H_A = MIX_WIDTH // HD_A
REL_CLIP = 128
DK_B = 128
DV_B = 128
H_B = MIX_WIDTH // DK_B
H_X = 4
HD_X = X_WIDTH // H_X
N_MEM = 256
N_MIXERS = 2
N_A = (DEPTH + 1) // 2
N_B = DEPTH // 2
D_PROJ = 3 * MIX_WIDTH + X_WIDTH + D_INNER
EPS = 1e-6
NEG = -1e30
F32 = jnp.float32

kernel_name = 'hybrid_streaming_encoder_step'


def rmsnorm(x, g):
    x32 = x.astype(F32)
    y = x32 * lax.rsqrt(jnp.mean(x32 * x32, axis=-1, keepdims=True) + EPS)
    return (y * g.astype(F32)).astype(x.dtype)


def heads(t, h, d):
    return t.reshape(*t.shape[:-1], h, d)


def rel_bias_lookup(table, dist):
    idx = jnp.clip(dist, -REL_CLIP, REL_CLIP) + REL_CLIP
    return table[:, idx].astype(F32)


def band_attn_prompt(q, k, v, table):
    b, t = q.shape[:2]
    nc = t // CHUNK
    qc = q.reshape(b, nc, CHUNK, H_A, HD_A)
    pad = ((0, 0), (N_PAST_CHUNKS, 0), (0, 0), (0, 0), (0, 0))
    kp = jnp.pad(k.reshape(b, nc, CHUNK, H_A, HD_A), pad)
    vp = jnp.pad(v.reshape(b, nc, CHUNK, H_A, HD_A), pad)
    kb = jnp.concatenate([kp[:, j:j + nc] for j in range(N_PAST_CHUNKS + 1)], axis=2)
    vb = jnp.concatenate([vp[:, j:j + nc] for j in range(N_PAST_CHUNKS + 1)], axis=2)
    s = jnp.einsum('bcqhd,bckhd->bchqk', qc, kb).astype(F32) * (HD_A ** -0.5)
    dist = PAST_BAND + jnp.arange(CHUNK)[:, None] - jnp.arange(BAND)[None, :]
    s = s + rel_bias_lookup(table, dist)
    valid = (jnp.arange(BAND) // CHUNK)[None, :] >= (N_PAST_CHUNKS - jnp.arange(nc))[:, None]
    s = jnp.where(valid[None, :, None, None, :], s, NEG)
    p = jax.nn.softmax(s, axis=-1).astype(v.dtype)
    o = jnp.einsum('bchqk,bckhd->bcqhd', p, vb)
    return o.reshape(b, t, MIX_WIDTH)


def band_attn_sample(q, k, v, k_cache, v_cache, table):
    b, s_len = q.shape[:2]
    L = k_cache.shape[1]
    kk = jnp.concatenate([k_cache.astype(k.dtype), k], axis=1)
    vv = jnp.concatenate([v_cache.astype(v.dtype), v], axis=1)
    s = jnp.einsum('bqhd,bkhd->bhqk', q, kk).astype(F32) * (HD_A ** -0.5)
    dist = L + jnp.arange(s_len)[:, None] - jnp.arange(L + s_len)[None, :]
    s = s + rel_bias_lookup(table, dist)
    p = jax.nn.softmax(s, axis=-1).astype(v.dtype)
    o = jnp.einsum('bhqk,bkhd->bqhd', p, vv)
    return o.reshape(b, s_len, MIX_WIDTH)


def hgrn2_inputs(a, b, c, lb):
    q = heads(a, H_B, DK_B).astype(F32)
    z = heads(b, H_B, DK_B).astype(F32)
    i = heads(c, H_B, DV_B).astype(F32)
    lbh = lb.reshape(H_B, DK_B)
    log_f = jnp.logaddexp(jnp.log(lbh), jnp.log1p(-lbh) + jax.nn.log_sigmoid(z))
    k = (1.0 - lbh) * jax.nn.sigmoid(-z)
    return q, k, i, log_f


def hgrn2_block(S, q, k, i, log_f):
    t = q.shape[1]
    A = jnp.cumsum(log_f, axis=1)
    causal = jnp.tril(jnp.ones((t, t), dtype=bool))
    diff = A[:, :, None] - A[:, None, :]
    decay = jnp.exp(jnp.where(causal[None, :, :, None, None], diff, -jnp.inf))
    attn = jnp.einsum('bthk,btshk,bshk->bhts', q, decay, k)
    o = jnp.einsum('bhts,bshv->bthv', attn, i) + jnp.einsum('bthk,bhkv->bthv', q * jnp.exp(A), S)
    a_last = A[:, -1]
    S_new = jnp.exp(a_last)[..., None] * S + jnp.einsum('bshk,bshv->bhkv', k * jnp.exp(a_last[:, None] - A), i)
    return S_new, o


def hgrn2_prompt(q, k, i, log_f):
    b, t = q.shape[:2]
    nb = t // CHUNK

    def to_blocks(u):
        return u.reshape(b, nb, CHUNK, *u.shape[2:]).swapaxes(0, 1)

    S0 = jnp.zeros((b, H_B, DK_B, DV_B), F32)
    S, o = lax.scan(lambda S, xs: hgrn2_block(S, *xs), S0,
                    (to_blocks(q), to_blocks(k), to_blocks(i), to_blocks(log_f)))
    return S, o.swapaxes(0, 1).reshape(b, t, H_B, DV_B)


def hgrn2_out(o, g, dtype):
    o = rmsnorm(o, g.reshape(H_B, DV_B))
    return o.reshape(*o.shape[:-2], MIX_WIDTH).astype(dtype)


def mem_kv(mem, g, w):
    kv = rmsnorm(mem, g) @ w
    return heads(kv[..., :X_WIDTH], H_X, HD_X), heads(kv[..., X_WIDTH:], H_X, HD_X)


def cross_attn(q, mk, mv):
    s = jnp.einsum('bthd,bmhd->bhtm', q, mk.astype(q.dtype)).astype(F32) * (HD_X ** -0.5)
    p = jax.nn.softmax(s, axis=-1).astype(q.dtype)
    o = jnp.einsum('bhtm,bmhd->bthd', p, mv.astype(q.dtype))
    return o.reshape(*o.shape[:-2], X_WIDTH)


def in_proj(x, g, w):
    p = rmsnorm(x, g) @ w
    m = MIX_WIDTH
    return (p[..., :m], p[..., m:2 * m], p[..., 2 * m:3 * m],
            heads(p[..., 3 * m:3 * m + X_WIDTH], H_X, HD_X), p[..., 3 * m + X_WIDTH:])


def out_proj(x, mix_o, xq, mk, mv, gate, w):
    y = jnp.concatenate([mix_o, cross_attn(xq, mk, mv)], axis=-1) * jax.nn.silu(gate)
    return x + y @ w


def setup_inputs(seed: int = 0) -> dict:
    key = jax.random.key(seed)
    ks = jax.random.split(key, 18)
    band_len = min(PAST_BAND, PAST_LEN)

    def nrm(k, shape, s=1.0):
        return s * jax.random.normal(k, shape, F32)

    return {
        'x_prompt': nrm(ks[0], (BATCH, SEQ, D_MODEL)),
        'x_sample': nrm(ks[1], (DEC_BATCH, DEC_SEQ, D_MODEL)),
        'cache_a_k': nrm(ks[2], (N_A, DEC_BATCH, band_len, H_A, HD_A)),
        'cache_a_v': nrm(ks[3], (N_A, DEC_BATCH, band_len, H_A, HD_A)),
        'state_b': nrm(ks[4], (N_B, DEC_BATCH, H_B, DK_B, DV_B), 0.3),
        'cache_mem_k': nrm(ks[5], (DEPTH, DEC_BATCH, N_MEM, H_X, HD_X)),
        'cache_mem_v': nrm(ks[6], (DEPTH, DEC_BATCH, N_MEM, H_X, HD_X)),
        'mem_prompt': nrm(ks[7], (BATCH, N_MEM, D_MODEL)),
        'ln_g': 1.0 + nrm(ks[8], (DEPTH, D_MODEL), 0.02),
        'w_in': nrm(ks[9], (DEPTH, D_MODEL, D_PROJ), D_MODEL ** -0.5),
        'w_out': nrm(ks[10], (DEPTH, D_INNER, D_MODEL), D_INNER ** -0.5),
        'rel_bias_table': nrm(ks[11], (N_A, H_A, 2 * REL_CLIP + 1), 0.5),
        'lower_bounds': nrm(ks[12], (N_B, MIX_WIDTH)),
        'hgrn_norm_g': 1.0 + nrm(ks[13], (N_B, MIX_WIDTH), 0.02),
        'mem_norm_g': 1.0 + nrm(ks[14], (DEPTH, D_MODEL), 0.02),
        'w_mem_kv': nrm(ks[15], (DEPTH, D_MODEL, 2 * X_WIDTH), D_MODEL ** -0.5),
        'final_g': 1.0 + nrm(ks[16], (D_MODEL,), 0.02),
    }


def reference(x_prompt, x_sample, cache_a_k, cache_a_v, state_b, cache_mem_k, cache_mem_v, mem_prompt,
              ln_g, w_in, w_out, rel_bias_table, lower_bounds, hgrn_norm_g, mem_norm_g, w_mem_kv, final_g):
    lb_all = jnp.cumsum(jax.nn.softmax(lower_bounds.astype(F32), axis=0), axis=0)
    lb_all = lb_all - lb_all[:1]
    xp, xs = x_prompt, x_sample
    ak_p, av_p, sb_p, mk_pl, mv_pl, ak_s, av_s, sb_s = [], [], [], [], [], [], [], []
    for l in range(DEPTH):
        j = l // N_MIXERS
        mk_p, mv_p = mem_kv(mem_prompt, mem_norm_g[l], w_mem_kv[l])
        a_p, b_p, c_p, xq_p, g_p = in_proj(xp, ln_g[l], w_in[l])
        a_s, b_s, c_s, xq_s, g_s = in_proj(xs, ln_g[l], w_in[l])
        if l % N_MIXERS == 0:
            k_p, v_p = heads(b_p, H_A, HD_A), heads(c_p, H_A, HD_A)
            o_p = band_attn_prompt(heads(a_p, H_A, HD_A), k_p, v_p, rel_bias_table[j])
            k_s, v_s = heads(b_s, H_A, HD_A), heads(c_s, H_A, HD_A)
            o_s = band_attn_sample(heads(a_s, H_A, HD_A), k_s, v_s, cache_a_k[j], cache_a_v[j], rel_bias_table[j])
            keep = min(PAST_BAND, k_p.shape[1])
            ak_p.append(k_p[:, -keep:])
            av_p.append(v_p[:, -keep:])
            ak_s.append(k_s)
            av_s.append(v_s)
        else:
            S_p, o = hgrn2_prompt(*hgrn2_inputs(a_p, b_p, c_p, lb_all[j]))
            o_p = hgrn2_out(o, hgrn_norm_g[j], xp.dtype)
            S_s, o = hgrn2_block(state_b[j].astype(F32), *hgrn2_inputs(a_s, b_s, c_s, lb_all[j]))
            o_s = hgrn2_out(o, hgrn_norm_g[j], xs.dtype)
            sb_p.append(S_p.astype(xp.dtype))
            sb_s.append(S_s.astype(state_b.dtype))
        xp = out_proj(xp, o_p, xq_p, mk_p, mv_p, g_p, w_out[l])
        xs = out_proj(xs, o_s, xq_s, cache_mem_k[l], cache_mem_v[l], g_s, w_out[l])
        mk_pl.append(mk_p)
        mv_pl.append(mv_p)
    y_prompt = rmsnorm(xp, final_g)
    y_sample = rmsnorm(xs, final_g)
    return (y_prompt, y_sample, jnp.stack(ak_p), jnp.stack(av_p), jnp.stack(sb_p), jnp.stack(mk_pl), jnp.stack(mv_pl),
            jnp.stack(ak_s), jnp.stack(av_s), jnp.stack(sb_s))
```

```python
import functools

import numpy as np
import jax
import jax.numpy as jnp
from jax import lax
from jax.experimental import pallas as pl
from jax.experimental.pallas import tpu as pltpu

D_MODEL = 1024
DEPTH = 4
CHUNK = 64
N_PAST_CHUNKS = 8
PAST_BAND = N_PAST_CHUNKS * CHUNK
MIX_WIDTH = 768
X_WIDTH = 256
D_INNER = MIX_WIDTH + X_WIDTH
HD_A = 64
H_A = MIX_WIDTH // HD_A
REL_CLIP = 128
DK_B = 128
H_B = MIX_WIDTH // DK_B
H_X = 4
HD_X = 64
N_MEM = 256
EPS = 1e-6
NEG = -1e30
F32 = jnp.float32
BF16 = jnp.bfloat16

LANES = 128
VMEM_LIMIT_BYTES = 56 * 1024 * 1024
PROJ_SPLITS = (MIX_WIDTH, MIX_WIDTH, MIX_WIDTH, X_WIDTH, D_INNER)


def _params(*sem):
    return pltpu.CompilerParams(dimension_semantics=sem, vmem_limit_bytes=VMEM_LIMIT_BYTES)


def _norm_proj_kernel(x_ref, g_ref, w_ref, *out_refs, splits):
    x = x_ref[...]
    ms = jnp.mean(x * x, axis=-1, keepdims=True)
    xn = (x * lax.rsqrt(ms + EPS) * g_ref[...]).astype(BF16)
    off = 0
    for o_ref, n in zip(out_refs, splits):
        o_ref[...] = jnp.dot(xn, w_ref[:, off:off + n], preferred_element_type=F32)
        off += n


def norm_proj(x2d, g, w_bf16, splits, tm):
    rows, d = x2d.shape
    n_total = w_bf16.shape[1]
    return pl.pallas_call(
        functools.partial(_norm_proj_kernel, splits=splits),
        grid=(rows // tm,),
        in_specs=[
            pl.BlockSpec((tm, d), lambda i: (i, 0)),
            pl.BlockSpec((1, d), lambda i: (0, 0)),
            pl.BlockSpec((d, n_total), lambda i: (0, 0)),
        ],
        out_specs=[pl.BlockSpec((tm, n), lambda i: (i, 0)) for n in splits],
        out_shape=[jax.ShapeDtypeStruct((rows, n), F32) for n in splits],
        compiler_params=_params("arbitrary"),
    )(x2d, g.reshape(1, d), w_bf16)


def _head_pair_attention(q_pair, k_pair, v_pair, add_bias):
    lane = lax.broadcasted_iota(jnp.int32, q_pair.shape, 1)
    first = lane < HD_A
    outs = []
    for sub in range(2):
        qm = jnp.where(first if sub == 0 else jnp.logical_not(first), q_pair, 0.0).astype(BF16)
        s = lax.dot_general(qm, k_pair, (((1,), (1,)), ((), ())), preferred_element_type=F32)
        s = add_bias(s, sub)
        m = jnp.max(s, axis=-1, keepdims=True)
        p = jnp.exp(s - m)
        l = jnp.sum(p, axis=-1, keepdims=True)
        o = jnp.dot(p.astype(BF16), v_pair, preferred_element_type=F32)
        outs.append(o / l)
    return jnp.where(first, outs[0], outs[1])


def _band_attn_kernel(q_ref, kprev_ref, kcur_ref, vprev_ref, vcur_ref, bias_ref, o_ref,
                      kwin, vwin, *, n_chunks, chunk, prev_rows, mask_first):
    step = pl.program_id(0)
    cur_rows = n_chunks * chunk
    band = prev_rows + chunk
    kwin[0:prev_rows, :] = kprev_ref[...].astype(BF16)
    kwin[prev_rows:prev_rows + cur_rows, :] = kcur_ref[...].astype(BF16)
    vwin[0:prev_rows, :] = vprev_ref[...].astype(BF16)
    vwin[prev_rows:prev_rows + cur_rows, :] = vcur_ref[...].astype(BF16)

    def chunk_body(cc, carry):
        r0 = pl.multiple_of(cc * chunk, chunk)
        if mask_first:
            key_row = lax.broadcasted_iota(jnp.int32, (1, band), 1) + r0 + (step - 1) * prev_rows
            neg = jnp.where(key_row < 0, NEG, 0.0)
        for hp in range(H_A // 2):
            lanes = slice(LANES * hp, LANES * (hp + 1))
            q_pair = q_ref[pl.ds(r0, chunk), lanes] * (HD_A ** -0.5)
            k_pair = kwin[pl.ds(r0, band), lanes]
            v_pair = vwin[pl.ds(r0, band), lanes]

            def add_bias(s, sub, hp=hp):
                s = s + bias_ref[2 * hp + sub]
                return s + neg if mask_first else s

            o_ref[pl.ds(r0, chunk), lanes] = _head_pair_attention(q_pair, k_pair, v_pair, add_bias)
        return carry

    if n_chunks == 1:
        chunk_body(0, 0)
    else:
        lax.fori_loop(0, n_chunks, chunk_body, 0)


def band_attn_prompt(q, k, v, bias, block_rows):
    rows = q.shape[0]
    n_chunks = block_rows // CHUNK
    assert block_rows == PAST_BAND
    cur = pl.BlockSpec((block_rows, MIX_WIDTH), lambda i: (i, 0))
    prev = pl.BlockSpec((block_rows, MIX_WIDTH), lambda i: (jnp.maximum(i - 1, 0), 0))
    return pl.pallas_call(
        functools.partial(_band_attn_kernel, n_chunks=n_chunks, chunk=CHUNK,
                          prev_rows=PAST_BAND, mask_first=True),
        grid=(rows // block_rows,),
        in_specs=[cur, prev, cur, prev, cur,
                  pl.BlockSpec(bias.shape, lambda i: (0, 0, 0))],
        out_specs=cur,
        out_shape=jax.ShapeDtypeStruct((rows, MIX_WIDTH), F32),
        scratch_shapes=[pltpu.VMEM((PAST_BAND + block_rows, MIX_WIDTH), BF16)] * 2,
        compiler_params=_params("arbitrary"),
    )(q, k, k, v, v, bias)


def band_attn_sample(q, k, v, k_cache, v_cache, bias):
    b, s_len, _ = q.shape
    cache_len = k_cache.shape[1]
    new = pl.BlockSpec((None, s_len, MIX_WIDTH), lambda i: (i, 0, 0))
    old = pl.BlockSpec((None, cache_len, MIX_WIDTH), lambda i: (i, 0, 0))
    return pl.pallas_call(
        functools.partial(_band_attn_kernel, n_chunks=1, chunk=s_len,
                          prev_rows=cache_len, mask_first=False),
        grid=(b,),
        in_specs=[new, old, new, old, new, pl.BlockSpec(bias.shape, lambda i: (0, 0, 0))],
        out_specs=new,
        out_shape=jax.ShapeDtypeStruct((b, s_len, MIX_WIDTH), F32),
        scratch_shapes=[pltpu.VMEM((cache_len + s_len, MIX_WIDTH), BF16)] * 2,
        compiler_params=_params("arbitrary"),
    )(q, k_cache, k, v_cache, v, bias)


def _hgrn2_constants(t):
    halves = []
    h = t // 2
    while h >= 1:
        halves.append(h)
        h //= 2
    n_lvl = len(halves)
    w = np.zeros(((n_lvl + 2) * t, t), np.float32)
    masks = np.zeros((n_lvl + 1, t, t), np.float32)
    for row in range(t):
        w[row, :row + 1] = 1.0
        w[t + row, row + 1:] = 1.0
    for li, h in enumerate(halves):
        base = (2 + li) * t
        for row in range(t):
            r = (row // (2 * h)) * 2 * h + h - 1
            if row > r:
                w[base + row, r + 1:row + 1] = 1.0
            else:
                w[base + row, row + 1:r + 1] = 1.0
        for tq in range(t):
            for sk in range(t):
                if tq // (2 * h) == sk // (2 * h):
                    r = (tq // (2 * h)) * 2 * h + h - 1
                    if tq > r and sk <= r:
                        masks[li, tq, sk] = 1.0
    masks[n_lvl] = np.eye(t, dtype=np.float32)
    return jnp.asarray(w, BF16), jnp.asarray(masks, F32), n_lvl


def _hgrn2_kernel(a_ref, b_ref, c_ref, s0_ref, lbc_ref, g_ref, w_ref, m_ref,
                  o_ref, sout_ref, st_ref, *, t, n_lvl):
    ci = pl.program_id(1)

    @pl.when(ci == 0)
    def _():
        for h in range(H_B):
            st_ref[h] = s0_ref[h].T

    q = a_ref[...]
    z = b_ref[...]
    iv = c_ref[...]
    log_lb = lbc_ref[0:1, :]
    log1m_lb = lbc_ref[1:2, :]
    one_m_lb = lbc_ref[2:3, :]
    u = log1m_lb + jnp.minimum(z, 0.0) - jnp.log1p(jnp.exp(-jnp.abs(z)))
    log_f = jnp.maximum(log_lb, u) + jnp.log1p(jnp.exp(-jnp.abs(log_lb - u)))
    kk = one_m_lb / (1.0 + jnp.exp(z))

    hi = log_f.astype(BF16)
    r1 = log_f - hi.astype(F32)
    mid = r1.astype(BF16)
    lo = (r1 - mid.astype(F32)).astype(BF16)
    w = w_ref[...]
    sums = (jnp.dot(w, hi, preferred_element_type=F32)
            + jnp.dot(w, mid, preferred_element_type=F32)
            + jnp.dot(w, lo, preferred_element_type=F32))

    q_inter = (q * jnp.exp(sums[0:t])).astype(BF16)
    k_rem = (kk * jnp.exp(sums[t:2 * t])).astype(BF16)
    e_last = jnp.exp(sums[t - 1:t])
    iv_b = iv.astype(BF16)

    attn = [None] * H_B
    for li in range(n_lvl + 1):
        if li < n_lvl:
            e = jnp.exp(sums[(2 + li) * t:(3 + li) * t])
            qs = (q * e).astype(BF16)
            ks = (kk * e).astype(BF16)
        else:
            qs = q.astype(BF16)
            ks = kk.astype(BF16)
        mask = m_ref[li]
        for h in range(H_B):
            lanes = slice(LANES * h, LANES * (h + 1))
            part = mask * lax.dot_general(qs[:, lanes], ks[:, lanes], (((1,), (1,)), ((), ())),
                                          preferred_element_type=F32)
            attn[h] = part if attn[h] is None else attn[h] + part

    for h in range(H_B):
        lanes = slice(LANES * h, LANES * (h + 1))
        st = st_ref[h]
        o = jnp.dot(attn[h].astype(BF16), iv_b[:, lanes], preferred_element_type=F32)
        o = o + lax.dot_general(q_inter[:, lanes], st.astype(BF16), (((1,), (1,)), ((), ())),
                                preferred_element_type=F32)
        iv_t = iv[:, lanes].T.astype(BF16)
        st_ref[h] = e_last[:, lanes] * st + jnp.dot(iv_t, k_rem[:, lanes],
                                                    preferred_element_type=F32)
        ms = jnp.mean(o * o, axis=-1, keepdims=True)
        o_ref[:, lanes] = o * lax.rsqrt(ms + EPS) * g_ref[:, lanes]

    @pl.when(ci == pl.num_programs(1) - 1)
    def _():
        for h in range(H_B):
            sout_ref[h] = st_ref[h].T


def hgrn2(a, b, c, s0, lbc, g, t):
    bsz, rows, _ = a.shape
    w, masks, n_lvl = _hgrn2_constants(t)
    tok = pl.BlockSpec((None, t, MIX_WIDTH), lambda bi, ci: (bi, ci, 0))
    state = pl.BlockSpec((None, H_B, DK_B, DK_B), lambda bi, ci: (bi, 0, 0, 0))
    return pl.pallas_call(
        functools.partial(_hgrn2_kernel, t=t, n_lvl=n_lvl),
        grid=(bsz, rows // t),
        in_specs=[tok, tok, tok, state,
                  pl.BlockSpec(lbc.shape, lambda bi, ci: (0, 0)),
                  pl.BlockSpec((1, MIX_WIDTH), lambda bi, ci: (0, 0)),
                  pl.BlockSpec(w.shape, lambda bi, ci: (0, 0)),
                  pl.BlockSpec(masks.shape, lambda bi, ci: (0, 0, 0))],
        out_specs=[tok, state],
        out_shape=[jax.ShapeDtypeStruct((bsz, rows, MIX_WIDTH), F32),
                   jax.ShapeDtypeStruct((bsz, H_B, DK_B, DK_B), F32)],
        scratch_shapes=[pltpu.VMEM((H_B, DK_B, DK_B), F32)],
        compiler_params=_params("arbitrary", "arbitrary"),
    )(a, b, c, s0, lbc, g.reshape(1, MIX_WIDTH), w, masks)


def _post_kernel(x_ref, mix_ref, xq_ref, gate_ref, mk_ref, mv_ref, w_ref, fg_ref, o_ref, *, final):
    gate = gate_ref[...]
    sg = gate / (1.0 + jnp.exp(-gate))
    y_mix = (mix_ref[...] * sg[:, 0:MIX_WIDTH]).astype(BF16)
    acc = x_ref[...] + jnp.dot(y_mix, w_ref[0:MIX_WIDTH, :], preferred_element_type=F32)
    mk = mk_ref[...].astype(BF16)
    mv = mv_ref[...].astype(BF16)
    for hp in range(H_X // 2):
        lanes = slice(LANES * hp, LANES * (hp + 1))
        q_pair = xq_ref[:, lanes] * (HD_X ** -0.5)
        cross = _head_pair_attention(q_pair, mk[:, lanes], mv[:, lanes], lambda s, sub: s)
        rows = slice(MIX_WIDTH + LANES * hp, MIX_WIDTH + LANES * (hp + 1))
        y = (cross * sg[:, rows]).astype(BF16)
        acc = acc + jnp.dot(y, w_ref[rows, :], preferred_element_type=F32)
    if final:
        ms = jnp.mean(acc * acc, axis=-1, keepdims=True)
        acc = acc * lax.rsqrt(ms + EPS) * fg_ref[...]
    o_ref[...] = acc


def post(x, mix, xq, gate, mk, mv, w_bf16, final_g, tm, final):
    bsz, rows, _ = x.shape

    def tok(n):
        return pl.BlockSpec((None, tm, n), lambda bi, i: (bi, i, 0))

    mem = pl.BlockSpec((None, N_MEM, X_WIDTH), lambda bi, i: (bi, 0, 0))
    return pl.pallas_call(
        functools.partial(_post_kernel, final=final),
        grid=(bsz, rows // tm),
        in_specs=[tok(D_MODEL), tok(MIX_WIDTH), tok(X_WIDTH), tok(D_INNER), mem, mem,
                  pl.BlockSpec((D_INNER, D_MODEL), lambda bi, i: (0, 0)),
                  pl.BlockSpec((1, D_MODEL), lambda bi, i: (0, 0))],
        out_specs=tok(D_MODEL),
        out_shape=jax.ShapeDtypeStruct((bsz, rows, D_MODEL), F32),
        compiler_params=_params("arbitrary", "arbitrary"),
    )(x, mix, xq, gate, mk, mv, w_bf16, final_g.reshape(1, D_MODEL))


def _rel_bias(table):
    dist = PAST_BAND + np.arange(CHUNK)[:, None] - np.arange(PAST_BAND + CHUNK)[None, :]
    idx = np.clip(dist, -REL_CLIP, REL_CLIP) + REL_CLIP
    return table[:, idx].astype(F32)


def kernel(x_prompt, x_sample, cache_a_k, cache_a_v, state_b, cache_mem_k, cache_mem_v, mem_prompt,
           ln_g, w_in, w_out, rel_bias_table, lower_bounds, hgrn_norm_g, mem_norm_g, w_mem_kv, final_g):
    bp, seq, _ = x_prompt.shape
    bs, dec_seq, _ = x_sample.shape
    assert bp == 1
    cache_len = cache_a_k.shape[2]

    w_in_b = w_in.astype(BF16)
    w_out_b = w_out.astype(BF16)
    w_mem_b = w_mem_kv.astype(BF16)

    lb_all = jnp.cumsum(jax.nn.softmax(lower_bounds.astype(F32), axis=0), axis=0)
    lb_all = lb_all - lb_all[:1]

    xp = x_prompt
    xs = x_sample
    ak_p, av_p, sb_p, mk_pl, mv_pl, ak_s, av_s, sb_s = [], [], [], [], [], [], [], []
    for l in range(DEPTH):
        j = l // 2
        final = l == DEPTH - 1
        mk_p, mv_p = norm_proj(mem_prompt.reshape(N_MEM, D_MODEL), mem_norm_g[l], w_mem_b[l],
                               (X_WIDTH, X_WIDTH), N_MEM)
        a_p, b_p, c_p, xq_p, g_p = norm_proj(xp.reshape(seq, D_MODEL), ln_g[l], w_in_b[l], PROJ_SPLITS, 512)
        a_s, b_s, c_s, xq_s, g_s = norm_proj(xs.reshape(bs * dec_seq, D_MODEL), ln_g[l], w_in_b[l],
                                             PROJ_SPLITS, bs * dec_seq)

        def per_batch(u):
            return u.reshape(bs, dec_seq, u.shape[-1])

        if l % 2 == 0:
            bias = _rel_bias(rel_bias_table[j])
            o_p = band_attn_prompt(a_p, b_p, c_p, bias, PAST_BAND)
            o_s = band_attn_sample(per_batch(a_s), per_batch(b_s), per_batch(c_s),
                                   cache_a_k[j].reshape(bs, cache_len, MIX_WIDTH),
                                   cache_a_v[j].reshape(bs, cache_len, MIX_WIDTH),
                                   bias[:, :dec_seq, :cache_len + dec_seq])
            keep = min(PAST_BAND, seq)
            ak_p.append(b_p[-keep:].reshape(1, keep, H_A, HD_A))
            av_p.append(c_p[-keep:].reshape(1, keep, H_A, HD_A))
            ak_s.append(b_s.reshape(bs, dec_seq, H_A, HD_A))
            av_s.append(c_s.reshape(bs, dec_seq, H_A, HD_A))
            o_p = o_p.reshape(1, seq, MIX_WIDTH)
        else:
            lb = lb_all[j]
            lbc = jnp.stack([jnp.log(lb), jnp.log1p(-lb), 1.0 - lb])
            o_p, s_p = hgrn2(a_p.reshape(1, seq, MIX_WIDTH), b_p.reshape(1, seq, MIX_WIDTH),
                             c_p.reshape(1, seq, MIX_WIDTH),
                             jnp.zeros((1, H_B, DK_B, DK_B), F32), lbc, hgrn_norm_g[j], CHUNK)
            o_s, s_s = hgrn2(per_batch(a_s), per_batch(b_s), per_batch(c_s),
                             state_b[j].astype(F32), lbc, hgrn_norm_g[j], dec_seq)
            sb_p.append(s_p)
            sb_s.append(s_s)
        xp = post(xp, o_p, xq_p.reshape(1, seq, X_WIDTH), g_p.reshape(1, seq, D_INNER),
                  mk_p.reshape(1, N_MEM, X_WIDTH), mv_p.reshape(1, N_MEM, X_WIDTH),
                  w_out_b[l], final_g, 512, final)
        xs = post(xs, o_s, per_batch(xq_s), per_batch(g_s),
                  cache_mem_k[l].reshape(bs, N_MEM, X_WIDTH), cache_mem_v[l].reshape(bs, N_MEM, X_WIDTH),
                  w_out_b[l], final_g, dec_seq, final)
        mk_pl.append(mk_p.reshape(1, N_MEM, H_X, HD_X))
        mv_pl.append(mv_p.reshape(1, N_MEM, H_X, HD_X))
    return (xp, xs, jnp.stack(ak_p), jnp.stack(av_p), jnp.stack(sb_p), jnp.stack(mk_pl), jnp.stack(mv_pl),
            jnp.stack(ak_s), jnp.stack(av_s), jnp.stack(sb_s))
```

```python
import functools

import numpy as np
import jax
import jax.numpy as jnp
from jax import lax
from jax.experimental import pallas as pl
from jax.experimental.pallas import tpu as pltpu

D_MODEL = 1024
DEPTH = 4
CHUNK = 64
N_PAST_CHUNKS = 8
PAST_BAND = N_PAST_CHUNKS * CHUNK
MIX_WIDTH = 768
X_WIDTH = 256
D_INNER = MIX_WIDTH + X_WIDTH
HD_A = 64
H_A = MIX_WIDTH // HD_A
REL_CLIP = 128
DK_B = 128
H_B = MIX_WIDTH // DK_B
H_X = 4
HD_X = 64
N_MEM = 256
EPS = 1e-6
NEG = -1e30
F32 = jnp.float32
BF16 = jnp.bfloat16

LANES = 128
VMEM_LIMIT_BYTES = 56 * 1024 * 1024
PROJ_SPLITS = (MIX_WIDTH, MIX_WIDTH, MIX_WIDTH, X_WIDTH, D_INNER)


def _params(*sem):
    return pltpu.CompilerParams(dimension_semantics=sem, vmem_limit_bytes=VMEM_LIMIT_BYTES)


def _norm_proj_kernel(x_ref, g_ref, w_ref, *out_refs, splits):
    x = x_ref[...]
    ms = jnp.mean(x * x, axis=-1, keepdims=True)
    xn = (x * lax.rsqrt(ms + EPS) * g_ref[...]).astype(BF16)
    off = 0
    for o_ref, n in zip(out_refs, splits):
        o_ref[...] = jnp.dot(xn, w_ref[:, off:off + n], preferred_element_type=F32)
        off += n


def norm_proj(x2d, g, w_bf16, splits, tm):
    rows, d = x2d.shape
    n_total = w_bf16.shape[1]
    return pl.pallas_call(
        functools.partial(_norm_proj_kernel, splits=splits),
        grid=(rows // tm,),
        in_specs=[
            pl.BlockSpec((tm, d), lambda i: (i, 0)),
            pl.BlockSpec((1, d), lambda i: (0, 0)),
            pl.BlockSpec((d, n_total), lambda i: (0, 0)),
        ],
        out_specs=[pl.BlockSpec((tm, n), lambda i: (i, 0)) for n in splits],
        out_shape=[jax.ShapeDtypeStruct((rows, n), F32) for n in splits],
        name=f"norm_proj_{rows}x{n_total}",
        compiler_params=_params("arbitrary"),
    )(x2d, g.reshape(1, d), w_bf16)


def _pair_scores(q_pair, k_pair):
    first = lax.broadcasted_iota(jnp.int32, q_pair.shape, 1) < HD_A
    q2 = jnp.concatenate([jnp.where(first, q_pair, 0.0), jnp.where(first, 0.0, q_pair)], axis=0)
    return lax.dot_general(q2.astype(BF16), k_pair, (((1,), (1,)), ((), ())),
                           preferred_element_type=F32)


def _pair_values(p, v_ext):
    r = jnp.dot(p, v_ext, preferred_element_type=F32)
    o = r[:, :LANES] / r[:, LANES:]
    m = o.shape[0] // 2
    first = lax.broadcasted_iota(jnp.int32, (m, LANES), 1) < HD_A
    return jnp.where(first, o[:m], o[m:])


def _band_attn_kernel(q_ref, kprev_ref, kcur_ref, vprev_ref, vcur_ref, bias_ref, o_ref,
                      kwin, vext, s_scr, p_scr, *, n_chunks, chunk, prev_rows, mask_first):
    step = pl.program_id(0)
    cur_rows = n_chunks * chunk
    band = prev_rows + chunk
    n_pairs = H_A // 2
    kwin[0:prev_rows, :] = kprev_ref[...].astype(BF16)
    kwin[prev_rows:prev_rows + cur_rows, :] = kcur_ref[...].astype(BF16)
    ones = jnp.ones((prev_rows + cur_rows, LANES), BF16)
    for hp in range(n_pairs):
        lanes = slice(LANES * hp, LANES * (hp + 1))
        vext[0:prev_rows, 2 * LANES * hp:2 * LANES * hp + LANES] = vprev_ref[:, lanes].astype(BF16)
        vext[prev_rows:prev_rows + cur_rows, 2 * LANES * hp:2 * LANES * hp + LANES] = (
            vcur_ref[:, lanes].astype(BF16))
        vext[:, 2 * LANES * hp + LANES:2 * LANES * (hp + 1)] = ones

    def chunk_body(cc, carry):
        r0 = pl.multiple_of(cc * chunk, chunk)
        for hp in range(n_pairs):
            lanes = slice(LANES * hp, LANES * (hp + 1))
            q_pair = q_ref[pl.ds(r0, chunk), lanes] * (HD_A ** -0.5)
            s_scr[hp] = _pair_scores(q_pair, kwin[pl.ds(r0, band), lanes])
        if mask_first:
            key_row = lax.broadcasted_iota(jnp.int32, (1, band), 1) + r0 + (step - 1) * prev_rows
            neg = jnp.where(key_row < 0, NEG, 0.0)
        for hp in range(n_pairs):
            s = s_scr[hp] + bias_ref[hp]
            if mask_first:
                s = s + neg
            m = jnp.max(s, axis=-1, keepdims=True)
            p_scr[hp] = jnp.exp(s - m).astype(BF16)
        for hp in range(n_pairs):
            o_ref[pl.ds(r0, chunk), LANES * hp:LANES * (hp + 1)] = _pair_values(
                p_scr[hp], vext[pl.ds(r0, band), 2 * LANES * hp:2 * LANES * (hp + 1)])
        return carry

    if n_chunks == 1:
        chunk_body(0, 0)
    else:
        lax.fori_loop(0, n_chunks, chunk_body, 0)


def _band_scratch(window_rows, chunk, band):
    n_pairs = H_A // 2
    return [pltpu.VMEM((window_rows, MIX_WIDTH), BF16),
            pltpu.VMEM((window_rows, 2 * MIX_WIDTH), BF16),
            pltpu.VMEM((n_pairs, 2 * chunk, band), F32),
            pltpu.VMEM((n_pairs, 2 * chunk, band), BF16)]


def band_attn_prompt(q, k, v, bias, block_rows):
    rows = q.shape[0]
    n_chunks = block_rows // CHUNK
    assert block_rows == PAST_BAND
    cur = pl.BlockSpec((block_rows, MIX_WIDTH), lambda i: (i, 0))
    prev = pl.BlockSpec((block_rows, MIX_WIDTH), lambda i: (jnp.maximum(i - 1, 0), 0))
    return pl.pallas_call(
        functools.partial(_band_attn_kernel, n_chunks=n_chunks, chunk=CHUNK,
                          prev_rows=PAST_BAND, mask_first=True),
        grid=(rows // block_rows,),
        in_specs=[cur, prev, cur, prev, cur,
                  pl.BlockSpec(bias.shape, lambda i: (0, 0, 0))],
        out_specs=cur,
        out_shape=jax.ShapeDtypeStruct((rows, MIX_WIDTH), F32),
        scratch_shapes=_band_scratch(PAST_BAND + block_rows, CHUNK, PAST_BAND + CHUNK),
        name="band_attn_prompt",
        compiler_params=_params("arbitrary"),
    )(q, k, k, v, v, bias)


def band_attn_sample(q, k, v, k_cache, v_cache, bias):
    b, s_len, _ = q.shape
    cache_len = k_cache.shape[1]
    new = pl.BlockSpec((None, s_len, MIX_WIDTH), lambda i: (i, 0, 0))
    old = pl.BlockSpec((None, cache_len, MIX_WIDTH), lambda i: (i, 0, 0))
    return pl.pallas_call(
        functools.partial(_band_attn_kernel, n_chunks=1, chunk=s_len,
                          prev_rows=cache_len, mask_first=False),
        grid=(b,),
        in_specs=[new, old, new, old, new, pl.BlockSpec(bias.shape, lambda i: (0, 0, 0))],
        out_specs=new,
        out_shape=jax.ShapeDtypeStruct((b, s_len, MIX_WIDTH), F32),
        scratch_shapes=_band_scratch(cache_len + s_len, s_len, cache_len + s_len),
        name="band_attn_sample",
        compiler_params=_params("arbitrary"),
    )(q, k_cache, k, v_cache, v, bias)


def _hgrn2_constants(t):
    halves = []
    h = t // 2
    while h >= 1:
        halves.append(h)
        h //= 2
    n_lvl = len(halves)
    w = np.zeros(((n_lvl + 2) * t, t), np.float32)
    masks = np.zeros((n_lvl + 1, t, t), np.float32)
    for row in range(t):
        w[row, :row + 1] = 1.0
        w[t + row, row + 1:] = 1.0
    for li, h in enumerate(halves):
        base = (2 + li) * t
        for row in range(t):
            r = (row // (2 * h)) * 2 * h + h - 1
            if row > r:
                w[base + row, r + 1:row + 1] = 1.0
            else:
                w[base + row, row + 1:r + 1] = 1.0
        for tq in range(t):
            for sk in range(t):
                if tq // (2 * h) == sk // (2 * h):
                    r = (tq // (2 * h)) * 2 * h + h - 1
                    if tq > r and sk <= r:
                        masks[li, tq, sk] = 1.0
    masks[n_lvl] = np.eye(t, dtype=np.float32)
    return jnp.asarray(w, BF16), jnp.asarray(masks, F32), n_lvl


def _hgrn2_kernel(a_ref, b_ref, c_ref, s0_ref, lbc_ref, g_ref, w_ref, m_ref,
                  o_ref, sout_ref, st_ref, *, t, n_lvl):
    ci = pl.program_id(1)

    @pl.when(ci == 0)
    def _():
        for h in range(H_B):
            st_ref[h] = s0_ref[h].T

    q = a_ref[...]
    z = b_ref[...]
    iv = c_ref[...]
    log_lb = lbc_ref[0:1, :]
    log1m_lb = lbc_ref[1:2, :]
    one_m_lb = lbc_ref[2:3, :]
    u = log1m_lb + jnp.minimum(z, 0.0) - jnp.log1p(jnp.exp(-jnp.abs(z)))
    log_f = jnp.maximum(log_lb, u) + jnp.log1p(jnp.exp(-jnp.abs(log_lb - u)))
    kk = one_m_lb / (1.0 + jnp.exp(z))

    hi = log_f.astype(BF16)
    r1 = log_f - hi.astype(F32)
    mid = r1.astype(BF16)
    lo = (r1 - mid.astype(F32)).astype(BF16)
    w = w_ref[...]
    sums = (jnp.dot(w, hi, preferred_element_type=F32)
            + jnp.dot(w, mid, preferred_element_type=F32)
            + jnp.dot(w, lo, preferred_element_type=F32))

    q_inter = (q * jnp.exp(sums[0:t])).astype(BF16)
    k_rem = (kk * jnp.exp(sums[t:2 * t])).astype(BF16)
    e_last = jnp.exp(sums[t - 1:t])
    iv_b = iv.astype(BF16)

    attn = [None] * H_B
    for li in range(n_lvl + 1):
        if li < n_lvl:
            e = jnp.exp(sums[(2 + li) * t:(3 + li) * t])
            qs = (q * e).astype(BF16)
            ks = (kk * e).astype(BF16)
        else:
            qs = q.astype(BF16)
            ks = kk.astype(BF16)
        mask = m_ref[li]
        for h in range(H_B):
            lanes = slice(LANES * h, LANES * (h + 1))
            part = mask * lax.dot_general(qs[:, lanes], ks[:, lanes], (((1,), (1,)), ((), ())),
                                          preferred_element_type=F32)
            attn[h] = part if attn[h] is None else attn[h] + part

    for h in range(H_B):
        lanes = slice(LANES * h, LANES * (h + 1))
        st = st_ref[h]
        o = jnp.dot(attn[h].astype(BF16), iv_b[:, lanes], preferred_element_type=F32)
        o = o + lax.dot_general(q_inter[:, lanes], st.astype(BF16), (((1,), (1,)), ((), ())),
                                preferred_element_type=F32)
        iv_t = iv[:, lanes].T.astype(BF16)
        st_ref[h] = e_last[:, lanes] * st + jnp.dot(iv_t, k_rem[:, lanes],
                                                    preferred_element_type=F32)
        ms = jnp.mean(o * o, axis=-1, keepdims=True)
        o_ref[:, lanes] = o * lax.rsqrt(ms + EPS) * g_ref[:, lanes]

    @pl.when(ci == pl.num_programs(1) - 1)
    def _():
        for h in range(H_B):
            sout_ref[h] = st_ref[h].T


def hgrn2(a, b, c, s0, lbc, g, t):
    bsz, rows, _ = a.shape
    w, masks, n_lvl = _hgrn2_constants(t)
    tok = pl.BlockSpec((None, t, MIX_WIDTH), lambda bi, ci: (bi, ci, 0))
    state = pl.BlockSpec((None, H_B, DK_B, DK_B), lambda bi, ci: (bi, 0, 0, 0))
    return pl.pallas_call(
        functools.partial(_hgrn2_kernel, t=t, n_lvl=n_lvl),
        grid=(bsz, rows // t),
        in_specs=[tok, tok, tok, state,
                  pl.BlockSpec(lbc.shape, lambda bi, ci: (0, 0)),
                  pl.BlockSpec((1, MIX_WIDTH), lambda bi, ci: (0, 0)),
                  pl.BlockSpec(w.shape, lambda bi, ci: (0, 0)),
                  pl.BlockSpec(masks.shape, lambda bi, ci: (0, 0, 0))],
        out_specs=[tok, state],
        out_shape=[jax.ShapeDtypeStruct((bsz, rows, MIX_WIDTH), F32),
                   jax.ShapeDtypeStruct((bsz, H_B, DK_B, DK_B), F32)],
        scratch_shapes=[pltpu.VMEM((H_B, DK_B, DK_B), F32)],
        name=f"hgrn2_t{t}",
        compiler_params=_params("arbitrary", "arbitrary"),
    )(a, b, c, s0, lbc, g.reshape(1, MIX_WIDTH), w, masks)


def _post_kernel(x_ref, mix_ref, xq_ref, gate_ref, mk_ref, mv_ref, w_ref, fg_ref, o_ref, *, final):
    gate = gate_ref[...]
    sg = gate / (1.0 + jnp.exp(-gate))
    y_mix = (mix_ref[...] * sg[:, 0:MIX_WIDTH]).astype(BF16)
    acc = x_ref[...] + jnp.dot(y_mix, w_ref[0:MIX_WIDTH, :], preferred_element_type=F32)
    mk = mk_ref[...].astype(BF16)
    mv = mv_ref[...].astype(BF16)
    ones = jnp.ones((N_MEM, LANES), BF16)
    crosses = []
    for hp in range(H_X // 2):
        lanes = slice(LANES * hp, LANES * (hp + 1))
        s = _pair_scores(xq_ref[:, lanes] * (HD_X ** -0.5), mk[:, lanes])
        p = jnp.exp(s - jnp.max(s, axis=-1, keepdims=True)).astype(BF16)
        crosses.append(_pair_values(p, jnp.concatenate([mv[:, lanes], ones], axis=1)))
    for hp, cross in enumerate(crosses):
        rows = slice(MIX_WIDTH + LANES * hp, MIX_WIDTH + LANES * (hp + 1))
        y = (cross * sg[:, rows]).astype(BF16)
        acc = acc + jnp.dot(y, w_ref[rows, :], preferred_element_type=F32)
    if final:
        ms = jnp.mean(acc * acc, axis=-1, keepdims=True)
        acc = acc * lax.rsqrt(ms + EPS) * fg_ref[...]
    o_ref[...] = acc


def post(x, mix, xq, gate, mk, mv, w_bf16, final_g, tm, final):
    bsz, rows, _ = x.shape

    def tok(n):
        return pl.BlockSpec((None, tm, n), lambda bi, i: (bi, i, 0))

    mem = pl.BlockSpec((None, N_MEM, X_WIDTH), lambda bi, i: (bi, 0, 0))
    return pl.pallas_call(
        functools.partial(_post_kernel, final=final),
        grid=(bsz, rows // tm),
        in_specs=[tok(D_MODEL), tok(MIX_WIDTH), tok(X_WIDTH), tok(D_INNER), mem, mem,
                  pl.BlockSpec((D_INNER, D_MODEL), lambda bi, i: (0, 0)),
                  pl.BlockSpec((1, D_MODEL), lambda bi, i: (0, 0))],
        out_specs=tok(D_MODEL),
        out_shape=jax.ShapeDtypeStruct((bsz, rows, D_MODEL), F32),
        name=f"post_b{bsz}",
        compiler_params=_params("arbitrary", "arbitrary"),
    )(x, mix, xq, gate, mk, mv, w_bf16, final_g.reshape(1, D_MODEL))


def _rel_bias(table):
    band = PAST_BAND + CHUNK
    n_diag = band + CHUNK - 1
    offs = np.arange(n_diag) - (CHUNK - 1)
    idx = np.clip(PAST_BAND - offs, -REL_CLIP, REL_CLIP) + REL_CLIP
    diag = jnp.pad(table[:, idx].astype(F32), ((0, 0), (0, 1)))
    skew = jnp.tile(diag, (1, CHUNK))[:, :CHUNK * n_diag].reshape(H_A, CHUNK, n_diag)
    return skew[:, :, CHUNK - 1:CHUNK - 1 + band]


def kernel(x_prompt, x_sample, cache_a_k, cache_a_v, state_b, cache_mem_k, cache_mem_v, mem_prompt,
           ln_g, w_in, w_out, rel_bias_table, lower_bounds, hgrn_norm_g, mem_norm_g, w_mem_kv, final_g):
    bp, seq, _ = x_prompt.shape
    bs, dec_seq, _ = x_sample.shape
    assert bp == 1
    cache_len = cache_a_k.shape[2]

    w_in_b = w_in.astype(BF16)
    w_out_b = w_out.astype(BF16)
    w_mem_b = w_mem_kv.astype(BF16)

    lb_all = jnp.cumsum(jax.nn.softmax(lower_bounds.astype(F32), axis=0), axis=0)
    lb_all = lb_all - lb_all[:1]

    xp = x_prompt
    xs = x_sample
    ak_p, av_p, sb_p, mk_pl, mv_pl, ak_s, av_s, sb_s = [], [], [], [], [], [], [], []
    for l in range(DEPTH):
        j = l // 2
        final = l == DEPTH - 1
        mk_p, mv_p = norm_proj(mem_prompt.reshape(N_MEM, D_MODEL), mem_norm_g[l], w_mem_b[l],
                               (X_WIDTH, X_WIDTH), N_MEM)
        a_p, b_p, c_p, xq_p, g_p = norm_proj(xp.reshape(seq, D_MODEL), ln_g[l], w_in_b[l], PROJ_SPLITS, 512)
        a_s, b_s, c_s, xq_s, g_s = norm_proj(xs.reshape(bs * dec_seq, D_MODEL), ln_g[l], w_in_b[l],
                                             PROJ_SPLITS, bs * dec_seq)

        def per_batch(u):
            return u.reshape(bs, dec_seq, u.shape[-1])

        if l % 2 == 0:
            bias = _rel_bias(rel_bias_table[j])
            band = PAST_BAND + CHUNK
            o_p = band_attn_prompt(a_p, b_p, c_p, bias.reshape(H_A // 2, 2 * CHUNK, band), PAST_BAND)
            o_s = band_attn_sample(per_batch(a_s), per_batch(b_s), per_batch(c_s),
                                   cache_a_k[j].reshape(bs, cache_len, MIX_WIDTH),
                                   cache_a_v[j].reshape(bs, cache_len, MIX_WIDTH),
                                   bias[:, :dec_seq, :cache_len + dec_seq].reshape(
                                       H_A // 2, 2 * dec_seq, cache_len + dec_seq))
            keep = min(PAST_BAND, seq)
            ak_p.append(b_p[-keep:].reshape(1, keep, H_A, HD_A))
            av_p.append(c_p[-keep:].reshape(1, keep, H_A, HD_A))
            ak_s.append(b_s.reshape(bs, dec_seq, H_A, HD_A))
            av_s.append(c_s.reshape(bs, dec_seq, H_A, HD_A))
            o_p = o_p.reshape(1, seq, MIX_WIDTH)
        else:
            lb = lb_all[j]
            lbc = jnp.stack([jnp.log(lb), jnp.log1p(-lb), 1.0 - lb])
            o_p, s_p = hgrn2(a_p.reshape(1, seq, MIX_WIDTH), b_p.reshape(1, seq, MIX_WIDTH),
                             c_p.reshape(1, seq, MIX_WIDTH),
                             jnp.zeros((1, H_B, DK_B, DK_B), F32), lbc, hgrn_norm_g[j], CHUNK)
            o_s, s_s = hgrn2(per_batch(a_s), per_batch(b_s), per_batch(c_s),
                             state_b[j].astype(F32), lbc, hgrn_norm_g[j], dec_seq)
            sb_p.append(s_p)
            sb_s.append(s_s)
        xp = post(xp, o_p, xq_p.reshape(1, seq, X_WIDTH), g_p.reshape(1, seq, D_INNER),
                  mk_p.reshape(1, N_MEM, X_WIDTH), mv_p.reshape(1, N_MEM, X_WIDTH),
                  w_out_b[l], final_g, 512, final)
        xs = post(xs, o_s, per_batch(xq_s), per_batch(g_s),
                  cache_mem_k[l].reshape(bs, N_MEM, X_WIDTH), cache_mem_v[l].reshape(bs, N_MEM, X_WIDTH),
                  w_out_b[l], final_g, dec_seq, final)
        mk_pl.append(mk_p.reshape(1, N_MEM, H_X, HD_X))
        mv_pl.append(mv_p.reshape(1, N_MEM, H_X, HD_X))
    return (xp, xs, jnp.stack(ak_p), jnp.stack(av_p), jnp.stack(sb_p), jnp.stack(mk_pl), jnp.stack(mv_pl),
            jnp.stack(ak_s), jnp.stack(av_s), jnp.stack(sb_s))
```

```python
import functools

import numpy as np
import jax
import jax.numpy as jnp
from jax import lax
from jax.experimental import pallas as pl
from jax.experimental.pallas import tpu as pltpu

D_MODEL = 1024
DEPTH = 4
CHUNK = 64
N_PAST_CHUNKS = 8
PAST_BAND = N_PAST_CHUNKS * CHUNK
MIX_WIDTH = 768
X_WIDTH = 256
D_INNER = MIX_WIDTH + X_WIDTH
HD_A = 64
H_A = MIX_WIDTH // HD_A
REL_CLIP = 128
DK_B = 128
H_B = MIX_WIDTH // DK_B
H_X = 4
HD_X = 64
N_MEM = 256
EPS = 1e-6
NEG = -1e30
LOG2_E = 1.4426950408889634
F32 = jnp.float32
BF16 = jnp.bfloat16

LANES = 128
VMEM_LIMIT_BYTES = 56 * 1024 * 1024
PROJ_SPLITS = (MIX_WIDTH, MIX_WIDTH, MIX_WIDTH, X_WIDTH, D_INNER)


def _params(*sem):
    return pltpu.CompilerParams(dimension_semantics=sem, vmem_limit_bytes=VMEM_LIMIT_BYTES)


def _norm_proj_kernel(x_ref, g_ref, w_ref, *out_refs, splits):
    x = x_ref[...]
    ms = jnp.mean(x * x, axis=-1, keepdims=True)
    xn = (x * lax.rsqrt(ms + EPS) * g_ref[...]).astype(BF16)
    off = 0
    for o_ref, n in zip(out_refs, splits):
        o_ref[...] = jnp.dot(xn, w_ref[:, off:off + n], preferred_element_type=F32)
        off += n


def norm_proj(x2d, g, w_bf16, splits, tm):
    rows, d = x2d.shape
    n_total = w_bf16.shape[1]
    return pl.pallas_call(
        functools.partial(_norm_proj_kernel, splits=splits),
        grid=(rows // tm,),
        in_specs=[
            pl.BlockSpec((tm, d), lambda i: (i, 0)),
            pl.BlockSpec((1, d), lambda i: (0, 0)),
            pl.BlockSpec((d, n_total), lambda i: (0, 0)),
        ],
        out_specs=[pl.BlockSpec((tm, n), lambda i: (i, 0)) for n in splits],
        out_shape=[jax.ShapeDtypeStruct((rows, n), F32) for n in splits],
        name=f"norm_proj_{rows}x{n_total}",
        compiler_params=_params("arbitrary"),
    )(x2d, g.reshape(1, d), w_bf16)


def _pair_scores(q_pair, k_pair):
    first = lax.broadcasted_iota(jnp.int32, q_pair.shape, 1) < HD_A
    q2 = jnp.concatenate([jnp.where(first, q_pair, 0.0), jnp.where(first, 0.0, q_pair)], axis=0)
    return lax.dot_general(q2.astype(BF16), k_pair, (((1,), (1,)), ((), ())),
                           preferred_element_type=F32)


def _pair_values(p, v_ext):
    r = jnp.dot(p, v_ext, preferred_element_type=F32)
    o = r[:, :LANES] / r[:, LANES:]
    m = o.shape[0] // 2
    first = lax.broadcasted_iota(jnp.int32, (m, LANES), 1) < HD_A
    return jnp.where(first, o[:m], o[m:])


def _band_attn_kernel(q_ref, kprev_ref, kcur_ref, vprev_ref, vcur_ref, bias_ref, o_ref,
                      kwin, vext, s_scr, p_scr, *, n_chunks, chunk, prev_rows, mask_first):
    step = pl.program_id(0)
    cur_rows = n_chunks * chunk
    band = prev_rows + chunk
    n_pairs = H_A // 2
    kwin[0:prev_rows, :] = kprev_ref[...].astype(BF16)
    kwin[prev_rows:prev_rows + cur_rows, :] = kcur_ref[...].astype(BF16)
    ones = jnp.ones((prev_rows + cur_rows, LANES), BF16)
    for hp in range(n_pairs):
        lanes = slice(LANES * hp, LANES * (hp + 1))
        vext[0:prev_rows, 2 * LANES * hp:2 * LANES * hp + LANES] = vprev_ref[:, lanes].astype(BF16)
        vext[prev_rows:prev_rows + cur_rows, 2 * LANES * hp:2 * LANES * hp + LANES] = (
            vcur_ref[:, lanes].astype(BF16))
        vext[:, 2 * LANES * hp + LANES:2 * LANES * (hp + 1)] = ones

    def chunk_body(cc, carry):
        r0 = pl.multiple_of(cc * chunk, chunk)
        for hp in range(n_pairs):
            lanes = slice(LANES * hp, LANES * (hp + 1))
            q_pair = q_ref[pl.ds(r0, chunk), lanes] * (HD_A ** -0.5)
            s_scr[hp] = _pair_scores(q_pair, kwin[pl.ds(r0, band), lanes])
        if mask_first:
            key_row = lax.broadcasted_iota(jnp.int32, (1, band), 1) + r0 + (step - 1) * prev_rows
            neg = jnp.where(key_row < 0, NEG, 0.0)
        for hp in range(n_pairs):
            s = s_scr[hp] + bias_ref[hp]
            if mask_first:
                s = s + neg
            m = jnp.max(s, axis=-1, keepdims=True)
            p_scr[hp] = jnp.exp(s - m).astype(BF16)
        for hp in range(n_pairs):
            o_ref[pl.ds(r0, chunk), LANES * hp:LANES * (hp + 1)] = _pair_values(
                p_scr[hp], vext[pl.ds(r0, band), 2 * LANES * hp:2 * LANES * (hp + 1)])
        return carry

    if n_chunks == 1:
        chunk_body(0, 0)
    else:
        lax.fori_loop(0, n_chunks, chunk_body, 0)


def _band_scratch(window_rows, chunk, band):
    n_pairs = H_A // 2
    return [pltpu.VMEM((window_rows, MIX_WIDTH), BF16),
            pltpu.VMEM((window_rows, 2 * MIX_WIDTH), BF16),
            pltpu.VMEM((n_pairs, 2 * chunk, band), F32),
            pltpu.VMEM((n_pairs, 2 * chunk, band), BF16)]


def band_attn_prompt(q, k, v, bias, block_rows):
    rows = q.shape[0]
    n_chunks = block_rows // CHUNK
    assert block_rows == PAST_BAND
    cur = pl.BlockSpec((block_rows, MIX_WIDTH), lambda i: (i, 0))
    prev = pl.BlockSpec((block_rows, MIX_WIDTH), lambda i: (jnp.maximum(i - 1, 0), 0))
    return pl.pallas_call(
        functools.partial(_band_attn_kernel, n_chunks=n_chunks, chunk=CHUNK,
                          prev_rows=PAST_BAND, mask_first=True),
        grid=(rows // block_rows,),
        in_specs=[cur, prev, cur, prev, cur,
                  pl.BlockSpec(bias.shape, lambda i: (0, 0, 0))],
        out_specs=cur,
        out_shape=jax.ShapeDtypeStruct((rows, MIX_WIDTH), F32),
        scratch_shapes=_band_scratch(PAST_BAND + block_rows, CHUNK, PAST_BAND + CHUNK),
        name="band_attn_prompt",
        compiler_params=_params("arbitrary"),
    )(q, k, k, v, v, bias)


def band_attn_sample(q, k, v, k_cache, v_cache, bias):
    b, s_len, _ = q.shape
    cache_len = k_cache.shape[1]
    new = pl.BlockSpec((None, s_len, MIX_WIDTH), lambda i: (i, 0, 0))
    old = pl.BlockSpec((None, cache_len, MIX_WIDTH), lambda i: (i, 0, 0))
    return pl.pallas_call(
        functools.partial(_band_attn_kernel, n_chunks=1, chunk=s_len,
                          prev_rows=cache_len, mask_first=False),
        grid=(b,),
        in_specs=[new, old, new, old, new, pl.BlockSpec(bias.shape, lambda i: (0, 0, 0))],
        out_specs=new,
        out_shape=jax.ShapeDtypeStruct((b, s_len, MIX_WIDTH), F32),
        scratch_shapes=_band_scratch(cache_len + s_len, s_len, cache_len + s_len),
        name="band_attn_sample",
        compiler_params=_params("arbitrary"),
    )(q, k_cache, k, v_cache, v, bias)


def _hgrn2_constants(t):
    halves = []
    h = t // 2
    while h >= 1:
        halves.append(h)
        h //= 2
    n_lvl = len(halves)
    w = np.zeros(((n_lvl + 2) * t, t), np.float32)
    masks = np.zeros((n_lvl + 1, t, t), np.float32)
    for row in range(t):
        w[row, :row + 1] = 1.0
        w[t + row, row + 1:] = 1.0
    for li, h in enumerate(halves):
        base = (2 + li) * t
        for row in range(t):
            r = (row // (2 * h)) * 2 * h + h - 1
            if row > r:
                w[base + row, r + 1:row + 1] = 1.0
            else:
                w[base + row, row + 1:r + 1] = 1.0
        for tq in range(t):
            for sk in range(t):
                if tq // (2 * h) == sk // (2 * h):
                    r = (tq // (2 * h)) * 2 * h + h - 1
                    if tq > r and sk <= r:
                        masks[li, tq, sk] = 1.0
    masks[n_lvl] = np.eye(t, dtype=np.float32)
    return jnp.asarray(np.tile(w, (1, 3)), BF16), jnp.asarray(masks, F32), n_lvl


def _hgrn2_kernel(a_ref, b_ref, c_ref, s0_ref, lbc_ref, g_ref, w_ref, m_ref,
                  o_ref, sout_ref,
                  st_ref, lf_scr, kk_scr, sums_scr, qs_scr, ks_scr, attn_scr, u_scr,
                  *, t, n_lvl, n_blk):
    ci = pl.program_id(1)
    inter = n_lvl + 1

    @pl.when(ci == 0)
    def _():
        for h in range(H_B):
            st_ref[h] = s0_ref[h].T

    def blk(g):
        return slice(g * t, (g + 1) * t)

    def head(h):
        return slice(LANES * h, LANES * (h + 1))

    z = b_ref[...]
    log_lb = lbc_ref[0:1, :]
    u = lbc_ref[1:2, :] + jnp.minimum(z, 0.0) - jnp.log(1.0 + jnp.exp(-jnp.abs(z)))
    log_f = jnp.maximum(log_lb, u) + jnp.log(1.0 + jnp.exp(-jnp.abs(log_lb - u)))
    kk_scr[...] = lbc_ref[2:3, :] / (1.0 + jnp.exp(z))
    log2_f = log_f * LOG2_E
    hi = log2_f.astype(BF16)
    r1 = log2_f - hi.astype(F32)
    mid = r1.astype(BF16)
    lo = (r1 - mid.astype(F32)).astype(BF16)
    for g in range(n_blk):
        lf_scr[g, 0:t] = hi[blk(g)]
        lf_scr[g, t:2 * t] = mid[blk(g)]
        lf_scr[g, 2 * t:3 * t] = lo[blk(g)]

    for g in range(n_blk):
        sums_scr[g] = jnp.dot(w_ref[...], lf_scr[g], preferred_element_type=F32)

    for g in range(n_blk):
        q = a_ref[blk(g), :].astype(BF16)
        kk = kk_scr[blk(g), :].astype(BF16)
        for li in range(n_lvl):
            e = jnp.exp2(sums_scr[g, (2 + li) * t:(3 + li) * t]).astype(BF16)
            qs_scr[g, li] = q * e
            ks_scr[g, li] = kk * e
        qs_scr[g, n_lvl] = q
        ks_scr[g, n_lvl] = kk
        qs_scr[g, inter] = q * jnp.exp2(sums_scr[g, 0:t]).astype(BF16)
        ks_scr[g, inter] = kk * jnp.exp2(sums_scr[g, t:2 * t]).astype(BF16)

    in_level = [m_ref[li] != 0.0 for li in range(n_lvl + 1)]
    for g in range(n_blk):
        for h in range(H_B):
            acc = jnp.zeros((t, t), F32)
            for li in range(n_lvl + 1):
                part = lax.dot_general(
                    qs_scr[g, li, :, head(h)], ks_scr[g, li, :, head(h)],
                    (((1,), (1,)), ((), ())), preferred_element_type=F32)
                acc = jnp.where(in_level[li], part, acc)
            attn_scr[g, h] = acc.astype(BF16)

    for g in range(n_blk):
        for h in range(H_B):
            iv = c_ref[blk(g), head(h)]
            o_ref[blk(g), head(h)] = jnp.dot(attn_scr[g, h], iv.astype(BF16),
                                             preferred_element_type=F32)
            u_scr[g, h] = jnp.dot(iv.T.astype(BF16), ks_scr[g, inter, :, head(h)],
                                  preferred_element_type=F32)

    for g in range(n_blk):
        e_last = jnp.exp2(sums_scr[g, t - 1:t, :])
        for h in range(H_B):
            st = st_ref[h]
            o = o_ref[blk(g), head(h)] + lax.dot_general(
                qs_scr[g, inter, :, head(h)], st.astype(BF16), (((1,), (1,)), ((), ())),
                preferred_element_type=F32)
            st_ref[h] = e_last[:, head(h)] * st + u_scr[g, h]
            ms = jnp.mean(o * o, axis=-1, keepdims=True)
            o_ref[blk(g), head(h)] = o * lax.rsqrt(ms + EPS) * g_ref[:, head(h)]

    @pl.when(ci == pl.num_programs(1) - 1)
    def _():
        for h in range(H_B):
            sout_ref[h] = st_ref[h].T


def hgrn2(a, b, c, s0, lbc, g, t, n_blk):
    bsz, rows, _ = a.shape
    w, masks, n_lvl = _hgrn2_constants(t)
    step_rows = n_blk * t
    tok = pl.BlockSpec((None, step_rows, MIX_WIDTH), lambda bi, ci: (bi, ci, 0))
    state = pl.BlockSpec((None, H_B, DK_B, DK_B), lambda bi, ci: (bi, 0, 0, 0))
    scratch = [pltpu.VMEM((H_B, DK_B, DK_B), F32),
               pltpu.VMEM((n_blk, 3 * t, MIX_WIDTH), BF16),
               pltpu.VMEM((step_rows, MIX_WIDTH), F32),
               pltpu.VMEM((n_blk, (n_lvl + 2) * t, MIX_WIDTH), F32),
               pltpu.VMEM((n_blk, n_lvl + 2, t, MIX_WIDTH), BF16),
               pltpu.VMEM((n_blk, n_lvl + 2, t, MIX_WIDTH), BF16),
               pltpu.VMEM((n_blk, H_B, t, t), BF16),
               pltpu.VMEM((n_blk, H_B, DK_B, DK_B), F32)]
    return pl.pallas_call(
        functools.partial(_hgrn2_kernel, t=t, n_lvl=n_lvl, n_blk=n_blk),
        grid=(bsz, rows // step_rows),
        in_specs=[tok, tok, tok, state,
                  pl.BlockSpec(lbc.shape, lambda bi, ci: (0, 0)),
                  pl.BlockSpec((1, MIX_WIDTH), lambda bi, ci: (0, 0)),
                  pl.BlockSpec(w.shape, lambda bi, ci: (0, 0)),
                  pl.BlockSpec(masks.shape, lambda bi, ci: (0, 0, 0))],
        out_specs=[tok, state],
        out_shape=[jax.ShapeDtypeStruct((bsz, rows, MIX_WIDTH), F32),
                   jax.ShapeDtypeStruct((bsz, H_B, DK_B, DK_B), F32)],
        scratch_shapes=scratch,
        name=f"hgrn2_t{t}",
        compiler_params=_params("arbitrary", "arbitrary"),
    )(a, b, c, s0, lbc, g.reshape(1, MIX_WIDTH), w, masks)


def _post_kernel(x_ref, mix_ref, xq_ref, gate_ref, mk_ref, mv_ref, w_ref, fg_ref, o_ref, *, final):
    gate = gate_ref[...]
    sg = gate / (1.0 + jnp.exp(-gate))
    y_mix = (mix_ref[...] * sg[:, 0:MIX_WIDTH]).astype(BF16)
    acc = x_ref[...] + jnp.dot(y_mix, w_ref[0:MIX_WIDTH, :], preferred_element_type=F32)
    mk = mk_ref[...].astype(BF16)
    mv = mv_ref[...].astype(BF16)
    ones = jnp.ones((N_MEM, LANES), BF16)
    crosses = []
    for hp in range(H_X // 2):
        lanes = slice(LANES * hp, LANES * (hp + 1))
        s = _pair_scores(xq_ref[:, lanes] * (HD_X ** -0.5), mk[:, lanes])
        p = jnp.exp(s - jnp.max(s, axis=-1, keepdims=True)).astype(BF16)
        crosses.append(_pair_values(p, jnp.concatenate([mv[:, lanes], ones], axis=1)))
    for hp, cross in enumerate(crosses):
        rows = slice(MIX_WIDTH + LANES * hp, MIX_WIDTH + LANES * (hp + 1))
        y = (cross * sg[:, rows]).astype(BF16)
        acc = acc + jnp.dot(y, w_ref[rows, :], preferred_element_type=F32)
    if final:
        ms = jnp.mean(acc * acc, axis=-1, keepdims=True)
        acc = acc * lax.rsqrt(ms + EPS) * fg_ref[...]
    o_ref[...] = acc


def post(x, mix, xq, gate, mk, mv, w_bf16, final_g, tm, final):
    bsz, rows, _ = x.shape

    def tok(n):
        return pl.BlockSpec((None, tm, n), lambda bi, i: (bi, i, 0))

    mem = pl.BlockSpec((None, N_MEM, X_WIDTH), lambda bi, i: (bi, 0, 0))
    return pl.pallas_call(
        functools.partial(_post_kernel, final=final),
        grid=(bsz, rows // tm),
        in_specs=[tok(D_MODEL), tok(MIX_WIDTH), tok(X_WIDTH), tok(D_INNER), mem, mem,
                  pl.BlockSpec((D_INNER, D_MODEL), lambda bi, i: (0, 0)),
                  pl.BlockSpec((1, D_MODEL), lambda bi, i: (0, 0))],
        out_specs=tok(D_MODEL),
        out_shape=jax.ShapeDtypeStruct((bsz, rows, D_MODEL), F32),
        name=f"post_b{bsz}",
        compiler_params=_params("arbitrary", "arbitrary"),
    )(x, mix, xq, gate, mk, mv, w_bf16, final_g.reshape(1, D_MODEL))


def _rel_bias(table):
    band = PAST_BAND + CHUNK
    n_diag = band + CHUNK - 1
    offs = np.arange(n_diag) - (CHUNK - 1)
    idx = np.clip(PAST_BAND - offs, -REL_CLIP, REL_CLIP) + REL_CLIP
    diag = jnp.pad(table[:, idx].astype(F32), ((0, 0), (0, 1)))
    skew = jnp.tile(diag, (1, CHUNK))[:, :CHUNK * n_diag].reshape(H_A, CHUNK, n_diag)
    return skew[:, :, CHUNK - 1:CHUNK - 1 + band]


def kernel(x_prompt, x_sample, cache_a_k, cache_a_v, state_b, cache_mem_k, cache_mem_v, mem_prompt,
           ln_g, w_in, w_out, rel_bias_table, lower_bounds, hgrn_norm_g, mem_norm_g, w_mem_kv, final_g):
    bp, seq, _ = x_prompt.shape
    bs, dec_seq, _ = x_sample.shape
    assert bp == 1
    cache_len = cache_a_k.shape[2]

    w_in_b = w_in.astype(BF16)
    w_out_b = w_out.astype(BF16)
    w_mem_b = w_mem_kv.astype(BF16)

    lb_all = jnp.cumsum(jax.nn.softmax(lower_bounds.astype(F32), axis=0), axis=0)
    lb_all = lb_all - lb_all[:1]

    xp = x_prompt
    xs = x_sample
    ak_p, av_p, sb_p, mk_pl, mv_pl, ak_s, av_s, sb_s = [], [], [], [], [], [], [], []
    for l in range(DEPTH):
        j = l // 2
        final = l == DEPTH - 1
        mk_p, mv_p = norm_proj(mem_prompt.reshape(N_MEM, D_MODEL), mem_norm_g[l], w_mem_b[l],
                               (X_WIDTH, X_WIDTH), N_MEM)
        a_p, b_p, c_p, xq_p, g_p = norm_proj(xp.reshape(seq, D_MODEL), ln_g[l], w_in_b[l], PROJ_SPLITS, 512)
        a_s, b_s, c_s, xq_s, g_s = norm_proj(xs.reshape(bs * dec_seq, D_MODEL), ln_g[l], w_in_b[l],
                                             PROJ_SPLITS, bs * dec_seq)

        def per_batch(u):
            return u.reshape(bs, dec_seq, u.shape[-1])

        if l % 2 == 0:
            bias = _rel_bias(rel_bias_table[j])
            band = PAST_BAND + CHUNK
            o_p = band_attn_prompt(a_p, b_p, c_p, bias.reshape(H_A // 2, 2 * CHUNK, band), PAST_BAND)
            o_s = band_attn_sample(per_batch(a_s), per_batch(b_s), per_batch(c_s),
                                   cache_a_k[j].reshape(bs, cache_len, MIX_WIDTH),
                                   cache_a_v[j].reshape(bs, cache_len, MIX_WIDTH),
                                   bias[:, :dec_seq, :cache_len + dec_seq].reshape(
                                       H_A // 2, 2 * dec_seq, cache_len + dec_seq))
            keep = min(PAST_BAND, seq)
            ak_p.append(b_p[-keep:].reshape(1, keep, H_A, HD_A))
            av_p.append(c_p[-keep:].reshape(1, keep, H_A, HD_A))
            ak_s.append(b_s.reshape(bs, dec_seq, H_A, HD_A))
            av_s.append(c_s.reshape(bs, dec_seq, H_A, HD_A))
            o_p = o_p.reshape(1, seq, MIX_WIDTH)
        else:
            lb = lb_all[j]
            lbc = jnp.stack([jnp.log(lb), jnp.log1p(-lb), 1.0 - lb])
            o_p, s_p = hgrn2(a_p.reshape(1, seq, MIX_WIDTH), b_p.reshape(1, seq, MIX_WIDTH),
                             c_p.reshape(1, seq, MIX_WIDTH),
                             jnp.zeros((1, H_B, DK_B, DK_B), F32), lbc, hgrn_norm_g[j], CHUNK, 4)
            o_s, s_s = hgrn2(per_batch(a_s), per_batch(b_s), per_batch(c_s),
                             state_b[j].astype(F32), lbc, hgrn_norm_g[j], dec_seq, 1)
            sb_p.append(s_p)
            sb_s.append(s_s)
        xp = post(xp, o_p, xq_p.reshape(1, seq, X_WIDTH), g_p.reshape(1, seq, D_INNER),
                  mk_p.reshape(1, N_MEM, X_WIDTH), mv_p.reshape(1, N_MEM, X_WIDTH),
                  w_out_b[l], final_g, 512, final)
        xs = post(xs, o_s, per_batch(xq_s), per_batch(g_s),
                  cache_mem_k[l].reshape(bs, N_MEM, X_WIDTH), cache_mem_v[l].reshape(bs, N_MEM, X_WIDTH),
                  w_out_b[l], final_g, dec_seq, final)
        mk_pl.append(mk_p.reshape(1, N_MEM, H_X, HD_X))
        mv_pl.append(mv_p.reshape(1, N_MEM, H_X, HD_X))
    return (xp, xs, jnp.stack(ak_p), jnp.stack(av_p), jnp.stack(sb_p), jnp.stack(mk_pl), jnp.stack(mv_pl),
            jnp.stack(ak_s), jnp.stack(av_s), jnp.stack(sb_s))
```

```python
import functools

import numpy as np
import jax
import jax.numpy as jnp
from jax import lax
from jax.experimental import pallas as pl
from jax.experimental.pallas import tpu as pltpu

D_MODEL = 1024
DEPTH = 4
CHUNK = 64
N_PAST_CHUNKS = 8
PAST_BAND = N_PAST_CHUNKS * CHUNK
MIX_WIDTH = 768
X_WIDTH = 256
D_INNER = MIX_WIDTH + X_WIDTH
HD_A = 64
H_A = MIX_WIDTH // HD_A
REL_CLIP = 128
DK_B = 128
H_B = MIX_WIDTH // DK_B
H_X = 4
HD_X = 64
N_MEM = 256
EPS = 1e-6
NEG = -1e30
LOG2_E = 1.4426950408889634
F32 = jnp.float32
BF16 = jnp.bfloat16

LANES = 128
VMEM_LIMIT_BYTES = 56 * 1024 * 1024
PROJ_SPLITS = (MIX_WIDTH, MIX_WIDTH, MIX_WIDTH, X_WIDTH, D_INNER)
CONTRACT_LAST = (((1,), (1,)), ((), ()))


def _params(*sem):
    return pltpu.CompilerParams(dimension_semantics=sem, vmem_limit_bytes=VMEM_LIMIT_BYTES)


def _norm_proj_kernel(x_ref, g_ref, w_ref, *out_refs, splits, tails, transpose):
    x = x_ref[...]
    ms = jnp.mean(x * x, axis=-1, keepdims=True)
    xn = (x * lax.rsqrt(ms + EPS) * g_ref[...]).astype(BF16)
    tail_refs = out_refs[len(splits):]
    off = 0
    for idx, (o_ref, n) in enumerate(zip(out_refs, splits)):
        r = jnp.dot(xn, w_ref[:, off:off + n], preferred_element_type=F32)
        o_ref[...] = (r.T if transpose else r).astype(o_ref.dtype)
        if idx in tails:
            tail_refs[tails.index(idx)][...] = r
        off += n


def norm_proj(x2d, g, w_bf16, splits, dtypes, tm, tails=(), transpose=False):
    rows, d = x2d.shape
    n_total = w_bf16.shape[1]
    assert rows % tm == 0 and (not transpose or rows == tm)
    if transpose:
        out_specs = [pl.BlockSpec((n, tm), lambda i: (0, 0)) for n in splits]
        out_shape = [jax.ShapeDtypeStruct((n, rows), dt) for n, dt in zip(splits, dtypes)]
    else:
        out_specs = [pl.BlockSpec((tm, n), lambda i: (i, 0)) for n in splits]
        out_shape = [jax.ShapeDtypeStruct((rows, n), dt) for n, dt in zip(splits, dtypes)]
    out_specs += [pl.BlockSpec((tm, splits[idx]), lambda i: (0, 0)) for idx in tails]
    out_shape += [jax.ShapeDtypeStruct((tm, splits[idx]), F32) for idx in tails]
    return pl.pallas_call(
        functools.partial(_norm_proj_kernel, splits=splits, tails=tuple(tails), transpose=transpose),
        grid=(rows // tm,),
        in_specs=[
            pl.BlockSpec((tm, d), lambda i: (i, 0)),
            pl.BlockSpec((1, d), lambda i: (0, 0)),
            pl.BlockSpec((d, n_total), lambda i: (0, 0)),
        ],
        out_specs=out_specs,
        out_shape=out_shape,
        name=f"norm_proj_{rows}x{n_total}",
        compiler_params=_params("arbitrary"),
    )(x2d, g.reshape(1, d), w_bf16)


def _stack_pair(q_pair):
    first = lax.broadcasted_iota(jnp.int32, (1, LANES), 1) < HD_A
    keep0 = jnp.where(first, 1.0, 0.0).astype(BF16)
    keep1 = jnp.where(first, 0.0, 1.0).astype(BF16)
    return jnp.concatenate([q_pair * keep0, q_pair * keep1], axis=0)


def _unstack_pair(o2):
    m = o2.shape[0] // 2
    first = lax.broadcasted_iota(jnp.int32, (m, LANES), 1) < HD_A
    return jnp.where(first, o2[:m], o2[m:])


def _pair_values(p, v_ext):
    r = jnp.dot(p, v_ext, preferred_element_type=F32)
    return _unstack_pair(r[:, :LANES] / r[:, LANES:])


def _pair_values_t(p, vt_ext):
    r = lax.dot_general(p, vt_ext, CONTRACT_LAST, preferred_element_type=F32)
    return _unstack_pair(r[:, :LANES] / r[:, LANES:])


def _band_attn_kernel(q_ref, kprev_ref, kcur_ref, vprev_ref, vcur_ref, bias_ref, o_ref,
                      kwin, vext, s_scr, p_scr, *, n_chunks):
    step = pl.program_id(0)
    chunk, prev_rows = CHUNK, PAST_BAND
    cur_rows = n_chunks * chunk
    band = prev_rows + chunk
    n_pairs = H_A // 2
    kwin[0:prev_rows, :] = kprev_ref[...]
    kwin[prev_rows:prev_rows + cur_rows, :] = kcur_ref[...]
    ones = jnp.ones((prev_rows + cur_rows, LANES), BF16)
    for hp in range(n_pairs):
        lanes = slice(LANES * hp, LANES * (hp + 1))
        vext[0:prev_rows, 2 * LANES * hp:2 * LANES * hp + LANES] = vprev_ref[:, lanes]
        vext[prev_rows:prev_rows + cur_rows, 2 * LANES * hp:2 * LANES * hp + LANES] = vcur_ref[:, lanes]
        vext[:, 2 * LANES * hp + LANES:2 * LANES * (hp + 1)] = ones

    def chunk_body(cc, carry):
        r0 = pl.multiple_of(cc * chunk, chunk)
        for hp in range(n_pairs):
            lanes = slice(LANES * hp, LANES * (hp + 1))
            q2 = _stack_pair(q_ref[pl.ds(r0, chunk), lanes] * (HD_A ** -0.5))
            s_scr[hp] = lax.dot_general(q2, kwin[pl.ds(r0, band), lanes], CONTRACT_LAST,
                                        preferred_element_type=F32)
        key_row = lax.broadcasted_iota(jnp.int32, (1, band), 1) + r0 + (step - 1) * prev_rows
        neg = jnp.where(key_row < 0, NEG, 0.0)
        for hp in range(n_pairs):
            s = s_scr[hp] + bias_ref[hp] + neg
            m = jnp.max(s, axis=-1, keepdims=True)
            p_scr[hp] = jnp.exp(s - m).astype(BF16)
        for hp in range(n_pairs):
            o_ref[pl.ds(r0, chunk), LANES * hp:LANES * (hp + 1)] = _pair_values(
                p_scr[hp], vext[pl.ds(r0, band), 2 * LANES * hp:2 * LANES * (hp + 1)]
            ).astype(o_ref.dtype)
        return carry

    lax.fori_loop(0, n_chunks, chunk_body, 0)


def band_attn_prompt(q, k, v, bias, block_rows):
    rows = q.shape[0]
    n_chunks = block_rows // CHUNK
    n_pairs = H_A // 2
    band = PAST_BAND + CHUNK
    assert block_rows == PAST_BAND
    cur = pl.BlockSpec((block_rows, MIX_WIDTH), lambda i: (i, 0))
    prev = pl.BlockSpec((block_rows, MIX_WIDTH), lambda i: (jnp.maximum(i - 1, 0), 0))
    return pl.pallas_call(
        functools.partial(_band_attn_kernel, n_chunks=n_chunks),
        grid=(rows // block_rows,),
        in_specs=[cur, prev, cur, prev, cur,
                  pl.BlockSpec(bias.shape, lambda i: (0, 0, 0))],
        out_specs=cur,
        out_shape=jax.ShapeDtypeStruct((rows, MIX_WIDTH), BF16),
        scratch_shapes=[pltpu.VMEM((PAST_BAND + block_rows, MIX_WIDTH), BF16),
                        pltpu.VMEM((PAST_BAND + block_rows, 2 * MIX_WIDTH), BF16),
                        pltpu.VMEM((n_pairs, 2 * CHUNK, band), F32),
                        pltpu.VMEM((n_pairs, 2 * CHUNK, band), BF16)],
        name="band_attn_prompt",
        compiler_params=_params("arbitrary"),
    )(q, k, k, v, v, bias)


def _band_sample_kernel(q_ref, k_ref, v_ref, kct_ref, vct_ref, bias_ref, o_ref, *, cache_len):
    for hp in range(H_A // 2):
        lanes = slice(LANES * hp, LANES * (hp + 1))
        q2 = _stack_pair(q_ref[:, lanes] * (HD_A ** -0.5))
        kct = kct_ref[2 * hp:2 * hp + 2].reshape(LANES, cache_len).astype(BF16)
        vct = vct_ref[2 * hp:2 * hp + 2].reshape(LANES, cache_len).astype(BF16)
        s_old = jnp.dot(q2, kct, preferred_element_type=F32) + bias_ref[hp, :, 0:cache_len]
        s_new = lax.dot_general(q2, k_ref[:, lanes], CONTRACT_LAST,
                                preferred_element_type=F32) + bias_ref[hp, :, cache_len:]
        m = jnp.maximum(jnp.max(s_old, axis=-1, keepdims=True),
                        jnp.max(s_new, axis=-1, keepdims=True))
        p_old = jnp.exp(s_old - m)
        p_new = jnp.exp(s_new - m)
        l = jnp.sum(p_old, axis=-1, keepdims=True) + jnp.sum(p_new, axis=-1, keepdims=True)
        o2 = (lax.dot_general(p_old.astype(BF16), vct, CONTRACT_LAST, preferred_element_type=F32)
              + jnp.dot(p_new.astype(BF16), v_ref[:, lanes], preferred_element_type=F32))
        o_ref[:, lanes] = _unstack_pair(o2 / l).astype(o_ref.dtype)


def band_attn_sample(q, k, v, k_cache_t, v_cache_t, layer, bias):
    b, s_len, _ = q.shape
    cache_len = k_cache_t.shape[-1]
    new = pl.BlockSpec((None, s_len, MIX_WIDTH), lambda i: (i, 0, 0))
    old = pl.BlockSpec((None, None, H_A, HD_A, cache_len), lambda i: (layer, i, 0, 0, 0))
    return pl.pallas_call(
        functools.partial(_band_sample_kernel, cache_len=cache_len),
        grid=(b,),
        in_specs=[new, new, new, old, old, pl.BlockSpec(bias.shape, lambda i: (0, 0, 0))],
        out_specs=new,
        out_shape=jax.ShapeDtypeStruct((b, s_len, MIX_WIDTH), BF16),
        name="band_attn_sample",
        compiler_params=_params("arbitrary"),
    )(q, k, v, k_cache_t, v_cache_t, bias)


def _hgrn2_constants(t):
    halves = []
    h = t // 2
    while h >= 1:
        halves.append(h)
        h //= 2
    n_lvl = len(halves)
    w = np.zeros(((n_lvl + 2) * t, t), np.float32)
    masks = np.zeros((n_lvl + 1, t, t), np.float32)
    for row in range(t):
        w[row, :row + 1] = 1.0
        w[t + row, row + 1:] = 1.0
    for li, h in enumerate(halves):
        base = (2 + li) * t
        for row in range(t):
            r = (row // (2 * h)) * 2 * h + h - 1
            if row > r:
                w[base + row, r + 1:row + 1] = 1.0
            else:
                w[base + row, row + 1:r + 1] = 1.0
        for tq in range(t):
            for sk in range(t):
                if tq // (2 * h) == sk // (2 * h):
                    r = (tq // (2 * h)) * 2 * h + h - 1
                    if tq > r and sk <= r:
                        masks[li, tq, sk] = 1.0
    masks[n_lvl] = np.eye(t, dtype=np.float32)
    return jnp.asarray(np.tile(w, (1, 3)), BF16), jnp.asarray(masks, F32), n_lvl


def _hgrn2_kernel(a_ref, b_ref, c_ref, s0_ref, lbc_ref, g_ref, w_ref, m_ref,
                  o_ref, sout_ref,
                  st_ref, lf_scr, kk_scr, sums_scr, qs_scr, ks_scr, attn_scr, u_scr, oi_scr,
                  *, t, n_lvl, n_blk):
    ci = pl.program_id(1)
    inter = n_lvl + 1

    @pl.when(ci == 0)
    def _():
        for h in range(H_B):
            st_ref[h] = s0_ref[h].T

    def blk(g):
        return slice(g * t, (g + 1) * t)

    def head(h):
        return slice(LANES * h, LANES * (h + 1))

    z = b_ref[...]
    log_lb = lbc_ref[0:1, :]
    u = lbc_ref[1:2, :] + jnp.minimum(z, 0.0) - jnp.log(1.0 + jnp.exp(-jnp.abs(z)))
    log_f = jnp.maximum(log_lb, u) + jnp.log(1.0 + jnp.exp(-jnp.abs(log_lb - u)))
    kk_scr[...] = lbc_ref[2:3, :] / (1.0 + jnp.exp(z))
    log2_f = log_f * LOG2_E
    hi = log2_f.astype(BF16)
    r1 = log2_f - hi.astype(F32)
    mid = r1.astype(BF16)
    lo = (r1 - mid.astype(F32)).astype(BF16)
    for g in range(n_blk):
        lf_scr[g, 0:t] = hi[blk(g)]
        lf_scr[g, t:2 * t] = mid[blk(g)]
        lf_scr[g, 2 * t:3 * t] = lo[blk(g)]

    for g in range(n_blk):
        sums_scr[g] = jnp.dot(w_ref[...], lf_scr[g], preferred_element_type=F32)

    for g in range(n_blk):
        q = a_ref[blk(g), :]
        kk = kk_scr[blk(g), :].astype(BF16)
        for li in range(n_lvl):
            e = jnp.exp2(sums_scr[g, (2 + li) * t:(3 + li) * t]).astype(BF16)
            qs_scr[g, li] = q * e
            ks_scr[g, li] = kk * e
        qs_scr[g, n_lvl] = q
        ks_scr[g, n_lvl] = kk
        qs_scr[g, inter] = q * jnp.exp2(sums_scr[g, 0:t]).astype(BF16)
        ks_scr[g, inter] = kk * jnp.exp2(sums_scr[g, t:2 * t]).astype(BF16)

    in_level = [m_ref[li] != 0.0 for li in range(n_lvl + 1)]
    for g in range(n_blk):
        for h in range(H_B):
            acc = jnp.zeros((t, t), F32)
            for li in range(n_lvl + 1):
                part = lax.dot_general(
                    qs_scr[g, li, :, head(h)], ks_scr[g, li, :, head(h)],
                    CONTRACT_LAST, preferred_element_type=F32)
                acc = jnp.where(in_level[li], part, acc)
            attn_scr[g, h] = acc.astype(BF16)

    for g in range(n_blk):
        for h in range(H_B):
            iv = c_ref[blk(g), head(h)]
            oi_scr[blk(g), head(h)] = jnp.dot(attn_scr[g, h], iv, preferred_element_type=F32)
            u_scr[g, h] = jnp.dot(iv.astype(F32).T.astype(BF16), ks_scr[g, inter, :, head(h)],
                                  preferred_element_type=F32)

    for g in range(n_blk):
        e_last = jnp.exp2(sums_scr[g, t - 1:t, :])
        for h in range(H_B):
            st = st_ref[h]
            o = oi_scr[blk(g), head(h)] + lax.dot_general(
                qs_scr[g, inter, :, head(h)], st.astype(BF16), CONTRACT_LAST,
                preferred_element_type=F32)
            st_ref[h] = e_last[:, head(h)] * st + u_scr[g, h]
            ms = jnp.mean(o * o, axis=-1, keepdims=True)
            o_ref[blk(g), head(h)] = (o * lax.rsqrt(ms + EPS) * g_ref[:, head(h)]).astype(o_ref.dtype)

    @pl.when(ci == pl.num_programs(1) - 1)
    def _():
        for h in range(H_B):
            sout_ref[h] = st_ref[h].T


def hgrn2(a, b, c, s0, layer, lbc, g, t, n_blk):
    bsz, rows, _ = a.shape
    w, masks, n_lvl = _hgrn2_constants(t)
    step_rows = n_blk * t
    tok = pl.BlockSpec((None, step_rows, MIX_WIDTH), lambda bi, ci: (bi, ci, 0))
    state = pl.BlockSpec((None, H_B, DK_B, DK_B), lambda bi, ci: (bi, 0, 0, 0))
    state_in = pl.BlockSpec((None, None, H_B, DK_B, DK_B), lambda bi, ci: (layer, bi, 0, 0, 0))
    scratch = [pltpu.VMEM((H_B, DK_B, DK_B), F32),
               pltpu.VMEM((n_blk, 3 * t, MIX_WIDTH), BF16),
               pltpu.VMEM((step_rows, MIX_WIDTH), F32),
               pltpu.VMEM((n_blk, (n_lvl + 2) * t, MIX_WIDTH), F32),
               pltpu.VMEM((n_blk, n_lvl + 2, t, MIX_WIDTH), BF16),
               pltpu.VMEM((n_blk, n_lvl + 2, t, MIX_WIDTH), BF16),
               pltpu.VMEM((n_blk, H_B, t, t), BF16),
               pltpu.VMEM((n_blk, H_B, DK_B, DK_B), F32),
               pltpu.VMEM((step_rows, MIX_WIDTH), F32)]
    return pl.pallas_call(
        functools.partial(_hgrn2_kernel, t=t, n_lvl=n_lvl, n_blk=n_blk),
        grid=(bsz, rows // step_rows),
        in_specs=[tok, tok, tok, state_in,
                  pl.BlockSpec(lbc.shape, lambda bi, ci: (0, 0)),
                  pl.BlockSpec((1, MIX_WIDTH), lambda bi, ci: (0, 0)),
                  pl.BlockSpec(w.shape, lambda bi, ci: (0, 0)),
                  pl.BlockSpec(masks.shape, lambda bi, ci: (0, 0, 0))],
        out_specs=[tok, state],
        out_shape=[jax.ShapeDtypeStruct((bsz, rows, MIX_WIDTH), BF16),
                   jax.ShapeDtypeStruct((bsz, H_B, DK_B, DK_B), F32)],
        scratch_shapes=scratch,
        name=f"hgrn2_t{t}",
        compiler_params=_params("arbitrary", "arbitrary"),
    )(a, b, c, s0, lbc, g.reshape(1, MIX_WIDTH), w, masks)


def _post_kernel(x_ref, mix_ref, xq_ref, gate_ref, mkt_ref, mvt_ref, w_ref, fg_ref, o_ref, *, final):
    gate = gate_ref[...]
    sg = gate / (1.0 + jnp.exp(-gate))
    y_mix = (mix_ref[...].astype(F32) * sg[:, 0:MIX_WIDTH]).astype(BF16)
    acc = x_ref[...] + jnp.dot(y_mix, w_ref[0:MIX_WIDTH, :], preferred_element_type=F32)
    ones = jnp.ones((LANES, N_MEM), BF16)
    crosses = []
    for hp in range(H_X // 2):
        lanes = slice(LANES * hp, LANES * (hp + 1))
        q2 = _stack_pair(xq_ref[:, lanes] * (HD_X ** -0.5))
        s = jnp.dot(q2, mkt_ref[lanes, :].astype(BF16), preferred_element_type=F32)
        p = jnp.exp(s - jnp.max(s, axis=-1, keepdims=True)).astype(BF16)
        crosses.append(_pair_values_t(
            p, jnp.concatenate([mvt_ref[lanes, :].astype(BF16), ones], axis=0)))
    for hp, cross in enumerate(crosses):
        rows = slice(MIX_WIDTH + LANES * hp, MIX_WIDTH + LANES * (hp + 1))
        y = (cross * sg[:, rows]).astype(BF16)
        acc = acc + jnp.dot(y, w_ref[rows, :], preferred_element_type=F32)
    if final:
        ms = jnp.mean(acc * acc, axis=-1, keepdims=True)
        acc = acc * lax.rsqrt(ms + EPS) * fg_ref[...]
    o_ref[...] = acc


def post(x, mix, xq, gate, mkt, mvt, layer, w_bf16, final_g, tm, final):
    bsz, rows, _ = x.shape

    def tok(n):
        return pl.BlockSpec((None, tm, n), lambda bi, i: (bi, i, 0))

    mem = pl.BlockSpec((None, None, X_WIDTH, N_MEM), lambda bi, i: (layer, bi, 0, 0))
    return pl.pallas_call(
        functools.partial(_post_kernel, final=final),
        grid=(bsz, rows // tm),
        in_specs=[tok(D_MODEL), tok(MIX_WIDTH), tok(X_WIDTH), tok(D_INNER), mem, mem,
                  pl.BlockSpec((D_INNER, D_MODEL), lambda bi, i: (0, 0)),
                  pl.BlockSpec((1, D_MODEL), lambda bi, i: (0, 0))],
        out_specs=tok(D_MODEL),
        out_shape=jax.ShapeDtypeStruct((bsz, rows, D_MODEL), F32),
        name=f"post_b{bsz}",
        compiler_params=_params("arbitrary", "arbitrary"),
    )(x, mix, xq, gate, mkt, mvt, w_bf16, final_g.reshape(1, D_MODEL))


def _rel_bias(table):
    band = PAST_BAND + CHUNK
    n_diag = band + CHUNK - 1
    offs = np.arange(n_diag) - (CHUNK - 1)
    idx = np.clip(PAST_BAND - offs, -REL_CLIP, REL_CLIP) + REL_CLIP
    diag = jnp.pad(table[:, idx].astype(F32), ((0, 0), (0, 1)))
    skew = jnp.tile(diag, (1, CHUNK))[:, :CHUNK * n_diag].reshape(H_A, CHUNK, n_diag)
    return skew[:, :, CHUNK - 1:CHUNK - 1 + band]


def _per_head_transposed(cache):
    return jnp.moveaxis(cache, -3, -1)


def kernel(x_prompt, x_sample, cache_a_k, cache_a_v, state_b, cache_mem_k, cache_mem_v, mem_prompt,
           ln_g, w_in, w_out, rel_bias_table, lower_bounds, hgrn_norm_g, mem_norm_g, w_mem_kv, final_g):
    bp, seq, _ = x_prompt.shape
    bs, dec_seq, _ = x_sample.shape
    assert bp == 1
    cache_len = cache_a_k.shape[2]
    n_s = bs * dec_seq
    keep = min(PAST_BAND, seq)

    w_in_b = w_in.astype(BF16)
    w_out_b = w_out.astype(BF16)
    w_mem_b = w_mem_kv.astype(BF16)

    lb_all = jnp.cumsum(jax.nn.softmax(lower_bounds.astype(F32), axis=0), axis=0)
    lb_all = lb_all - lb_all[:1]

    cache_a_kt = _per_head_transposed(cache_a_k)
    cache_a_vt = _per_head_transposed(cache_a_v)
    cache_mem_kt = _per_head_transposed(cache_mem_k).reshape(DEPTH, bs, X_WIDTH, N_MEM)
    cache_mem_vt = _per_head_transposed(cache_mem_v).reshape(DEPTH, bs, X_WIDTH, N_MEM)
    zero_state = jnp.zeros((1, 1, H_B, DK_B, DK_B), F32)

    def per_batch(u):
        return u.reshape(bs, dec_seq, u.shape[-1])

    xp = x_prompt
    xs = x_sample
    ak_p, av_p, sb_p, mk_pl, mv_pl, ak_s, av_s, sb_s = [], [], [], [], [], [], [], []
    for l in range(DEPTH):
        j = l // 2
        final = l == DEPTH - 1
        attn_layer = l % 2 == 0
        mkt_p, mvt_p = norm_proj(mem_prompt.reshape(N_MEM, D_MODEL), mem_norm_g[l], w_mem_b[l],
                                 (X_WIDTH, X_WIDTH), (F32, F32), N_MEM, transpose=True)
        dtypes = (BF16, BF16 if attn_layer else F32, BF16, BF16, F32)
        tails = (1, 2) if attn_layer else ()
        outs_p = norm_proj(xp.reshape(seq, D_MODEL), ln_g[l], w_in_b[l], PROJ_SPLITS, dtypes,
                           PAST_BAND, tails)
        outs_s = norm_proj(xs.reshape(n_s, D_MODEL), ln_g[l], w_in_b[l], PROJ_SPLITS, dtypes,
                           n_s, tails)
        a_p, b_p, c_p, xq_p, g_p = outs_p[:5]
        a_s, b_s, c_s, xq_s, g_s = outs_s[:5]

        if attn_layer:
            bias = _rel_bias(rel_bias_table[j])
            band = PAST_BAND + CHUNK
            o_p = band_attn_prompt(a_p, b_p, c_p, bias.reshape(H_A // 2, 2 * CHUNK, band), PAST_BAND)
            o_s = band_attn_sample(per_batch(a_s), per_batch(b_s), per_batch(c_s),
                                   cache_a_kt, cache_a_vt, j,
                                   bias[:, :dec_seq, :cache_len + dec_seq].reshape(
                                       H_A // 2, 2 * dec_seq, cache_len + dec_seq))
            assert keep == PAST_BAND
            ak_p.append(outs_p[5].reshape(1, keep, H_A, HD_A))
            av_p.append(outs_p[6].reshape(1, keep, H_A, HD_A))
            ak_s.append(outs_s[5].reshape(bs, dec_seq, H_A, HD_A))
            av_s.append(outs_s[6].reshape(bs, dec_seq, H_A, HD_A))
            o_p = o_p.reshape(1, seq, MIX_WIDTH)
        else:
            lb = lb_all[j]
            lbc = jnp.stack([jnp.log(lb), jnp.log1p(-lb), 1.0 - lb])
            o_p, s_p = hgrn2(a_p.reshape(1, seq, MIX_WIDTH), b_p.reshape(1, seq, MIX_WIDTH),
                             c_p.reshape(1, seq, MIX_WIDTH), zero_state, 0, lbc, hgrn_norm_g[j],
                             CHUNK, 4)
            o_s, s_s = hgrn2(per_batch(a_s), per_batch(b_s), per_batch(c_s),
                             state_b.astype(F32), j, lbc, hgrn_norm_g[j], dec_seq, 1)
            sb_p.append(s_p)
            sb_s.append(s_s)
        xp = post(xp, o_p, xq_p.reshape(1, seq, X_WIDTH), g_p.reshape(1, seq, D_INNER),
                  mkt_p.reshape(1, 1, X_WIDTH, N_MEM), mvt_p.reshape(1, 1, X_WIDTH, N_MEM), 0,
                  w_out_b[l], final_g, 512, final)
        xs = post(xs, o_s, per_batch(xq_s), per_batch(g_s), cache_mem_kt, cache_mem_vt, l,
                  w_out_b[l], final_g, dec_seq, final)
        mk_pl.append(jnp.moveaxis(mkt_p.reshape(1, H_X, HD_X, N_MEM), -1, 1))
        mv_pl.append(jnp.moveaxis(mvt_p.reshape(1, H_X, HD_X, N_MEM), -1, 1))
    return (xp, xs, jnp.stack(ak_p), jnp.stack(av_p), jnp.stack(sb_p), jnp.stack(mk_pl), jnp.stack(mv_pl),
            jnp.stack(ak_s), jnp.stack(av_s), jnp.stack(sb_s))
```

```python
import functools

import numpy as np
import jax
import jax.numpy as jnp
from jax import lax
from jax.experimental import pallas as pl
from jax.experimental.pallas import tpu as pltpu

D_MODEL = 1024
DEPTH = 4
CHUNK = 64
N_PAST_CHUNKS = 8
PAST_BAND = N_PAST_CHUNKS * CHUNK
MIX_WIDTH = 768
X_WIDTH = 256
D_INNER = MIX_WIDTH + X_WIDTH
HD_A = 64
H_A = MIX_WIDTH // HD_A
REL_CLIP = 128
DK_B = 128
H_B = MIX_WIDTH // DK_B
H_X = 4
HD_X = 64
N_MEM = 256
EPS = 1e-6
NEG = -1e30
LOG2_E = 1.4426950408889634
F32 = jnp.float32
BF16 = jnp.bfloat16

LANES = 128
VMEM_LIMIT_BYTES = 56 * 1024 * 1024
PROJ_SPLITS = (MIX_WIDTH, MIX_WIDTH, MIX_WIDTH, X_WIDTH, D_INNER)
CONTRACT_LAST = (((1,), (1,)), ((), ()))
BAND_UNROLL = 4


def _params(*sem):
    return pltpu.CompilerParams(dimension_semantics=sem, vmem_limit_bytes=VMEM_LIMIT_BYTES)


def _norm_proj_kernel(x_ref, g_ref, w_ref, *out_refs, splits, tails, transpose):
    x = x_ref[...]
    ms = jnp.mean(x * x, axis=-1, keepdims=True)
    xn = (x * lax.rsqrt(ms + EPS) * g_ref[...]).astype(BF16)
    tail_refs = out_refs[len(splits):]
    off = 0
    for idx, (o_ref, n) in enumerate(zip(out_refs, splits)):
        r = jnp.dot(xn, w_ref[:, off:off + n], preferred_element_type=F32)
        o_ref[...] = (r.T if transpose else r).astype(o_ref.dtype)
        if idx in tails:
            tail_refs[tails.index(idx)][...] = r
        off += n


def norm_proj(x2d, g, w_bf16, splits, dtypes, tm, tails=(), transpose=False):
    rows, d = x2d.shape
    n_total = w_bf16.shape[1]
    assert rows % tm == 0 and (not transpose or rows == tm)
    if transpose:
        out_specs = [pl.BlockSpec((n, tm), lambda i: (0, 0)) for n in splits]
        out_shape = [jax.ShapeDtypeStruct((n, rows), dt) for n, dt in zip(splits, dtypes)]
    else:
        out_specs = [pl.BlockSpec((tm, n), lambda i: (i, 0)) for n in splits]
        out_shape = [jax.ShapeDtypeStruct((rows, n), dt) for n, dt in zip(splits, dtypes)]
    out_specs += [pl.BlockSpec((tm, splits[idx]), lambda i: (0, 0)) for idx in tails]
    out_shape += [jax.ShapeDtypeStruct((tm, splits[idx]), F32) for idx in tails]
    return pl.pallas_call(
        functools.partial(_norm_proj_kernel, splits=splits, tails=tuple(tails), transpose=transpose),
        grid=(rows // tm,),
        in_specs=[
            pl.BlockSpec((tm, d), lambda i: (i, 0)),
            pl.BlockSpec((1, d), lambda i: (0, 0)),
            pl.BlockSpec((d, n_total), lambda i: (0, 0)),
        ],
        out_specs=out_specs,
        out_shape=out_shape,
        name=f"norm_proj_{rows}x{n_total}",
        compiler_params=_params("arbitrary"),
    )(x2d, g.reshape(1, d), w_bf16)


def _stack_pair(q_pair):
    first = lax.broadcasted_iota(jnp.int32, (1, LANES), 1) < HD_A
    keep0 = jnp.where(first, 1.0, 0.0).astype(BF16)
    keep1 = jnp.where(first, 0.0, 1.0).astype(BF16)
    return jnp.concatenate([q_pair * keep0, q_pair * keep1], axis=0)


def _unstack_pair(o2):
    m = o2.shape[0] // 2
    first = lax.broadcasted_iota(jnp.int32, (m, LANES), 1) < HD_A
    return jnp.where(first, o2[:m], o2[m:])


def _pair_values(p, v_ext):
    r = jnp.dot(p, v_ext, preferred_element_type=F32)
    return _unstack_pair(r[:, :LANES] / r[:, LANES:])


def _pair_values_t(p, vt_ext):
    r = lax.dot_general(p, vt_ext, CONTRACT_LAST, preferred_element_type=F32)
    return _unstack_pair(r[:, :LANES] / r[:, LANES:])


def _band_attn_kernel(q_ref, kprev_ref, kcur_ref, vprev_ref, vcur_ref, bias_ref, o_ref,
                      kwin, vext, s_scr, p_scr, *, n_chunks, unroll):
    step = pl.program_id(0)
    chunk, prev_rows = CHUNK, PAST_BAND
    cur_rows = n_chunks * chunk
    band = prev_rows + chunk
    n_pairs = H_A // 2
    kwin[0:prev_rows, :] = kprev_ref[...]
    kwin[prev_rows:prev_rows + cur_rows, :] = kcur_ref[...]
    ones = jnp.ones((prev_rows + cur_rows, LANES), BF16)
    for hp in range(n_pairs):
        lanes = slice(LANES * hp, LANES * (hp + 1))
        vext[0:prev_rows, 2 * LANES * hp:2 * LANES * hp + LANES] = vprev_ref[:, lanes]
        vext[prev_rows:prev_rows + cur_rows, 2 * LANES * hp:2 * LANES * hp + LANES] = vcur_ref[:, lanes]
        vext[:, 2 * LANES * hp + LANES:2 * LANES * (hp + 1)] = ones

    def chunks_body(ci, carry):
        starts = [pl.multiple_of((ci * unroll + u) * chunk, chunk) for u in range(unroll)]
        for u, r0 in enumerate(starts):
            for hp in range(n_pairs):
                lanes = slice(LANES * hp, LANES * (hp + 1))
                q2 = _stack_pair(q_ref[pl.ds(r0, chunk), lanes] * (HD_A ** -0.5))
                s_scr[u * n_pairs + hp] = lax.dot_general(
                    q2, kwin[pl.ds(r0, band), lanes], CONTRACT_LAST, preferred_element_type=F32)
        for u, r0 in enumerate(starts):
            key_row = lax.broadcasted_iota(jnp.int32, (1, band), 1) + r0 + (step - 1) * prev_rows
            neg = jnp.where(key_row < 0, NEG, 0.0)
            for hp in range(n_pairs):
                s = s_scr[u * n_pairs + hp] + bias_ref[hp] + neg
                m = jnp.max(s, axis=-1, keepdims=True)
                p_scr[u * n_pairs + hp] = jnp.exp(s - m).astype(BF16)
        for u, r0 in enumerate(starts):
            for hp in range(n_pairs):
                o_ref[pl.ds(r0, chunk), LANES * hp:LANES * (hp + 1)] = _pair_values(
                    p_scr[u * n_pairs + hp],
                    vext[pl.ds(r0, band), 2 * LANES * hp:2 * LANES * (hp + 1)]
                ).astype(o_ref.dtype)
        return carry

    lax.fori_loop(0, n_chunks // unroll, chunks_body, 0)


def band_attn_prompt(q, k, v, bias, block_rows):
    rows = q.shape[0]
    n_chunks = block_rows // CHUNK
    n_pairs = H_A // 2
    band = PAST_BAND + CHUNK
    assert block_rows == PAST_BAND
    cur = pl.BlockSpec((block_rows, MIX_WIDTH), lambda i: (i, 0))
    prev = pl.BlockSpec((block_rows, MIX_WIDTH), lambda i: (jnp.maximum(i - 1, 0), 0))
    return pl.pallas_call(
        functools.partial(_band_attn_kernel, n_chunks=n_chunks, unroll=BAND_UNROLL),
        grid=(rows // block_rows,),
        in_specs=[cur, prev, cur, prev, cur,
                  pl.BlockSpec(bias.shape, lambda i: (0, 0, 0))],
        out_specs=cur,
        out_shape=jax.ShapeDtypeStruct((rows, MIX_WIDTH), BF16),
        scratch_shapes=[pltpu.VMEM((PAST_BAND + block_rows, MIX_WIDTH), BF16),
                        pltpu.VMEM((PAST_BAND + block_rows, 2 * MIX_WIDTH), BF16),
                        pltpu.VMEM((BAND_UNROLL * n_pairs, 2 * CHUNK, band), F32),
                        pltpu.VMEM((BAND_UNROLL * n_pairs, 2 * CHUNK, band), BF16)],
        name="band_attn_prompt",
        compiler_params=_params("arbitrary"),
    )(q, k, k, v, v, bias)


def _band_sample_kernel(q_ref, k_ref, v_ref, kct_ref, vct_ref, bias_ref, o_ref, *, cache_len):
    for hp in range(H_A // 2):
        lanes = slice(LANES * hp, LANES * (hp + 1))
        q2 = _stack_pair(q_ref[:, lanes] * (HD_A ** -0.5))
        kct = kct_ref[2 * hp:2 * hp + 2].reshape(LANES, cache_len).astype(BF16)
        vct = vct_ref[2 * hp:2 * hp + 2].reshape(LANES, cache_len).astype(BF16)
        s_old = jnp.dot(q2, kct, preferred_element_type=F32) + bias_ref[hp, :, 0:cache_len]
        s_new = lax.dot_general(q2, k_ref[:, lanes], CONTRACT_LAST,
                                preferred_element_type=F32) + bias_ref[hp, :, cache_len:]
        m = jnp.maximum(jnp.max(s_old, axis=-1, keepdims=True),
                        jnp.max(s_new, axis=-1, keepdims=True))
        p_old = jnp.exp(s_old - m)
        p_new = jnp.exp(s_new - m)
        l = jnp.sum(p_old, axis=-1, keepdims=True) + jnp.sum(p_new, axis=-1, keepdims=True)
        o2 = (lax.dot_general(p_old.astype(BF16), vct, CONTRACT_LAST, preferred_element_type=F32)
              + jnp.dot(p_new.astype(BF16), v_ref[:, lanes], preferred_element_type=F32))
        o_ref[:, lanes] = _unstack_pair(o2 / l).astype(o_ref.dtype)


def band_attn_sample(q, k, v, k_cache_t, v_cache_t, layer, bias):
    b, s_len, _ = q.shape
    cache_len = k_cache_t.shape[-1]
    new = pl.BlockSpec((None, s_len, MIX_WIDTH), lambda i: (i, 0, 0))
    old = pl.BlockSpec((None, None, H_A, HD_A, cache_len), lambda i: (layer, i, 0, 0, 0))
    return pl.pallas_call(
        functools.partial(_band_sample_kernel, cache_len=cache_len),
        grid=(b,),
        in_specs=[new, new, new, old, old, pl.BlockSpec(bias.shape, lambda i: (0, 0, 0))],
        out_specs=new,
        out_shape=jax.ShapeDtypeStruct((b, s_len, MIX_WIDTH), BF16),
        name="band_attn_sample",
        compiler_params=_params("arbitrary"),
    )(q, k, v, k_cache_t, v_cache_t, bias)


def _hgrn2_constants(t):
    halves = []
    h = t // 2
    while h >= 1:
        halves.append(h)
        h //= 2
    n_lvl = len(halves)
    w = np.zeros(((n_lvl + 2) * t, t), np.float32)
    masks = np.zeros((n_lvl + 1, t, t), np.float32)
    for row in range(t):
        w[row, :row + 1] = 1.0
        w[t + row, row + 1:] = 1.0
    for li, h in enumerate(halves):
        base = (2 + li) * t
        for row in range(t):
            r = (row // (2 * h)) * 2 * h + h - 1
            if row > r:
                w[base + row, r + 1:row + 1] = 1.0
            else:
                w[base + row, row + 1:r + 1] = 1.0
        for tq in range(t):
            for sk in range(t):
                if tq // (2 * h) == sk // (2 * h):
                    r = (tq // (2 * h)) * 2 * h + h - 1
                    if tq > r and sk <= r:
                        masks[li, tq, sk] = 1.0
    masks[n_lvl] = np.eye(t, dtype=np.float32)
    return jnp.asarray(np.tile(w, (1, 3)), BF16), jnp.asarray(masks, F32), n_lvl


def _hgrn2_kernel(a_ref, b_ref, c_ref, s0_ref, lbc_ref, g_ref, w_ref, m_ref,
                  o_ref, sout_ref,
                  st_ref, lf_scr, kk_scr, sums_scr, qs_scr, ks_scr, attn_scr, u_scr, oi_scr,
                  *, t, n_lvl, n_blk):
    ci = pl.program_id(1)
    inter = n_lvl + 1

    @pl.when(ci == 0)
    def _():
        for h in range(H_B):
            st_ref[h] = s0_ref[h].T

    def blk(g):
        return slice(g * t, (g + 1) * t)

    def head(h):
        return slice(LANES * h, LANES * (h + 1))

    z = b_ref[...]
    log_lb = lbc_ref[0:1, :]
    u = lbc_ref[1:2, :] + jnp.minimum(z, 0.0) - jnp.log(1.0 + jnp.exp(-jnp.abs(z)))
    log_f = jnp.maximum(log_lb, u) + jnp.log(1.0 + jnp.exp(-jnp.abs(log_lb - u)))
    kk_scr[...] = lbc_ref[2:3, :] / (1.0 + jnp.exp(z))
    log2_f = log_f * LOG2_E
    hi = log2_f.astype(BF16)
    r1 = log2_f - hi.astype(F32)
    mid = r1.astype(BF16)
    lo = (r1 - mid.astype(F32)).astype(BF16)
    for g in range(n_blk):
        lf_scr[g, 0:t] = hi[blk(g)]
        lf_scr[g, t:2 * t] = mid[blk(g)]
        lf_scr[g, 2 * t:3 * t] = lo[blk(g)]

    for g in range(n_blk):
        sums_scr[g] = jnp.dot(w_ref[...], lf_scr[g], preferred_element_type=F32)

    for g in range(n_blk):
        q = a_ref[blk(g), :]
        kk = kk_scr[blk(g), :].astype(BF16)
        for li in range(n_lvl):
            e = jnp.exp2(sums_scr[g, (2 + li) * t:(3 + li) * t]).astype(BF16)
            qs_scr[g, li] = q * e
            ks_scr[g, li] = kk * e
        qs_scr[g, n_lvl] = q
        ks_scr[g, n_lvl] = kk
        qs_scr[g, inter] = q * jnp.exp2(sums_scr[g, 0:t]).astype(BF16)
        ks_scr[g, inter] = kk * jnp.exp2(sums_scr[g, t:2 * t]).astype(BF16)

    in_level = [m_ref[li] != 0.0 for li in range(n_lvl + 1)]
    for g in range(n_blk):
        for h in range(H_B):
            acc = jnp.zeros((t, t), F32)
            for li in range(n_lvl + 1):
                part = lax.dot_general(
                    qs_scr[g, li, :, head(h)], ks_scr[g, li, :, head(h)],
                    CONTRACT_LAST, preferred_element_type=F32)
                acc = jnp.where(in_level[li], part, acc)
            attn_scr[g, h] = acc.astype(BF16)

    for g in range(n_blk):
        for h in range(H_B):
            iv = c_ref[blk(g), head(h)]
            oi_scr[blk(g), head(h)] = jnp.dot(attn_scr[g, h], iv, preferred_element_type=F32)
            u_scr[g, h] = jnp.dot(iv.astype(F32).T.astype(BF16), ks_scr[g, inter, :, head(h)],
                                  preferred_element_type=F32)

    for g in range(n_blk):
        e_last = jnp.exp2(sums_scr[g, t - 1:t, :])
        for h in range(H_B):
            st = st_ref[h]
            o = oi_scr[blk(g), head(h)] + lax.dot_general(
                qs_scr[g, inter, :, head(h)], st.astype(BF16), CONTRACT_LAST,
                preferred_element_type=F32)
            st_ref[h] = e_last[:, head(h)] * st + u_scr[g, h]
            ms = jnp.mean(o * o, axis=-1, keepdims=True)
            o_ref[blk(g), head(h)] = (o * lax.rsqrt(ms + EPS) * g_ref[:, head(h)]).astype(o_ref.dtype)

    @pl.when(ci == pl.num_programs(1) - 1)
    def _():
        for h in range(H_B):
            sout_ref[h] = st_ref[h].T


def hgrn2(a, b, c, s0, layer, lbc, g, t, n_blk):
    bsz, rows, _ = a.shape
    w, masks, n_lvl = _hgrn2_constants(t)
    step_rows = n_blk * t
    tok = pl.BlockSpec((None, step_rows, MIX_WIDTH), lambda bi, ci: (bi, ci, 0))
    state = pl.BlockSpec((None, H_B, DK_B, DK_B), lambda bi, ci: (bi, 0, 0, 0))
    state_in = pl.BlockSpec((None, None, H_B, DK_B, DK_B), lambda bi, ci: (layer, bi, 0, 0, 0))
    scratch = [pltpu.VMEM((H_B, DK_B, DK_B), F32),
               pltpu.VMEM((n_blk, 3 * t, MIX_WIDTH), BF16),
               pltpu.VMEM((step_rows, MIX_WIDTH), F32),
               pltpu.VMEM((n_blk, (n_lvl + 2) * t, MIX_WIDTH), F32),
               pltpu.VMEM((n_blk, n_lvl + 2, t, MIX_WIDTH), BF16),
               pltpu.VMEM((n_blk, n_lvl + 2, t, MIX_WIDTH), BF16),
               pltpu.VMEM((n_blk, H_B, t, t), BF16),
               pltpu.VMEM((n_blk, H_B, DK_B, DK_B), F32),
               pltpu.VMEM((step_rows, MIX_WIDTH), F32)]
    return pl.pallas_call(
        functools.partial(_hgrn2_kernel, t=t, n_lvl=n_lvl, n_blk=n_blk),
        grid=(bsz, rows // step_rows),
        in_specs=[tok, tok, tok, state_in,
                  pl.BlockSpec(lbc.shape, lambda bi, ci: (0, 0)),
                  pl.BlockSpec((1, MIX_WIDTH), lambda bi, ci: (0, 0)),
                  pl.BlockSpec(w.shape, lambda bi, ci: (0, 0)),
                  pl.BlockSpec(masks.shape, lambda bi, ci: (0, 0, 0))],
        out_specs=[tok, state],
        out_shape=[jax.ShapeDtypeStruct((bsz, rows, MIX_WIDTH), BF16),
                   jax.ShapeDtypeStruct((bsz, H_B, DK_B, DK_B), F32)],
        scratch_shapes=scratch,
        name=f"hgrn2_t{t}",
        compiler_params=_params("arbitrary", "arbitrary"),
    )(a, b, c, s0, lbc, g.reshape(1, MIX_WIDTH), w, masks)


def _post_kernel(x_ref, mix_ref, xq_ref, gate_ref, mkt_ref, mvt_ref, w_ref, fg_ref, o_ref, *, final):
    gate = gate_ref[...]
    sg = gate / (1.0 + jnp.exp(-gate))
    y_mix = (mix_ref[...].astype(F32) * sg[:, 0:MIX_WIDTH]).astype(BF16)
    acc = x_ref[...] + jnp.dot(y_mix, w_ref[0:MIX_WIDTH, :], preferred_element_type=F32)
    ones = jnp.ones((LANES, N_MEM), BF16)
    crosses = []
    for hp in range(H_X // 2):
        lanes = slice(LANES * hp, LANES * (hp + 1))
        q2 = _stack_pair(xq_ref[:, lanes] * (HD_X ** -0.5))
        s = jnp.dot(q2, mkt_ref[lanes, :].astype(BF16), preferred_element_type=F32)
        p = jnp.exp(s - jnp.max(s, axis=-1, keepdims=True)).astype(BF16)
        crosses.append(_pair_values_t(
            p, jnp.concatenate([mvt_ref[lanes, :].astype(BF16), ones], axis=0)))
    for hp, cross in enumerate(crosses):
        rows = slice(MIX_WIDTH + LANES * hp, MIX_WIDTH + LANES * (hp + 1))
        y = (cross * sg[:, rows]).astype(BF16)
        acc = acc + jnp.dot(y, w_ref[rows, :], preferred_element_type=F32)
    if final:
        ms = jnp.mean(acc * acc, axis=-1, keepdims=True)
        acc = acc * lax.rsqrt(ms + EPS) * fg_ref[...]
    o_ref[...] = acc


def post(x, mix, xq, gate, mkt, mvt, layer, w_bf16, final_g, tm, final):
    bsz, rows, _ = x.shape

    def tok(n):
        return pl.BlockSpec((None, tm, n), lambda bi, i: (bi, i, 0))

    mem = pl.BlockSpec((None, None, X_WIDTH, N_MEM), lambda bi, i: (layer, bi, 0, 0))
    return pl.pallas_call(
        functools.partial(_post_kernel, final=final),
        grid=(bsz, rows // tm),
        in_specs=[tok(D_MODEL), tok(MIX_WIDTH), tok(X_WIDTH), tok(D_INNER), mem, mem,
                  pl.BlockSpec((D_INNER, D_MODEL), lambda bi, i: (0, 0)),
                  pl.BlockSpec((1, D_MODEL), lambda bi, i: (0, 0))],
        out_specs=tok(D_MODEL),
        out_shape=jax.ShapeDtypeStruct((bsz, rows, D_MODEL), F32),
        name=f"post_b{bsz}",
        compiler_params=_params("arbitrary", "arbitrary"),
    )(x, mix, xq, gate, mkt, mvt, w_bf16, final_g.reshape(1, D_MODEL))


def _rel_bias(table):
    band = PAST_BAND + CHUNK
    n_diag = band + CHUNK - 1
    offs = np.arange(n_diag) - (CHUNK - 1)
    idx = np.clip(PAST_BAND - offs, -REL_CLIP, REL_CLIP) + REL_CLIP
    diag = jnp.pad(table[:, idx].astype(F32), ((0, 0), (0, 1)))
    skew = jnp.tile(diag, (1, CHUNK))[:, :CHUNK * n_diag].reshape(H_A, CHUNK, n_diag)
    return skew[:, :, CHUNK - 1:CHUNK - 1 + band]


def _per_head_transposed(cache):
    return jnp.moveaxis(cache, -3, -1)


def kernel(x_prompt, x_sample, cache_a_k, cache_a_v, state_b, cache_mem_k, cache_mem_v, mem_prompt,
           ln_g, w_in, w_out, rel_bias_table, lower_bounds, hgrn_norm_g, mem_norm_g, w_mem_kv, final_g):
    bp, seq, _ = x_prompt.shape
    bs, dec_seq, _ = x_sample.shape
    assert bp == 1
    cache_len = cache_a_k.shape[2]
    n_s = bs * dec_seq
    keep = min(PAST_BAND, seq)

    w_in_b = w_in.astype(BF16)
    w_out_b = w_out.astype(BF16)
    w_mem_b = w_mem_kv.astype(BF16)

    lb_all = jnp.cumsum(jax.nn.softmax(lower_bounds.astype(F32), axis=0), axis=0)
    lb_all = lb_all - lb_all[:1]

    cache_a_kt = _per_head_transposed(cache_a_k)
    cache_a_vt = _per_head_transposed(cache_a_v)
    cache_mem_kt = _per_head_transposed(cache_mem_k).reshape(DEPTH, bs, X_WIDTH, N_MEM)
    cache_mem_vt = _per_head_transposed(cache_mem_v).reshape(DEPTH, bs, X_WIDTH, N_MEM)
    zero_state = jnp.zeros((1, 1, H_B, DK_B, DK_B), F32)

    def per_batch(u):
        return u.reshape(bs, dec_seq, u.shape[-1])

    xp = x_prompt
    xs = x_sample
    ak_p, av_p, sb_p, mk_pl, mv_pl, ak_s, av_s, sb_s = [], [], [], [], [], [], [], []
    for l in range(DEPTH):
        j = l // 2
        final = l == DEPTH - 1
        attn_layer = l % 2 == 0
        mkt_p, mvt_p = norm_proj(mem_prompt.reshape(N_MEM, D_MODEL), mem_norm_g[l], w_mem_b[l],
                                 (X_WIDTH, X_WIDTH), (F32, F32), N_MEM, transpose=True)
        dtypes = (BF16, BF16 if attn_layer else F32, BF16, BF16, F32)
        tails = (1, 2) if attn_layer else ()
        outs_p = norm_proj(xp.reshape(seq, D_MODEL), ln_g[l], w_in_b[l], PROJ_SPLITS, dtypes,
                           PAST_BAND, tails)
        outs_s = norm_proj(xs.reshape(n_s, D_MODEL), ln_g[l], w_in_b[l], PROJ_SPLITS, dtypes,
                           n_s, tails)
        a_p, b_p, c_p, xq_p, g_p = outs_p[:5]
        a_s, b_s, c_s, xq_s, g_s = outs_s[:5]

        if attn_layer:
            bias = _rel_bias(rel_bias_table[j])
            band = PAST_BAND + CHUNK
            o_p = band_attn_prompt(a_p, b_p, c_p, bias.reshape(H_A // 2, 2 * CHUNK, band), PAST_BAND)
            o_s = band_attn_sample(per_batch(a_s), per_batch(b_s), per_batch(c_s),
                                   cache_a_kt, cache_a_vt, j,
                                   bias[:, :dec_seq, :cache_len + dec_seq].reshape(
                                       H_A // 2, 2 * dec_seq, cache_len + dec_seq))
            assert keep == PAST_BAND
            ak_p.append(outs_p[5].reshape(1, keep, H_A, HD_A))
            av_p.append(outs_p[6].reshape(1, keep, H_A, HD_A))
            ak_s.append(outs_s[5].reshape(bs, dec_seq, H_A, HD_A))
            av_s.append(outs_s[6].reshape(bs, dec_seq, H_A, HD_A))
            o_p = o_p.reshape(1, seq, MIX_WIDTH)
        else:
            lb = lb_all[j]
            lbc = jnp.stack([jnp.log(lb), jnp.log1p(-lb), 1.0 - lb])
            o_p, s_p = hgrn2(a_p.reshape(1, seq, MIX_WIDTH), b_p.reshape(1, seq, MIX_WIDTH),
                             c_p.reshape(1, seq, MIX_WIDTH), zero_state, 0, lbc, hgrn_norm_g[j],
                             CHUNK, 4)
            o_s, s_s = hgrn2(per_batch(a_s), per_batch(b_s), per_batch(c_s),
                             state_b.astype(F32), j, lbc, hgrn_norm_g[j], dec_seq, 1)
            sb_p.append(s_p)
            sb_s.append(s_s)
        xp = post(xp, o_p, xq_p.reshape(1, seq, X_WIDTH), g_p.reshape(1, seq, D_INNER),
                  mkt_p.reshape(1, 1, X_WIDTH, N_MEM), mvt_p.reshape(1, 1, X_WIDTH, N_MEM), 0,
                  w_out_b[l], final_g, 512, final)
        xs = post(xs, o_s, per_batch(xq_s), per_batch(g_s), cache_mem_kt, cache_mem_vt, l,
                  w_out_b[l], final_g, dec_seq, final)
        mk_pl.append(jnp.moveaxis(mkt_p.reshape(1, H_X, HD_X, N_MEM), -1, 1))
        mv_pl.append(jnp.moveaxis(mvt_p.reshape(1, H_X, HD_X, N_MEM), -1, 1))
    return (xp, xs, jnp.stack(ak_p), jnp.stack(av_p), jnp.stack(sb_p), jnp.stack(mk_pl), jnp.stack(mv_pl),
            jnp.stack(ak_s), jnp.stack(av_s), jnp.stack(sb_s))
```

```python
import functools

import numpy as np
import jax
import jax.numpy as jnp
from jax import lax
from jax.experimental import pallas as pl
from jax.experimental.pallas import tpu as pltpu

D_MODEL = 1024
DEPTH = 4
CHUNK = 64
N_PAST_CHUNKS = 8
PAST_BAND = N_PAST_CHUNKS * CHUNK
MIX_WIDTH = 768
X_WIDTH = 256
D_INNER = MIX_WIDTH + X_WIDTH
HD_A = 64
H_A = MIX_WIDTH // HD_A
REL_CLIP = 128
DK_B = 128
H_B = MIX_WIDTH // DK_B
H_X = 4
HD_X = 64
N_MEM = 256
EPS = 1e-6
NEG = -1e30
LOG2_E = 1.4426950408889634
F32 = jnp.float32
BF16 = jnp.bfloat16

LANES = 128
VMEM_LIMIT_BYTES = 56 * 1024 * 1024
PROJ_SPLITS = (MIX_WIDTH, MIX_WIDTH, MIX_WIDTH, X_WIDTH, D_INNER)
CONTRACT_LAST = (((1,), (1,)), ((), ()))
BAND_UNROLL = 4


def _params(*sem):
    return pltpu.CompilerParams(dimension_semantics=sem, vmem_limit_bytes=VMEM_LIMIT_BYTES)


def _norm_proj_kernel(x_ref, g_ref, w_ref, *out_refs, splits, tails, transpose):
    x = x_ref[...]
    ms = jnp.mean(x * x, axis=-1, keepdims=True)
    xn = (x * lax.rsqrt(ms + EPS) * g_ref[...]).astype(BF16)
    tail_refs = out_refs[len(splits):]
    off = 0
    for idx, (o_ref, n) in enumerate(zip(out_refs, splits)):
        r = jnp.dot(xn, w_ref[:, off:off + n], preferred_element_type=F32)
        o_ref[...] = (r.T if transpose else r).astype(o_ref.dtype)
        if idx in tails:
            tail_refs[tails.index(idx)][...] = r
        off += n


def norm_proj(x2d, g, w_bf16, splits, dtypes, tm, tails=(), transpose=False):
    rows, d = x2d.shape
    n_total = w_bf16.shape[1]
    assert rows % tm == 0 and (not transpose or rows == tm)
    if transpose:
        out_specs = [pl.BlockSpec((n, tm), lambda i: (0, 0)) for n in splits]
        out_shape = [jax.ShapeDtypeStruct((n, rows), dt) for n, dt in zip(splits, dtypes)]
    else:
        out_specs = [pl.BlockSpec((tm, n), lambda i: (i, 0)) for n in splits]
        out_shape = [jax.ShapeDtypeStruct((rows, n), dt) for n, dt in zip(splits, dtypes)]
    out_specs += [pl.BlockSpec((tm, splits[idx]), lambda i: (0, 0)) for idx in tails]
    out_shape += [jax.ShapeDtypeStruct((tm, splits[idx]), F32) for idx in tails]
    return pl.pallas_call(
        functools.partial(_norm_proj_kernel, splits=splits, tails=tuple(tails), transpose=transpose),
        grid=(rows // tm,),
        in_specs=[
            pl.BlockSpec((tm, d), lambda i: (i, 0)),
            pl.BlockSpec((1, d), lambda i: (0, 0)),
            pl.BlockSpec((d, n_total), lambda i: (0, 0)),
        ],
        out_specs=out_specs,
        out_shape=out_shape,
        name=f"norm_proj_{rows}x{n_total}",
        compiler_params=_params("arbitrary"),
    )(x2d, g.reshape(1, d), w_bf16)


def _stack_pair(q_pair):
    first = lax.broadcasted_iota(jnp.int32, (1, LANES), 1) < HD_A
    keep0 = jnp.where(first, 1.0, 0.0).astype(BF16)
    keep1 = jnp.where(first, 0.0, 1.0).astype(BF16)
    return jnp.concatenate([q_pair * keep0, q_pair * keep1], axis=0)


def _unstack_pair(o2):
    m = o2.shape[0] // 2
    first = lax.broadcasted_iota(jnp.int32, (m, LANES), 1) < HD_A
    return jnp.where(first, o2[:m], o2[m:])


def _pair_values(p, v_ext):
    r = jnp.dot(p, v_ext, preferred_element_type=F32)
    return _unstack_pair(r[:, :LANES] / r[:, LANES:])


def _pair_values_t(p, vt_ext):
    r = lax.dot_general(p, vt_ext, CONTRACT_LAST, preferred_element_type=F32)
    return _unstack_pair(r[:, :LANES] / r[:, LANES:])


def _band_attn_kernel(q_ref, kprev_ref, kcur_ref, vprev_ref, vcur_ref, bias_ref, o_ref,
                      kwin, vext, s_scr, p_scr, *, n_chunks, unroll):
    step = pl.program_id(0)
    chunk, prev_rows = CHUNK, PAST_BAND
    cur_rows = n_chunks * chunk
    band = prev_rows + chunk
    n_pairs = H_A // 2
    kwin[0:prev_rows, :] = kprev_ref[...]
    kwin[prev_rows:prev_rows + cur_rows, :] = kcur_ref[...]
    ones = jnp.ones((prev_rows + cur_rows, LANES), BF16)
    for hp in range(n_pairs):
        lanes = slice(LANES * hp, LANES * (hp + 1))
        vext[0:prev_rows, 2 * LANES * hp:2 * LANES * hp + LANES] = vprev_ref[:, lanes]
        vext[prev_rows:prev_rows + cur_rows, 2 * LANES * hp:2 * LANES * hp + LANES] = vcur_ref[:, lanes]
        vext[:, 2 * LANES * hp + LANES:2 * LANES * (hp + 1)] = ones

    def chunks_body(ci, carry):
        starts = [pl.multiple_of((ci * unroll + u) * chunk, chunk) for u in range(unroll)]
        for u, r0 in enumerate(starts):
            for hp in range(n_pairs):
                lanes = slice(LANES * hp, LANES * (hp + 1))
                q2 = _stack_pair(q_ref[pl.ds(r0, chunk), lanes] * (HD_A ** -0.5))
                s_scr[u * n_pairs + hp] = lax.dot_general(
                    q2, kwin[pl.ds(r0, band), lanes], CONTRACT_LAST, preferred_element_type=F32)
        for u, r0 in enumerate(starts):
            key_row = lax.broadcasted_iota(jnp.int32, (1, band), 1) + r0 + (step - 1) * prev_rows
            neg = jnp.where(key_row < 0, NEG, 0.0)
            for hp in range(n_pairs):
                s = s_scr[u * n_pairs + hp] + bias_ref[hp] + neg
                m = jnp.max(s, axis=-1, keepdims=True)
                p_scr[u * n_pairs + hp] = jnp.exp(s - m).astype(BF16)
        for u, r0 in enumerate(starts):
            for hp in range(n_pairs):
                o_ref[pl.ds(r0, chunk), LANES * hp:LANES * (hp + 1)] = _pair_values(
                    p_scr[u * n_pairs + hp],
                    vext[pl.ds(r0, band), 2 * LANES * hp:2 * LANES * (hp + 1)]
                ).astype(o_ref.dtype)
        return carry

    lax.fori_loop(0, n_chunks // unroll, chunks_body, 0)


def band_attn_prompt(q, k, v, bias, block_rows):
    rows = q.shape[0]
    n_chunks = block_rows // CHUNK
    n_pairs = H_A // 2
    band = PAST_BAND + CHUNK
    assert block_rows == PAST_BAND
    cur = pl.BlockSpec((block_rows, MIX_WIDTH), lambda i: (i, 0))
    prev = pl.BlockSpec((block_rows, MIX_WIDTH), lambda i: (jnp.maximum(i - 1, 0), 0))
    return pl.pallas_call(
        functools.partial(_band_attn_kernel, n_chunks=n_chunks, unroll=BAND_UNROLL),
        grid=(rows // block_rows,),
        in_specs=[cur, prev, cur, prev, cur,
                  pl.BlockSpec(bias.shape, lambda i: (0, 0, 0))],
        out_specs=cur,
        out_shape=jax.ShapeDtypeStruct((rows, MIX_WIDTH), BF16),
        scratch_shapes=[pltpu.VMEM((PAST_BAND + block_rows, MIX_WIDTH), BF16),
                        pltpu.VMEM((PAST_BAND + block_rows, 2 * MIX_WIDTH), BF16),
                        pltpu.VMEM((BAND_UNROLL * n_pairs, 2 * CHUNK, band), F32),
                        pltpu.VMEM((BAND_UNROLL * n_pairs, 2 * CHUNK, band), BF16)],
        name="band_attn_prompt",
        compiler_params=_params("arbitrary"),
    )(q, k, k, v, v, bias)


def _band_sample_kernel(q_ref, k_ref, v_ref, kct_ref, vct_ref, bias_ref, o_ref, s_scr, p_scr,
                        *, cache_len):
    n_pairs = H_A // 2
    s_len = q_ref.shape[0]
    for hp in range(n_pairs):
        lanes = slice(LANES * hp, LANES * (hp + 1))
        q2 = _stack_pair(q_ref[:, lanes] * (HD_A ** -0.5))
        kct = kct_ref[2 * hp:2 * hp + 2].reshape(LANES, cache_len).astype(BF16)
        s_scr[hp, :, 0:cache_len] = jnp.dot(q2, kct, preferred_element_type=F32)
        s_scr[hp, :, cache_len:] = lax.dot_general(q2, k_ref[:, lanes], CONTRACT_LAST,
                                                   preferred_element_type=F32)
    for hp in range(n_pairs):
        s = s_scr[hp] + bias_ref[hp]
        p_scr[hp] = jnp.exp(s - jnp.max(s, axis=-1, keepdims=True)).astype(BF16)
    ones_old = jnp.ones((LANES, cache_len), BF16)
    ones_new = jnp.ones((s_len, LANES), BF16)
    for hp in range(n_pairs):
        lanes = slice(LANES * hp, LANES * (hp + 1))
        vct = vct_ref[2 * hp:2 * hp + 2].reshape(LANES, cache_len).astype(BF16)
        r = (lax.dot_general(p_scr[hp, :, 0:cache_len], jnp.concatenate([vct, ones_old], axis=0),
                             CONTRACT_LAST, preferred_element_type=F32)
             + jnp.dot(p_scr[hp, :, cache_len:], jnp.concatenate([v_ref[:, lanes], ones_new], axis=1),
                       preferred_element_type=F32))
        o_ref[:, lanes] = _unstack_pair(r[:, :LANES] / r[:, LANES:]).astype(o_ref.dtype)


def band_attn_sample(q, k, v, k_cache_t, v_cache_t, layer, bias):
    b, s_len, _ = q.shape
    cache_len = k_cache_t.shape[-1]
    new = pl.BlockSpec((None, s_len, MIX_WIDTH), lambda i: (i, 0, 0))
    old = pl.BlockSpec((None, None, H_A, HD_A, cache_len), lambda i: (layer, i, 0, 0, 0))
    return pl.pallas_call(
        functools.partial(_band_sample_kernel, cache_len=cache_len),
        grid=(b,),
        in_specs=[new, new, new, old, old, pl.BlockSpec(bias.shape, lambda i: (0, 0, 0))],
        out_specs=new,
        out_shape=jax.ShapeDtypeStruct((b, s_len, MIX_WIDTH), BF16),
        scratch_shapes=[pltpu.VMEM((H_A // 2, 2 * s_len, cache_len + s_len), F32),
                        pltpu.VMEM((H_A // 2, 2 * s_len, cache_len + s_len), BF16)],
        name="band_attn_sample",
        compiler_params=_params("arbitrary"),
    )(q, k, v, k_cache_t, v_cache_t, bias)


def _hgrn2_constants(t):
    halves = []
    h = t // 2
    while h >= 1:
        halves.append(h)
        h //= 2
    n_lvl = len(halves)
    w = np.zeros(((n_lvl + 2) * t, t), np.float32)
    masks = np.zeros((n_lvl + 1, t, t), np.float32)
    for row in range(t):
        w[row, :row + 1] = 1.0
        w[t + row, row + 1:] = 1.0
    for li, h in enumerate(halves):
        base = (2 + li) * t
        for row in range(t):
            r = (row // (2 * h)) * 2 * h + h - 1
            if row > r:
                w[base + row, r + 1:row + 1] = 1.0
            else:
                w[base + row, row + 1:r + 1] = 1.0
        for tq in range(t):
            for sk in range(t):
                if tq // (2 * h) == sk // (2 * h):
                    r = (tq // (2 * h)) * 2 * h + h - 1
                    if tq > r and sk <= r:
                        masks[li, tq, sk] = 1.0
    masks[n_lvl] = np.eye(t, dtype=np.float32)
    return jnp.asarray(np.tile(w, (1, 3)), BF16), jnp.asarray(masks, F32), n_lvl


def _hgrn2_kernel(a_ref, b_ref, c_ref, s0_ref, lbc_ref, g_ref, w_ref, m_ref,
                  o_ref, sout_ref,
                  st_ref, lf_scr, kk_scr, sums_scr, qs_scr, ks_scr, attn_scr, u_scr, oi_scr,
                  *, t, n_lvl, n_blk):
    ci = pl.program_id(1)
    inter = n_lvl + 1

    @pl.when(ci == 0)
    def _():
        for h in range(H_B):
            st_ref[h] = s0_ref[h].T

    def blk(g):
        return slice(g * t, (g + 1) * t)

    def head(h):
        return slice(LANES * h, LANES * (h + 1))

    z = b_ref[...]
    log_lb = lbc_ref[0:1, :]
    u = lbc_ref[1:2, :] + jnp.minimum(z, 0.0) - jnp.log(1.0 + jnp.exp(-jnp.abs(z)))
    log_f = jnp.maximum(log_lb, u) + jnp.log(1.0 + jnp.exp(-jnp.abs(log_lb - u)))
    kk_scr[...] = lbc_ref[2:3, :] / (1.0 + jnp.exp(z))
    log2_f = log_f * LOG2_E
    hi = log2_f.astype(BF16)
    r1 = log2_f - hi.astype(F32)
    mid = r1.astype(BF16)
    lo = (r1 - mid.astype(F32)).astype(BF16)
    for g in range(n_blk):
        lf_scr[g, 0:t] = hi[blk(g)]
        lf_scr[g, t:2 * t] = mid[blk(g)]
        lf_scr[g, 2 * t:3 * t] = lo[blk(g)]

    for g in range(n_blk):
        sums_scr[g] = jnp.dot(w_ref[...], lf_scr[g], preferred_element_type=F32)

    for g in range(n_blk):
        q = a_ref[blk(g), :]
        kk = kk_scr[blk(g), :].astype(BF16)
        for li in range(n_lvl):
            e = jnp.exp2(sums_scr[g, (2 + li) * t:(3 + li) * t]).astype(BF16)
            qs_scr[g, li] = q * e
            ks_scr[g, li] = kk * e
        qs_scr[g, n_lvl] = q
        ks_scr[g, n_lvl] = kk
        qs_scr[g, inter] = q * jnp.exp2(sums_scr[g, 0:t]).astype(BF16)
        ks_scr[g, inter] = kk * jnp.exp2(sums_scr[g, t:2 * t]).astype(BF16)

    in_level = [m_ref[li] != 0.0 for li in range(n_lvl + 1)]
    for g in range(n_blk):
        for h in range(H_B):
            acc = jnp.zeros((t, t), F32)
            for li in range(n_lvl + 1):
                part = lax.dot_general(
                    qs_scr[g, li, :, head(h)], ks_scr[g, li, :, head(h)],
                    CONTRACT_LAST, preferred_element_type=F32)
                acc = jnp.where(in_level[li], part, acc)
            attn_scr[g, h] = acc.astype(BF16)

    for g in range(n_blk):
        for h in range(H_B):
            iv = c_ref[blk(g), head(h)]
            oi_scr[blk(g), head(h)] = jnp.dot(attn_scr[g, h], iv, preferred_element_type=F32)
            u_scr[g, h] = jnp.dot(iv.astype(F32).T.astype(BF16), ks_scr[g, inter, :, head(h)],
                                  preferred_element_type=F32)

    for g in range(n_blk):
        e_last = jnp.exp2(sums_scr[g, t - 1:t, :])
        for h in range(H_B):
            st = st_ref[h]
            o = oi_scr[blk(g), head(h)] + lax.dot_general(
                qs_scr[g, inter, :, head(h)], st.astype(BF16), CONTRACT_LAST,
                preferred_element_type=F32)
            st_ref[h] = e_last[:, head(h)] * st + u_scr[g, h]
            ms = jnp.mean(o * o, axis=-1, keepdims=True)
            o_ref[blk(g), head(h)] = (o * lax.rsqrt(ms + EPS) * g_ref[:, head(h)]).astype(o_ref.dtype)

    @pl.when(ci == pl.num_programs(1) - 1)
    def _():
        for h in range(H_B):
            sout_ref[h] = st_ref[h].T


def hgrn2(a, b, c, s0, layer, lbc, g, t, n_blk):
    bsz, rows, _ = a.shape
    w, masks, n_lvl = _hgrn2_constants(t)
    step_rows = n_blk * t
    tok = pl.BlockSpec((None, step_rows, MIX_WIDTH), lambda bi, ci: (bi, ci, 0))
    state = pl.BlockSpec((None, H_B, DK_B, DK_B), lambda bi, ci: (bi, 0, 0, 0))
    state_in = pl.BlockSpec((None, None, H_B, DK_B, DK_B), lambda bi, ci: (layer, bi, 0, 0, 0))
    scratch = [pltpu.VMEM((H_B, DK_B, DK_B), F32),
               pltpu.VMEM((n_blk, 3 * t, MIX_WIDTH), BF16),
               pltpu.VMEM((step_rows, MIX_WIDTH), F32),
               pltpu.VMEM((n_blk, (n_lvl + 2) * t, MIX_WIDTH), F32),
               pltpu.VMEM((n_blk, n_lvl + 2, t, MIX_WIDTH), BF16),
               pltpu.VMEM((n_blk, n_lvl + 2, t, MIX_WIDTH), BF16),
               pltpu.VMEM((n_blk, H_B, t, t), BF16),
               pltpu.VMEM((n_blk, H_B, DK_B, DK_B), F32),
               pltpu.VMEM((step_rows, MIX_WIDTH), F32)]
    return pl.pallas_call(
        functools.partial(_hgrn2_kernel, t=t, n_lvl=n_lvl, n_blk=n_blk),
        grid=(bsz, rows // step_rows),
        in_specs=[tok, tok, tok, state_in,
                  pl.BlockSpec(lbc.shape, lambda bi, ci: (0, 0)),
                  pl.BlockSpec((1, MIX_WIDTH), lambda bi, ci: (0, 0)),
                  pl.BlockSpec(w.shape, lambda bi, ci: (0, 0)),
                  pl.BlockSpec(masks.shape, lambda bi, ci: (0, 0, 0))],
        out_specs=[tok, state],
        out_shape=[jax.ShapeDtypeStruct((bsz, rows, MIX_WIDTH), BF16),
                   jax.ShapeDtypeStruct((bsz, H_B, DK_B, DK_B), F32)],
        scratch_shapes=scratch,
        name=f"hgrn2_t{t}",
        compiler_params=_params("arbitrary", "arbitrary"),
    )(a, b, c, s0, lbc, g.reshape(1, MIX_WIDTH), w, masks)


def _post_kernel(x_ref, mix_ref, xq_ref, gate_ref, mkt_ref, mvt_ref, w_ref, fg_ref, o_ref,
                 s_scr, p_scr, cross_scr, *, final, n_seg):
    seg_rows = x_ref.shape[0] // n_seg
    n_pairs = H_X // 2
    for seg in range(n_seg):
        rows = slice(seg * seg_rows, (seg + 1) * seg_rows)
        for hp in range(n_pairs):
            lanes = slice(LANES * hp, LANES * (hp + 1))
            q2 = _stack_pair(xq_ref[rows, lanes] * (HD_X ** -0.5))
            s_scr[seg * n_pairs + hp] = jnp.dot(q2, mkt_ref[seg, lanes, :].astype(BF16),
                                                preferred_element_type=F32)
    for i in range(n_seg * n_pairs):
        s = s_scr[i]
        p_scr[i] = jnp.exp(s - jnp.max(s, axis=-1, keepdims=True)).astype(BF16)
    ones = jnp.ones((LANES, N_MEM), BF16)
    for seg in range(n_seg):
        rows = slice(seg * seg_rows, (seg + 1) * seg_rows)
        for hp in range(n_pairs):
            lanes = slice(LANES * hp, LANES * (hp + 1))
            cross_scr[rows, lanes] = _pair_values_t(
                p_scr[seg * n_pairs + hp],
                jnp.concatenate([mvt_ref[seg, lanes, :].astype(BF16), ones], axis=0))

    gate = gate_ref[...]
    sg = gate / (1.0 + jnp.exp(-gate))
    y_mix = (mix_ref[...].astype(F32) * sg[:, 0:MIX_WIDTH]).astype(BF16)
    y_cross = (cross_scr[...] * sg[:, MIX_WIDTH:]).astype(BF16)
    acc = (x_ref[...] + jnp.dot(y_mix, w_ref[0:MIX_WIDTH, :], preferred_element_type=F32)
           + jnp.dot(y_cross, w_ref[MIX_WIDTH:, :], preferred_element_type=F32))
    if final:
        ms = jnp.mean(acc * acc, axis=-1, keepdims=True)
        acc = acc * lax.rsqrt(ms + EPS) * fg_ref[...]
    o_ref[...] = acc


def post(x, mix, xq, gate, mkt, mvt, layer, w_bf16, final_g, tm, n_seg, final):
    rows = x.shape[0]
    seg_rows = tm // n_seg
    n_items = n_seg * (H_X // 2)

    def tok(n):
        return pl.BlockSpec((tm, n), lambda i: (i, 0))

    mem = pl.BlockSpec((None, n_seg, X_WIDTH, N_MEM), lambda i: (layer, 0, 0, 0))
    return pl.pallas_call(
        functools.partial(_post_kernel, final=final, n_seg=n_seg),
        grid=(rows // tm,),
        in_specs=[tok(D_MODEL), tok(MIX_WIDTH), tok(X_WIDTH), tok(D_INNER), mem, mem,
                  pl.BlockSpec((D_INNER, D_MODEL), lambda i: (0, 0)),
                  pl.BlockSpec((1, D_MODEL), lambda i: (0, 0))],
        out_specs=tok(D_MODEL),
        out_shape=jax.ShapeDtypeStruct((rows, D_MODEL), F32),
        scratch_shapes=[pltpu.VMEM((n_items, 2 * seg_rows, N_MEM), F32),
                        pltpu.VMEM((n_items, 2 * seg_rows, N_MEM), BF16),
                        pltpu.VMEM((tm, X_WIDTH), F32)],
        name=f"post_{rows}",
        compiler_params=_params("arbitrary"),
    )(x, mix, xq, gate, mkt, mvt, w_bf16, final_g.reshape(1, D_MODEL))


def _rel_bias(table):
    band = PAST_BAND + CHUNK
    n_diag = band + CHUNK - 1
    offs = np.arange(n_diag) - (CHUNK - 1)
    idx = np.clip(PAST_BAND - offs, -REL_CLIP, REL_CLIP) + REL_CLIP
    diag = jnp.pad(table[:, idx].astype(F32), ((0, 0), (0, 1)))
    skew = jnp.tile(diag, (1, CHUNK))[:, :CHUNK * n_diag].reshape(H_A, CHUNK, n_diag)
    return skew[:, :, CHUNK - 1:CHUNK - 1 + band]


def _per_head_transposed(cache):
    return jnp.moveaxis(cache, -3, -1)


def kernel(x_prompt, x_sample, cache_a_k, cache_a_v, state_b, cache_mem_k, cache_mem_v, mem_prompt,
           ln_g, w_in, w_out, rel_bias_table, lower_bounds, hgrn_norm_g, mem_norm_g, w_mem_kv, final_g):
    bp, seq, _ = x_prompt.shape
    bs, dec_seq, _ = x_sample.shape
    assert bp == 1
    cache_len = cache_a_k.shape[2]
    n_s = bs * dec_seq
    keep = min(PAST_BAND, seq)

    w_in_b = w_in.astype(BF16)
    w_out_b = w_out.astype(BF16)
    w_mem_b = w_mem_kv.astype(BF16)

    lb_all = jnp.cumsum(jax.nn.softmax(lower_bounds.astype(F32), axis=0), axis=0)
    lb_all = lb_all - lb_all[:1]

    cache_a_kt = _per_head_transposed(cache_a_k)
    cache_a_vt = _per_head_transposed(cache_a_v)
    cache_mem_kt = _per_head_transposed(cache_mem_k).reshape(DEPTH, bs, X_WIDTH, N_MEM)
    cache_mem_vt = _per_head_transposed(cache_mem_v).reshape(DEPTH, bs, X_WIDTH, N_MEM)
    zero_state = jnp.zeros((1, 1, H_B, DK_B, DK_B), F32)

    def per_batch(u):
        return u.reshape(bs, dec_seq, u.shape[-1])

    xp = x_prompt.reshape(seq, D_MODEL)
    xs = x_sample.reshape(n_s, D_MODEL)
    ak_p, av_p, sb_p, mk_pl, mv_pl, ak_s, av_s, sb_s = [], [], [], [], [], [], [], []
    for l in range(DEPTH):
        j = l // 2
        final = l == DEPTH - 1
        attn_layer = l % 2 == 0
        mkt_p, mvt_p = norm_proj(mem_prompt.reshape(N_MEM, D_MODEL), mem_norm_g[l], w_mem_b[l],
                                 (X_WIDTH, X_WIDTH), (F32, F32), N_MEM, transpose=True)
        dtypes = (BF16, BF16 if attn_layer else F32, BF16, BF16, F32)
        tails = (1, 2) if attn_layer else ()
        outs_p = norm_proj(xp, ln_g[l], w_in_b[l], PROJ_SPLITS, dtypes, PAST_BAND, tails)
        outs_s = norm_proj(xs, ln_g[l], w_in_b[l], PROJ_SPLITS, dtypes, n_s, tails)
        a_p, b_p, c_p, xq_p, g_p = outs_p[:5]
        a_s, b_s, c_s, xq_s, g_s = outs_s[:5]

        if attn_layer:
            bias = _rel_bias(rel_bias_table[j])
            band = PAST_BAND + CHUNK
            o_p = band_attn_prompt(a_p, b_p, c_p, bias.reshape(H_A // 2, 2 * CHUNK, band), PAST_BAND)
            o_s = band_attn_sample(per_batch(a_s), per_batch(b_s), per_batch(c_s),
                                   cache_a_kt, cache_a_vt, j,
                                   bias[:, :dec_seq, :cache_len + dec_seq].reshape(
                                       H_A // 2, 2 * dec_seq, cache_len + dec_seq))
            assert keep == PAST_BAND
            ak_p.append(outs_p[5].reshape(1, keep, H_A, HD_A))
            av_p.append(outs_p[6].reshape(1, keep, H_A, HD_A))
            ak_s.append(outs_s[5].reshape(bs, dec_seq, H_A, HD_A))
            av_s.append(outs_s[6].reshape(bs, dec_seq, H_A, HD_A))
        else:
            lb = lb_all[j]
            lbc = jnp.stack([jnp.log(lb), jnp.log1p(-lb), 1.0 - lb])
            o_p, s_p = hgrn2(a_p.reshape(1, seq, MIX_WIDTH), b_p.reshape(1, seq, MIX_WIDTH),
                             c_p.reshape(1, seq, MIX_WIDTH), zero_state, 0, lbc, hgrn_norm_g[j],
                             CHUNK, 4)
            o_s, s_s = hgrn2(per_batch(a_s), per_batch(b_s), per_batch(c_s),
                             state_b.astype(F32), j, lbc, hgrn_norm_g[j], dec_seq, 1)
            sb_p.append(s_p)
            sb_s.append(s_s)
        xp = post(xp, o_p.reshape(seq, MIX_WIDTH), xq_p, g_p, mkt_p.reshape(1, 1, X_WIDTH, N_MEM),
                  mvt_p.reshape(1, 1, X_WIDTH, N_MEM), 0, w_out_b[l], final_g, 512, 1, final)
        xs = post(xs, o_s.reshape(n_s, MIX_WIDTH), xq_s, g_s, cache_mem_kt, cache_mem_vt, l,
                  w_out_b[l], final_g, n_s, bs, final)
        mk_pl.append(jnp.moveaxis(mkt_p.reshape(1, H_X, HD_X, N_MEM), -1, 1))
        mv_pl.append(jnp.moveaxis(mvt_p.reshape(1, H_X, HD_X, N_MEM), -1, 1))
    return (xp.reshape(1, seq, D_MODEL), xs.reshape(bs, dec_seq, D_MODEL),
            jnp.stack(ak_p), jnp.stack(av_p), jnp.stack(sb_p), jnp.stack(mk_pl), jnp.stack(mv_pl),
            jnp.stack(ak_s), jnp.stack(av_s), jnp.stack(sb_s))
```

```python
import functools

import numpy as np
import jax
import jax.numpy as jnp
from jax import lax
from jax.experimental import pallas as pl
from jax.experimental.pallas import tpu as pltpu

D_MODEL = 1024
DEPTH = 4
CHUNK = 64
N_PAST_CHUNKS = 8
PAST_BAND = N_PAST_CHUNKS * CHUNK
MIX_WIDTH = 768
X_WIDTH = 256
D_INNER = MIX_WIDTH + X_WIDTH
HD_A = 64
H_A = MIX_WIDTH // HD_A
REL_CLIP = 128
DK_B = 128
H_B = MIX_WIDTH // DK_B
H_X = 4
HD_X = 64
N_MEM = 256
EPS = 1e-6
NEG = -1e30
LOG2_E = 1.4426950408889634
F32 = jnp.float32
BF16 = jnp.bfloat16

LANES = 128
VMEM_LIMIT_BYTES = 56 * 1024 * 1024
PROJ_SPLITS = (MIX_WIDTH, MIX_WIDTH, MIX_WIDTH, X_WIDTH, D_INNER)
CONTRACT_LAST = (((1,), (1,)), ((), ()))
HGRN_BLOCKS_PER_STEP = 8
BAND_UNROLL = 4


def _params(*sem):
    return pltpu.CompilerParams(dimension_semantics=sem, vmem_limit_bytes=VMEM_LIMIT_BYTES)


def _norm_proj_kernel(x_ref, g_ref, w_ref, *out_refs, splits, tails, transpose):
    x = x_ref[...]
    ms = jnp.mean(x * x, axis=-1, keepdims=True)
    xn = (x * lax.rsqrt(ms + EPS) * g_ref[...]).astype(BF16)
    tail_refs = out_refs[len(splits):]
    off = 0
    for idx, (o_ref, n) in enumerate(zip(out_refs, splits)):
        r = jnp.dot(xn, w_ref[:, off:off + n], preferred_element_type=F32)
        o_ref[...] = (r.T if transpose else r).astype(o_ref.dtype)
        if idx in tails:
            tail_refs[tails.index(idx)][...] = r
        off += n


def norm_proj(x2d, g, w_all, layer, splits, dtypes, tm, tails=(), transpose=False):
    rows, d = x2d.shape
    n_total = w_all.shape[2]
    assert rows % tm == 0 and (not transpose or rows == tm)
    if transpose:
        out_specs = [pl.BlockSpec((n, tm), lambda i: (0, 0)) for n in splits]
        out_shape = [jax.ShapeDtypeStruct((n, rows), dt) for n, dt in zip(splits, dtypes)]
    else:
        out_specs = [pl.BlockSpec((tm, n), lambda i: (i, 0)) for n in splits]
        out_shape = [jax.ShapeDtypeStruct((rows, n), dt) for n, dt in zip(splits, dtypes)]
    out_specs += [pl.BlockSpec((tm, splits[idx]), lambda i: (0, 0)) for idx in tails]
    out_shape += [jax.ShapeDtypeStruct((tm, splits[idx]), F32) for idx in tails]
    return pl.pallas_call(
        functools.partial(_norm_proj_kernel, splits=splits, tails=tuple(tails), transpose=transpose),
        grid=(rows // tm,),
        in_specs=[
            pl.BlockSpec((tm, d), lambda i: (i, 0)),
            pl.BlockSpec((1, d), lambda i: (0, 0)),
            pl.BlockSpec((None, d, n_total), lambda i: (layer, 0, 0)),
        ],
        out_specs=out_specs,
        out_shape=out_shape,
        name=f"norm_proj_{rows}x{n_total}",
        compiler_params=_params("arbitrary"),
    )(x2d, g.reshape(1, d), w_all)


def _stack_pair(q_pair):
    first = lax.broadcasted_iota(jnp.int32, (1, LANES), 1) < HD_A
    keep0 = jnp.where(first, 1.0, 0.0).astype(BF16)
    keep1 = jnp.where(first, 0.0, 1.0).astype(BF16)
    return jnp.concatenate([q_pair * keep0, q_pair * keep1], axis=0)


def _unstack_pair(o2):
    m = o2.shape[0] // 2
    first = lax.broadcasted_iota(jnp.int32, (m, LANES), 1) < HD_A
    return jnp.where(first, o2[:m], o2[m:])


def _pair_values(p, v_ext):
    r = jnp.dot(p, v_ext, preferred_element_type=F32)
    return _unstack_pair(r[:, :LANES] / r[:, LANES:])


def _pair_values_t(p, vt_ext):
    r = lax.dot_general(p, vt_ext, CONTRACT_LAST, preferred_element_type=F32)
    return _unstack_pair(r[:, :LANES] / r[:, LANES:])


def _band_attn_kernel(q_ref, k_ref, v_ref, bias_ref, o_ref,
                      kwin, vext, s_scr, p_scr, *, n_chunks, unroll):
    step = pl.program_id(0)
    chunk, prev_rows = CHUNK, PAST_BAND
    cur_rows = n_chunks * chunk
    band = prev_rows + chunk
    n_pairs = H_A // 2
    assert cur_rows == prev_rows

    @pl.when(step == 0)
    def _():
        kwin[...] = jnp.zeros_like(kwin)
        vext[...] = jnp.zeros_like(vext)

    kwin[0:prev_rows, :] = kwin[prev_rows:prev_rows + cur_rows, :]
    kwin[prev_rows:prev_rows + cur_rows, :] = k_ref[...]
    ones = jnp.ones((prev_rows + cur_rows, LANES), BF16)
    for hp in range(n_pairs):
        lanes = slice(LANES * hp, LANES * (hp + 1))
        vcol = slice(2 * LANES * hp, 2 * LANES * hp + LANES)
        vext[0:prev_rows, vcol] = vext[prev_rows:prev_rows + cur_rows, vcol]
        vext[prev_rows:prev_rows + cur_rows, vcol] = v_ref[:, lanes]
        vext[:, 2 * LANES * hp + LANES:2 * LANES * (hp + 1)] = ones

    def chunks_body(ci, carry):
        starts = [pl.multiple_of((ci * unroll + u) * chunk, chunk) for u in range(unroll)]
        for u, r0 in enumerate(starts):
            for hp in range(n_pairs):
                lanes = slice(LANES * hp, LANES * (hp + 1))
                q2 = _stack_pair(q_ref[pl.ds(r0, chunk), lanes] * (HD_A ** -0.5))
                s_scr[u * n_pairs + hp] = lax.dot_general(
                    q2, kwin[pl.ds(r0, band), lanes], CONTRACT_LAST, preferred_element_type=F32)
        for u, r0 in enumerate(starts):
            key_row = lax.broadcasted_iota(jnp.int32, (1, band), 1) + r0 + (step - 1) * prev_rows
            neg = jnp.where(key_row < 0, NEG, 0.0)
            for hp in range(n_pairs):
                s = s_scr[u * n_pairs + hp] + bias_ref[hp] + neg
                m = jnp.max(s, axis=-1, keepdims=True)
                p_scr[u * n_pairs + hp] = jnp.exp(s - m).astype(BF16)
        for u, r0 in enumerate(starts):
            for hp in range(n_pairs):
                o_ref[pl.ds(r0, chunk), LANES * hp:LANES * (hp + 1)] = _pair_values(
                    p_scr[u * n_pairs + hp],
                    vext[pl.ds(r0, band), 2 * LANES * hp:2 * LANES * (hp + 1)]
                ).astype(o_ref.dtype)
        return carry

    lax.fori_loop(0, n_chunks // unroll, chunks_body, 0)


def band_attn_prompt(q, k, v, bias, block_rows):
    rows = q.shape[0]
    n_chunks = block_rows // CHUNK
    n_pairs = H_A // 2
    band = PAST_BAND + CHUNK
    assert block_rows == PAST_BAND
    cur = pl.BlockSpec((block_rows, MIX_WIDTH), lambda i: (i, 0))
    return pl.pallas_call(
        functools.partial(_band_attn_kernel, n_chunks=n_chunks, unroll=BAND_UNROLL),
        grid=(rows // block_rows,),
        in_specs=[cur, cur, cur, pl.BlockSpec(bias.shape, lambda i: (0, 0, 0))],
        out_specs=cur,
        out_shape=jax.ShapeDtypeStruct((rows, MIX_WIDTH), BF16),
        scratch_shapes=[pltpu.VMEM((PAST_BAND + block_rows, MIX_WIDTH), BF16),
                        pltpu.VMEM((PAST_BAND + block_rows, 2 * MIX_WIDTH), BF16),
                        pltpu.VMEM((BAND_UNROLL * n_pairs, 2 * CHUNK, band), F32),
                        pltpu.VMEM((BAND_UNROLL * n_pairs, 2 * CHUNK, band), BF16)],
        name="band_attn_prompt",
        compiler_params=_params("arbitrary"),
    )(q, k, v, bias)


def _band_sample_kernel(q_ref, k_ref, v_ref, kct_ref, vct_ref, bias_ref, o_ref, s_scr, p_scr,
                        *, cache_len):
    n_pairs = H_A // 2
    s_len = q_ref.shape[0]
    for hp in range(n_pairs):
        lanes = slice(LANES * hp, LANES * (hp + 1))
        q2 = _stack_pair(q_ref[:, lanes] * (HD_A ** -0.5))
        kct = kct_ref[2 * hp:2 * hp + 2].reshape(LANES, cache_len).astype(BF16)
        s_scr[hp, :, 0:cache_len] = jnp.dot(q2, kct, preferred_element_type=F32)
        s_scr[hp, :, cache_len:] = lax.dot_general(q2, k_ref[:, lanes], CONTRACT_LAST,
                                                   preferred_element_type=F32)
    for hp in range(n_pairs):
        s = s_scr[hp] + bias_ref[hp]
        p_scr[hp] = jnp.exp(s - jnp.max(s, axis=-1, keepdims=True)).astype(BF16)
    ones_old = jnp.ones((LANES, cache_len), BF16)
    ones_new = jnp.ones((s_len, LANES), BF16)
    for hp in range(n_pairs):
        lanes = slice(LANES * hp, LANES * (hp + 1))
        vct = vct_ref[2 * hp:2 * hp + 2].reshape(LANES, cache_len).astype(BF16)
        r = (lax.dot_general(p_scr[hp, :, 0:cache_len], jnp.concatenate([vct, ones_old], axis=0),
                             CONTRACT_LAST, preferred_element_type=F32)
             + jnp.dot(p_scr[hp, :, cache_len:], jnp.concatenate([v_ref[:, lanes], ones_new], axis=1),
                       preferred_element_type=F32))
        o_ref[:, lanes] = _unstack_pair(r[:, :LANES] / r[:, LANES:]).astype(o_ref.dtype)


def band_attn_sample(q, k, v, k_cache_t, v_cache_t, layer, bias):
    b, s_len, _ = q.shape
    cache_len = k_cache_t.shape[-1]
    new = pl.BlockSpec((None, s_len, MIX_WIDTH), lambda i: (i, 0, 0))
    old = pl.BlockSpec((None, None, H_A, HD_A, cache_len), lambda i: (layer, i, 0, 0, 0))
    return pl.pallas_call(
        functools.partial(_band_sample_kernel, cache_len=cache_len),
        grid=(b,),
        in_specs=[new, new, new, old, old, pl.BlockSpec(bias.shape, lambda i: (0, 0, 0))],
        out_specs=new,
        out_shape=jax.ShapeDtypeStruct((b, s_len, MIX_WIDTH), BF16),
        scratch_shapes=[pltpu.VMEM((H_A // 2, 2 * s_len, cache_len + s_len), F32),
                        pltpu.VMEM((H_A // 2, 2 * s_len, cache_len + s_len), BF16)],
        name="band_attn_sample",
        compiler_params=_params("arbitrary"),
    )(q, k, v, k_cache_t, v_cache_t, bias)


def _hgrn2_constants(t):
    halves = []
    h = t // 2
    while h >= 1:
        halves.append(h)
        h //= 2
    n_lvl = len(halves)
    w = np.zeros(((n_lvl + 2) * t, t), np.float32)
    masks = np.zeros((n_lvl + 1, t, t), np.float32)
    for row in range(t):
        w[row, :row + 1] = 1.0
        w[t + row, row + 1:] = 1.0
    for li, h in enumerate(halves):
        base = (2 + li) * t
        for row in range(t):
            r = (row // (2 * h)) * 2 * h + h - 1
            if row > r:
                w[base + row, r + 1:row + 1] = 1.0
            else:
                w[base + row, row + 1:r + 1] = 1.0
        for tq in range(t):
            for sk in range(t):
                if tq // (2 * h) == sk // (2 * h):
                    r = (tq // (2 * h)) * 2 * h + h - 1
                    if tq > r and sk <= r:
                        masks[li, tq, sk] = 1.0
    masks[n_lvl] = np.eye(t, dtype=np.float32)
    return jnp.asarray(np.tile(w, (1, 3)), BF16), jnp.asarray(masks, F32), n_lvl


def _hgrn2_kernel(a_ref, b_ref, c_ref, s0_ref, lbc_ref, g_ref, w_ref, m_ref,
                  o_ref, sout_ref,
                  st_ref, lf_scr, kk_scr, sums_scr, qs_scr, ks_scr, attn_scr, u_scr, oi_scr,
                  *, t, n_lvl, n_blk):
    ci = pl.program_id(1)
    inter = n_lvl + 1

    @pl.when(ci == 0)
    def _():
        for h in range(H_B):
            st_ref[h] = s0_ref[h].T

    def blk(g):
        return slice(g * t, (g + 1) * t)

    def head(h):
        return slice(LANES * h, LANES * (h + 1))

    z = b_ref[...]
    log_lb = lbc_ref[0:1, :]
    u = lbc_ref[1:2, :] + jnp.minimum(z, 0.0) - jnp.log(1.0 + jnp.exp(-jnp.abs(z)))
    log_f = jnp.maximum(log_lb, u) + jnp.log(1.0 + jnp.exp(-jnp.abs(log_lb - u)))
    kk_scr[...] = lbc_ref[2:3, :] / (1.0 + jnp.exp(z))
    log2_f = log_f * LOG2_E
    hi = log2_f.astype(BF16)
    r1 = log2_f - hi.astype(F32)
    mid = r1.astype(BF16)
    lo = (r1 - mid.astype(F32)).astype(BF16)
    for g in range(n_blk):
        lf_scr[g, 0:t] = hi[blk(g)]
        lf_scr[g, t:2 * t] = mid[blk(g)]
        lf_scr[g, 2 * t:3 * t] = lo[blk(g)]

    for g in range(n_blk):
        sums_scr[g] = jnp.dot(w_ref[...], lf_scr[g], preferred_element_type=F32)

    for g in range(n_blk):
        q = a_ref[blk(g), :]
        kk = kk_scr[blk(g), :].astype(BF16)
        for li in range(n_lvl):
            e = jnp.exp2(sums_scr[g, (2 + li) * t:(3 + li) * t]).astype(BF16)
            qs_scr[g, li] = q * e
            ks_scr[g, li] = kk * e
        qs_scr[g, n_lvl] = q
        ks_scr[g, n_lvl] = kk
        qs_scr[g, inter] = q * jnp.exp2(sums_scr[g, 0:t]).astype(BF16)
        ks_scr[g, inter] = kk * jnp.exp2(sums_scr[g, t:2 * t]).astype(BF16)

    in_level = [m_ref[li] != 0.0 for li in range(n_lvl + 1)]
    for g in range(n_blk):
        for h in range(H_B):
            acc = jnp.zeros((t, t), F32)
            for li in range(n_lvl + 1):
                part = lax.dot_general(
                    qs_scr[g, li, :, head(h)], ks_scr[g, li, :, head(h)],
                    CONTRACT_LAST, preferred_element_type=F32)
                acc = jnp.where(in_level[li], part, acc)
            attn_scr[g, h] = acc.astype(BF16)

    for g in range(n_blk):
        for h in range(H_B):
            iv = c_ref[blk(g), head(h)]
            oi_scr[blk(g), head(h)] = jnp.dot(attn_scr[g, h], iv, preferred_element_type=F32)
            u_scr[g, h] = jnp.dot(iv.astype(F32).T.astype(BF16), ks_scr[g, inter, :, head(h)],
                                  preferred_element_type=F32)

    for g in range(n_blk):
        e_last = jnp.exp2(sums_scr[g, t - 1:t, :])
        for h in range(H_B):
            st = st_ref[h]
            o = oi_scr[blk(g), head(h)] + lax.dot_general(
                qs_scr[g, inter, :, head(h)], st.astype(BF16), CONTRACT_LAST,
                preferred_element_type=F32)
            st_ref[h] = e_last[:, head(h)] * st + u_scr[g, h]
            ms = jnp.mean(o * o, axis=-1, keepdims=True)
            o_ref[blk(g), head(h)] = (o * lax.rsqrt(ms + EPS) * g_ref[:, head(h)]).astype(o_ref.dtype)

    @pl.when(ci == pl.num_programs(1) - 1)
    def _():
        for h in range(H_B):
            sout_ref[h] = st_ref[h].T


def hgrn2(a, b, c, s0, layer, lbc, g, t, n_blk):
    bsz, rows, _ = a.shape
    w, masks, n_lvl = _hgrn2_constants(t)
    step_rows = n_blk * t
    tok = pl.BlockSpec((None, step_rows, MIX_WIDTH), lambda bi, ci: (bi, ci, 0))
    state = pl.BlockSpec((None, H_B, DK_B, DK_B), lambda bi, ci: (bi, 0, 0, 0))
    state_in = pl.BlockSpec((None, None, H_B, DK_B, DK_B), lambda bi, ci: (layer, bi, 0, 0, 0))
    scratch = [pltpu.VMEM((H_B, DK_B, DK_B), F32),
               pltpu.VMEM((n_blk, 3 * t, MIX_WIDTH), BF16),
               pltpu.VMEM((step_rows, MIX_WIDTH), F32),
               pltpu.VMEM((n_blk, (n_lvl + 2) * t, MIX_WIDTH), F32),
               pltpu.VMEM((n_blk, n_lvl + 2, t, MIX_WIDTH), BF16),
               pltpu.VMEM((n_blk, n_lvl + 2, t, MIX_WIDTH), BF16),
               pltpu.VMEM((n_blk, H_B, t, t), BF16),
               pltpu.VMEM((n_blk, H_B, DK_B, DK_B), F32),
               pltpu.VMEM((step_rows, MIX_WIDTH), F32)]
    return pl.pallas_call(
        functools.partial(_hgrn2_kernel, t=t, n_lvl=n_lvl, n_blk=n_blk),
        grid=(bsz, rows // step_rows),
        in_specs=[tok, tok, tok, state_in,
                  pl.BlockSpec(lbc.shape, lambda bi, ci: (0, 0)),
                  pl.BlockSpec((1, MIX_WIDTH), lambda bi, ci: (0, 0)),
                  pl.BlockSpec(w.shape, lambda bi, ci: (0, 0)),
                  pl.BlockSpec(masks.shape, lambda bi, ci: (0, 0, 0))],
        out_specs=[tok, state],
        out_shape=[jax.ShapeDtypeStruct((bsz, rows, MIX_WIDTH), BF16),
                   jax.ShapeDtypeStruct((bsz, H_B, DK_B, DK_B), F32)],
        scratch_shapes=scratch,
        name=f"hgrn2_t{t}",
        compiler_params=_params("arbitrary", "arbitrary"),
    )(a, b, c, s0, lbc, g.reshape(1, MIX_WIDTH), w, masks)


def _post_kernel(x_ref, mix_ref, xq_ref, gate_ref, mkt_ref, mvt_ref, w_ref, fg_ref, o_ref,
                 s_scr, p_scr, cross_scr, *, final, n_seg):
    seg_rows = x_ref.shape[0] // n_seg
    n_pairs = H_X // 2
    for seg in range(n_seg):
        rows = slice(seg * seg_rows, (seg + 1) * seg_rows)
        for hp in range(n_pairs):
            lanes = slice(LANES * hp, LANES * (hp + 1))
            q2 = _stack_pair(xq_ref[rows, lanes] * (HD_X ** -0.5))
            s_scr[seg * n_pairs + hp] = jnp.dot(q2, mkt_ref[seg, lanes, :].astype(BF16),
                                                preferred_element_type=F32)
    for i in range(n_seg * n_pairs):
        s = s_scr[i]
        p_scr[i] = jnp.exp(s - jnp.max(s, axis=-1, keepdims=True)).astype(BF16)
    ones = jnp.ones((LANES, N_MEM), BF16)
    for seg in range(n_seg):
        rows = slice(seg * seg_rows, (seg + 1) * seg_rows)
        for hp in range(n_pairs):
            lanes = slice(LANES * hp, LANES * (hp + 1))
            cross_scr[rows, lanes] = _pair_values_t(
                p_scr[seg * n_pairs + hp],
                jnp.concatenate([mvt_ref[seg, lanes, :].astype(BF16), ones], axis=0))

    gate = gate_ref[...]
    sg = gate / (1.0 + jnp.exp(-gate))
    y_mix = (mix_ref[...].astype(F32) * sg[:, 0:MIX_WIDTH]).astype(BF16)
    y_cross = (cross_scr[...] * sg[:, MIX_WIDTH:]).astype(BF16)
    acc = (x_ref[...] + jnp.dot(y_mix, w_ref[0:MIX_WIDTH, :], preferred_element_type=F32)
           + jnp.dot(y_cross, w_ref[MIX_WIDTH:, :], preferred_element_type=F32))
    if final:
        ms = jnp.mean(acc * acc, axis=-1, keepdims=True)
        acc = acc * lax.rsqrt(ms + EPS) * fg_ref[...]
    o_ref[...] = acc


def post(x, mix, xq, gate, mkt, mvt, mem_layer, w_all, w_layer, final_g, tm, n_seg, final):
    rows = x.shape[0]
    seg_rows = tm // n_seg
    n_items = n_seg * (H_X // 2)

    def tok(n):
        return pl.BlockSpec((tm, n), lambda i: (i, 0))

    mem = pl.BlockSpec((None, n_seg, X_WIDTH, N_MEM), lambda i: (mem_layer, 0, 0, 0))
    return pl.pallas_call(
        functools.partial(_post_kernel, final=final, n_seg=n_seg),
        grid=(rows // tm,),
        in_specs=[tok(D_MODEL), tok(MIX_WIDTH), tok(X_WIDTH), tok(D_INNER), mem, mem,
                  pl.BlockSpec((None, D_INNER, D_MODEL), lambda i: (w_layer, 0, 0)),
                  pl.BlockSpec((1, D_MODEL), lambda i: (0, 0))],
        out_specs=tok(D_MODEL),
        out_shape=jax.ShapeDtypeStruct((rows, D_MODEL), F32),
        scratch_shapes=[pltpu.VMEM((n_items, 2 * seg_rows, N_MEM), F32),
                        pltpu.VMEM((n_items, 2 * seg_rows, N_MEM), BF16),
                        pltpu.VMEM((tm, X_WIDTH), F32)],
        name=f"post_{rows}",
        compiler_params=_params("arbitrary"),
    )(x, mix, xq, gate, mkt, mvt, w_all, final_g.reshape(1, D_MODEL))


def _rel_bias(table):
    band = PAST_BAND + CHUNK
    n_diag = band + CHUNK - 1
    offs = np.arange(n_diag) - (CHUNK - 1)
    idx = np.clip(PAST_BAND - offs, -REL_CLIP, REL_CLIP) + REL_CLIP
    diag = jnp.pad(table[:, idx].astype(F32), ((0, 0), (0, 1)))
    skew = jnp.tile(diag, (1, CHUNK))[:, :CHUNK * n_diag].reshape(H_A, CHUNK, n_diag)
    return skew[:, :, CHUNK - 1:CHUNK - 1 + band]


def _per_head_transposed(cache):
    return jnp.moveaxis(cache, -3, -1)


def kernel(x_prompt, x_sample, cache_a_k, cache_a_v, state_b, cache_mem_k, cache_mem_v, mem_prompt,
           ln_g, w_in, w_out, rel_bias_table, lower_bounds, hgrn_norm_g, mem_norm_g, w_mem_kv, final_g):
    bp, seq, _ = x_prompt.shape
    bs, dec_seq, _ = x_sample.shape
    assert bp == 1
    cache_len = cache_a_k.shape[2]
    n_s = bs * dec_seq
    keep = min(PAST_BAND, seq)

    w_in_b = w_in.astype(BF16)
    w_out_b = w_out.astype(BF16)
    w_mem_b = w_mem_kv.astype(BF16)

    lb_all = jnp.cumsum(jax.nn.softmax(lower_bounds.astype(F32), axis=0), axis=0)
    lb_all = lb_all - lb_all[:1]

    cache_a_kt = _per_head_transposed(cache_a_k)
    cache_a_vt = _per_head_transposed(cache_a_v)
    cache_mem_kt = _per_head_transposed(cache_mem_k).reshape(DEPTH, bs, X_WIDTH, N_MEM)
    cache_mem_vt = _per_head_transposed(cache_mem_v).reshape(DEPTH, bs, X_WIDTH, N_MEM)
    zero_state = jnp.zeros((1, 1, H_B, DK_B, DK_B), F32)

    def per_batch(u):
        return u.reshape(bs, dec_seq, u.shape[-1])

    xp = x_prompt.reshape(seq, D_MODEL)
    xs = x_sample.reshape(n_s, D_MODEL)
    ak_p, av_p, sb_p, mk_pl, mv_pl, ak_s, av_s, sb_s = [], [], [], [], [], [], [], []
    for l in range(DEPTH):
        j = l // 2
        final = l == DEPTH - 1
        attn_layer = l % 2 == 0
        mkt_p, mvt_p = norm_proj(mem_prompt.reshape(N_MEM, D_MODEL), mem_norm_g[l], w_mem_b, l,
                                 (X_WIDTH, X_WIDTH), (F32, F32), N_MEM, transpose=True)
        dtypes = (BF16, BF16 if attn_layer else F32, BF16, BF16, F32)
        tails = (1, 2) if attn_layer else ()
        outs_p = norm_proj(xp, ln_g[l], w_in_b, l, PROJ_SPLITS, dtypes, PAST_BAND, tails)
        outs_s = norm_proj(xs, ln_g[l], w_in_b, l, PROJ_SPLITS, dtypes, n_s, tails)
        a_p, b_p, c_p, xq_p, g_p = outs_p[:5]
        a_s, b_s, c_s, xq_s, g_s = outs_s[:5]

        if attn_layer:
            bias = _rel_bias(rel_bias_table[j])
            band = PAST_BAND + CHUNK
            o_p = band_attn_prompt(a_p, b_p, c_p, bias.reshape(H_A // 2, 2 * CHUNK, band), PAST_BAND)
            o_s = band_attn_sample(per_batch(a_s), per_batch(b_s), per_batch(c_s),
                                   cache_a_kt, cache_a_vt, j,
                                   bias[:, :dec_seq, :cache_len + dec_seq].reshape(
                                       H_A // 2, 2 * dec_seq, cache_len + dec_seq))
            assert keep == PAST_BAND
            ak_p.append(outs_p[5].reshape(1, keep, H_A, HD_A))
            av_p.append(outs_p[6].reshape(1, keep, H_A, HD_A))
            ak_s.append(outs_s[5].reshape(bs, dec_seq, H_A, HD_A))
            av_s.append(outs_s[6].reshape(bs, dec_seq, H_A, HD_A))
        else:
            lb = lb_all[j]
            lbc = jnp.stack([jnp.log(lb), jnp.log1p(-lb), 1.0 - lb])
            o_p, s_p = hgrn2(a_p.reshape(1, seq, MIX_WIDTH), b_p.reshape(1, seq, MIX_WIDTH),
                             c_p.reshape(1, seq, MIX_WIDTH), zero_state, 0, lbc, hgrn_norm_g[j],
                             CHUNK, HGRN_BLOCKS_PER_STEP)
            o_s, s_s = hgrn2(per_batch(a_s), per_batch(b_s), per_batch(c_s),
                             state_b.astype(F32), j, lbc, hgrn_norm_g[j], dec_seq, 1)
            sb_p.append(s_p)
            sb_s.append(s_s)
        xp = post(xp, o_p.reshape(seq, MIX_WIDTH), xq_p, g_p, mkt_p.reshape(1, 1, X_WIDTH, N_MEM),
                  mvt_p.reshape(1, 1, X_WIDTH, N_MEM), 0, w_out_b, l, final_g, 512, 1, final)
        xs = post(xs, o_s.reshape(n_s, MIX_WIDTH), xq_s, g_s, cache_mem_kt, cache_mem_vt, l,
                  w_out_b, l, final_g, n_s, bs, final)
        mk_pl.append(jnp.moveaxis(mkt_p.reshape(1, H_X, HD_X, N_MEM), -1, 1))
        mv_pl.append(jnp.moveaxis(mvt_p.reshape(1, H_X, HD_X, N_MEM), -1, 1))
    return (xp.reshape(1, seq, D_MODEL), xs.reshape(bs, dec_seq, D_MODEL),
            jnp.stack(ak_p), jnp.stack(av_p), jnp.stack(sb_p), jnp.stack(mk_pl), jnp.stack(mv_pl),
            jnp.stack(ak_s), jnp.stack(av_s), jnp.stack(sb_s))
```

```python
import functools

import numpy as np
import jax
import jax.numpy as jnp
from jax import lax
from jax.experimental import pallas as pl
from jax.experimental.pallas import tpu as pltpu

D_MODEL = 1024
DEPTH = 4
CHUNK = 64
N_PAST_CHUNKS = 8
PAST_BAND = N_PAST_CHUNKS * CHUNK
MIX_WIDTH = 768
X_WIDTH = 256
D_INNER = MIX_WIDTH + X_WIDTH
HD_A = 64
H_A = MIX_WIDTH // HD_A
REL_CLIP = 128
DK_B = 128
H_B = MIX_WIDTH // DK_B
H_X = 4
HD_X = 64
N_MEM = 256
EPS = 1e-6
NEG = -1e30
LOG2_E = 1.4426950408889634
F32 = jnp.float32
BF16 = jnp.bfloat16

LANES = 128
VMEM_LIMIT_BYTES = 56 * 1024 * 1024
PROJ_SPLITS = (MIX_WIDTH, MIX_WIDTH, MIX_WIDTH, X_WIDTH, D_INNER)
CONTRACT_LAST = (((1,), (1,)), ((), ()))
HGRN_BLOCKS_PER_STEP = 8
BAND_UNROLL = 4


def _params(*sem):
    return pltpu.CompilerParams(dimension_semantics=sem, vmem_limit_bytes=VMEM_LIMIT_BYTES)


def _norm_proj_body(x, g_ref, w_ref, out_refs, splits, tails, transpose):
    ms = jnp.mean(x * x, axis=-1, keepdims=True)
    xn = (x * lax.rsqrt(ms + EPS) * g_ref[...]).astype(BF16)
    tail_refs = out_refs[len(splits):]
    off = 0
    for idx, (o_ref, n) in enumerate(zip(out_refs, splits)):
        r = jnp.dot(xn, w_ref[:, off:off + n], preferred_element_type=F32)
        o_ref[...] = (r.T if transpose else r).astype(o_ref.dtype)
        if idx in tails:
            tail_refs[tails.index(idx)][...] = r
        off += n


def _norm_proj_kernel(x_ref, g_ref, w_ref, *out_refs, splits, tails, transpose):
    _norm_proj_body(x_ref[...], g_ref, w_ref, out_refs, splits, tails, transpose)


def _proj_out_specs(rows, tm, splits, dtypes, tails):
    out_specs = [pl.BlockSpec((tm, n), lambda i: (i, 0)) for n in splits]
    out_shape = [jax.ShapeDtypeStruct((rows, n), dt) for n, dt in zip(splits, dtypes)]
    out_specs += [pl.BlockSpec((tm, splits[idx]), lambda i: (0, 0)) for idx in tails]
    out_shape += [jax.ShapeDtypeStruct((tm, splits[idx]), F32) for idx in tails]
    return out_specs, out_shape


def norm_proj(x2d, g, w_all, layer, splits, dtypes, tm, tails=(), transpose=False):
    rows, d = x2d.shape
    n_total = w_all.shape[2]
    assert rows % tm == 0 and (not transpose or rows == tm)
    if transpose:
        out_specs = [pl.BlockSpec((n, tm), lambda i: (0, 0)) for n in splits]
        out_shape = [jax.ShapeDtypeStruct((n, rows), dt) for n, dt in zip(splits, dtypes)]
    else:
        out_specs, out_shape = _proj_out_specs(rows, tm, splits, dtypes, tails)
    return pl.pallas_call(
        functools.partial(_norm_proj_kernel, splits=splits, tails=tuple(tails), transpose=transpose),
        grid=(rows // tm,),
        in_specs=[
            pl.BlockSpec((tm, d), lambda i: (i, 0)),
            pl.BlockSpec((1, d), lambda i: (0, 0)),
            pl.BlockSpec((None, d, n_total), lambda i: (layer, 0, 0)),
        ],
        out_specs=out_specs,
        out_shape=out_shape,
        name=f"norm_proj_{rows}x{n_total}",
        compiler_params=_params("arbitrary"),
    )(x2d, g.reshape(1, d), w_all)


def _stack_pair(q_pair):
    first = lax.broadcasted_iota(jnp.int32, (1, LANES), 1) < HD_A
    keep0 = jnp.where(first, 1.0, 0.0).astype(BF16)
    keep1 = jnp.where(first, 0.0, 1.0).astype(BF16)
    return jnp.concatenate([q_pair * keep0, q_pair * keep1], axis=0)


def _unstack_pair(o2):
    m = o2.shape[0] // 2
    first = lax.broadcasted_iota(jnp.int32, (m, LANES), 1) < HD_A
    return jnp.where(first, o2[:m], o2[m:])


def _pair_values(p, v_ext):
    r = jnp.dot(p, v_ext, preferred_element_type=F32)
    return _unstack_pair(r[:, :LANES] / r[:, LANES:])


def _pair_values_t(p, vt_ext):
    r = lax.dot_general(p, vt_ext, CONTRACT_LAST, preferred_element_type=F32)
    return _unstack_pair(r[:, :LANES] / r[:, LANES:])


def _band_attn_kernel(q_ref, k_ref, v_ref, bias_ref, o_ref,
                      kwin, vext, s_scr, p_scr, *, n_chunks, unroll):
    step = pl.program_id(0)
    chunk, prev_rows = CHUNK, PAST_BAND
    cur_rows = n_chunks * chunk
    band = prev_rows + chunk
    n_pairs = H_A // 2
    assert cur_rows == prev_rows

    @pl.when(step == 0)
    def _():
        kwin[...] = jnp.zeros_like(kwin)
        vext[...] = jnp.zeros_like(vext)

    kwin[0:prev_rows, :] = kwin[prev_rows:prev_rows + cur_rows, :]
    kwin[prev_rows:prev_rows + cur_rows, :] = k_ref[...]
    ones = jnp.ones((prev_rows + cur_rows, LANES), BF16)
    for hp in range(n_pairs):
        lanes = slice(LANES * hp, LANES * (hp + 1))
        vcol = slice(2 * LANES * hp, 2 * LANES * hp + LANES)
        vext[0:prev_rows, vcol] = vext[prev_rows:prev_rows + cur_rows, vcol]
        vext[prev_rows:prev_rows + cur_rows, vcol] = v_ref[:, lanes]
        vext[:, 2 * LANES * hp + LANES:2 * LANES * (hp + 1)] = ones

    def chunks_body(ci, carry):
        starts = [pl.multiple_of((ci * unroll + u) * chunk, chunk) for u in range(unroll)]
        for u, r0 in enumerate(starts):
            for hp in range(n_pairs):
                lanes = slice(LANES * hp, LANES * (hp + 1))
                q2 = _stack_pair(q_ref[pl.ds(r0, chunk), lanes] * (HD_A ** -0.5))
                s_scr[u * n_pairs + hp] = lax.dot_general(
                    q2, kwin[pl.ds(r0, band), lanes], CONTRACT_LAST, preferred_element_type=F32)
        for u, r0 in enumerate(starts):
            key_row = lax.broadcasted_iota(jnp.int32, (1, band), 1) + r0 + (step - 1) * prev_rows
            neg = jnp.where(key_row < 0, NEG, 0.0)
            for hp in range(n_pairs):
                s = s_scr[u * n_pairs + hp] + bias_ref[hp] + neg
                m = jnp.max(s, axis=-1, keepdims=True)
                p_scr[u * n_pairs + hp] = jnp.exp(s - m).astype(BF16)
        for u, r0 in enumerate(starts):
            for hp in range(n_pairs):
                o_ref[pl.ds(r0, chunk), LANES * hp:LANES * (hp + 1)] = _pair_values(
                    p_scr[u * n_pairs + hp],
                    vext[pl.ds(r0, band), 2 * LANES * hp:2 * LANES * (hp + 1)]
                ).astype(o_ref.dtype)
        return carry

    lax.fori_loop(0, n_chunks // unroll, chunks_body, 0)


def band_attn_prompt(q, k, v, bias, block_rows):
    rows = q.shape[0]
    n_chunks = block_rows // CHUNK
    n_pairs = H_A // 2
    band = PAST_BAND + CHUNK
    assert block_rows == PAST_BAND
    cur = pl.BlockSpec((block_rows, MIX_WIDTH), lambda i: (i, 0))
    return pl.pallas_call(
        functools.partial(_band_attn_kernel, n_chunks=n_chunks, unroll=BAND_UNROLL),
        grid=(rows // block_rows,),
        in_specs=[cur, cur, cur, pl.BlockSpec(bias.shape, lambda i: (0, 0, 0))],
        out_specs=cur,
        out_shape=jax.ShapeDtypeStruct((rows, MIX_WIDTH), BF16),
        scratch_shapes=[pltpu.VMEM((PAST_BAND + block_rows, MIX_WIDTH), BF16),
                        pltpu.VMEM((PAST_BAND + block_rows, 2 * MIX_WIDTH), BF16),
                        pltpu.VMEM((BAND_UNROLL * n_pairs, 2 * CHUNK, band), F32),
                        pltpu.VMEM((BAND_UNROLL * n_pairs, 2 * CHUNK, band), BF16)],
        name="band_attn_prompt",
        compiler_params=_params("arbitrary"),
    )(q, k, v, bias)


def _band_sample_kernel(q_ref, k_ref, v_ref, kct_ref, vct_ref, bias_ref, o_ref, s_scr, p_scr,
                        *, cache_len):
    n_pairs = H_A // 2
    s_len = q_ref.shape[0]
    for hp in range(n_pairs):
        lanes = slice(LANES * hp, LANES * (hp + 1))
        q2 = _stack_pair(q_ref[:, lanes] * (HD_A ** -0.5))
        kct = kct_ref[2 * hp:2 * hp + 2].reshape(LANES, cache_len).astype(BF16)
        s_scr[hp, :, 0:cache_len] = jnp.dot(q2, kct, preferred_element_type=F32)
        s_scr[hp, :, cache_len:] = lax.dot_general(q2, k_ref[:, lanes], CONTRACT_LAST,
                                                   preferred_element_type=F32)
    for hp in range(n_pairs):
        s = s_scr[hp] + bias_ref[hp]
        p_scr[hp] = jnp.exp(s - jnp.max(s, axis=-1, keepdims=True)).astype(BF16)
    ones_old = jnp.ones((LANES, cache_len), BF16)
    ones_new = jnp.ones((s_len, LANES), BF16)
    for hp in range(n_pairs):
        lanes = slice(LANES * hp, LANES * (hp + 1))
        vct = vct_ref[2 * hp:2 * hp + 2].reshape(LANES, cache_len).astype(BF16)
        r = (lax.dot_general(p_scr[hp, :, 0:cache_len], jnp.concatenate([vct, ones_old], axis=0),
                             CONTRACT_LAST, preferred_element_type=F32)
             + jnp.dot(p_scr[hp, :, cache_len:], jnp.concatenate([v_ref[:, lanes], ones_new], axis=1),
                       preferred_element_type=F32))
        o_ref[:, lanes] = _unstack_pair(r[:, :LANES] / r[:, LANES:]).astype(o_ref.dtype)


def band_attn_sample(q, k, v, k_cache_t, v_cache_t, layer, bias):
    b, s_len, _ = q.shape
    cache_len = k_cache_t.shape[-1]
    new = pl.BlockSpec((None, s_len, MIX_WIDTH), lambda i: (i, 0, 0))
    old = pl.BlockSpec((None, None, H_A, HD_A, cache_len), lambda i: (layer, i, 0, 0, 0))
    return pl.pallas_call(
        functools.partial(_band_sample_kernel, cache_len=cache_len),
        grid=(b,),
        in_specs=[new, new, new, old, old, pl.BlockSpec(bias.shape, lambda i: (0, 0, 0))],
        out_specs=new,
        out_shape=jax.ShapeDtypeStruct((b, s_len, MIX_WIDTH), BF16),
        scratch_shapes=[pltpu.VMEM((H_A // 2, 2 * s_len, cache_len + s_len), F32),
                        pltpu.VMEM((H_A // 2, 2 * s_len, cache_len + s_len), BF16)],
        name="band_attn_sample",
        compiler_params=_params("arbitrary"),
    )(q, k, v, k_cache_t, v_cache_t, bias)


def _hgrn2_constants(t):
    halves = []
    h = t // 2
    while h >= 1:
        halves.append(h)
        h //= 2
    n_lvl = len(halves)
    w = np.zeros(((n_lvl + 2) * t, t), np.float32)
    masks = np.zeros((n_lvl + 1, t, t), np.float32)
    for row in range(t):
        w[row, :row + 1] = 1.0
        w[t + row, row + 1:] = 1.0
    for li, h in enumerate(halves):
        base = (2 + li) * t
        for row in range(t):
            r = (row // (2 * h)) * 2 * h + h - 1
            if row > r:
                w[base + row, r + 1:row + 1] = 1.0
            else:
                w[base + row, row + 1:r + 1] = 1.0
        for tq in range(t):
            for sk in range(t):
                if tq // (2 * h) == sk // (2 * h):
                    r = (tq // (2 * h)) * 2 * h + h - 1
                    if tq > r and sk <= r:
                        masks[li, tq, sk] = 1.0
    masks[n_lvl] = np.eye(t, dtype=np.float32)
    return jnp.asarray(np.tile(w, (1, 3)), BF16), jnp.asarray(masks, F32), n_lvl


def _hgrn2_kernel(a_ref, b_ref, c_ref, s0_ref, lbc_ref, g_ref, w_ref, m_ref,
                  o_ref, sout_ref,
                  st_ref, lf_scr, kk_scr, sums_scr, qs_scr, ks_scr, attn_scr, u_scr, oi_scr,
                  *, t, n_lvl, n_blk):
    ci = pl.program_id(1)
    inter = n_lvl + 1

    @pl.when(ci == 0)
    def _():
        for h in range(H_B):
            st_ref[h] = s0_ref[h].T

    def blk(g):
        return slice(g * t, (g + 1) * t)

    def head(h):
        return slice(LANES * h, LANES * (h + 1))

    z = b_ref[...]
    log_lb = lbc_ref[0:1, :]
    u = lbc_ref[1:2, :] + jnp.minimum(z, 0.0) - jnp.log(1.0 + jnp.exp(-jnp.abs(z)))
    log_f = jnp.maximum(log_lb, u) + jnp.log(1.0 + jnp.exp(-jnp.abs(log_lb - u)))
    kk_scr[...] = lbc_ref[2:3, :] / (1.0 + jnp.exp(z))
    log2_f = log_f * LOG2_E
    hi = log2_f.astype(BF16)
    r1 = log2_f - hi.astype(F32)
    mid = r1.astype(BF16)
    lo = (r1 - mid.astype(F32)).astype(BF16)
    for g in range(n_blk):
        lf_scr[g, 0:t] = hi[blk(g)]
        lf_scr[g, t:2 * t] = mid[blk(g)]
        lf_scr[g, 2 * t:3 * t] = lo[blk(g)]

    for g in range(n_blk):
        sums_scr[g] = jnp.dot(w_ref[...], lf_scr[g], preferred_element_type=F32)

    for g in range(n_blk):
        q = a_ref[blk(g), :]
        kk = kk_scr[blk(g), :].astype(BF16)
        for li in range(n_lvl):
            e = jnp.exp2(sums_scr[g, (2 + li) * t:(3 + li) * t]).astype(BF16)
            qs_scr[g, li] = q * e
            ks_scr[g, li] = kk * e
        qs_scr[g, n_lvl] = q
        ks_scr[g, n_lvl] = kk
        qs_scr[g, inter] = q * jnp.exp2(sums_scr[g, 0:t]).astype(BF16)
        ks_scr[g, inter] = kk * jnp.exp2(sums_scr[g, t:2 * t]).astype(BF16)

    in_level = [m_ref[li] != 0.0 for li in range(n_lvl + 1)]
    for g in range(n_blk):
        for h in range(H_B):
            acc = jnp.zeros((t, t), F32)
            for li in range(n_lvl + 1):
                part = lax.dot_general(
                    qs_scr[g, li, :, head(h)], ks_scr[g, li, :, head(h)],
                    CONTRACT_LAST, preferred_element_type=F32)
                acc = jnp.where(in_level[li], part, acc)
            attn_scr[g, h] = acc.astype(BF16)

    for g in range(n_blk):
        for h in range(H_B):
            iv = c_ref[blk(g), head(h)]
            oi_scr[blk(g), head(h)] = jnp.dot(attn_scr[g, h], iv, preferred_element_type=F32)
            u_scr[g, h] = jnp.dot(iv.astype(F32).T.astype(BF16), ks_scr[g, inter, :, head(h)],
                                  preferred_element_type=F32)

    for g in range(n_blk):
        e_last = jnp.exp2(sums_scr[g, t - 1:t, :])
        for h in range(H_B):
            st = st_ref[h]
            o = oi_scr[blk(g), head(h)] + lax.dot_general(
                qs_scr[g, inter, :, head(h)], st.astype(BF16), CONTRACT_LAST,
                preferred_element_type=F32)
            st_ref[h] = e_last[:, head(h)] * st + u_scr[g, h]
            ms = jnp.mean(o * o, axis=-1, keepdims=True)
            o_ref[blk(g), head(h)] = (o * lax.rsqrt(ms + EPS) * g_ref[:, head(h)]).astype(o_ref.dtype)

    @pl.when(ci == pl.num_programs(1) - 1)
    def _():
        for h in range(H_B):
            sout_ref[h] = st_ref[h].T


def hgrn2(a, b, c, s0, layer, lbc, g, t, n_blk):
    bsz, rows, _ = a.shape
    w, masks, n_lvl = _hgrn2_constants(t)
    step_rows = n_blk * t
    tok = pl.BlockSpec((None, step_rows, MIX_WIDTH), lambda bi, ci: (bi, ci, 0))
    state = pl.BlockSpec((None, H_B, DK_B, DK_B), lambda bi, ci: (bi, 0, 0, 0))
    state_in = pl.BlockSpec((None, None, H_B, DK_B, DK_B), lambda bi, ci: (layer, bi, 0, 0, 0))
    scratch = [pltpu.VMEM((H_B, DK_B, DK_B), F32),
               pltpu.VMEM((n_blk, 3 * t, MIX_WIDTH), BF16),
               pltpu.VMEM((step_rows, MIX_WIDTH), F32),
               pltpu.VMEM((n_blk, (n_lvl + 2) * t, MIX_WIDTH), F32),
               pltpu.VMEM((n_blk, n_lvl + 2, t, MIX_WIDTH), BF16),
               pltpu.VMEM((n_blk, n_lvl + 2, t, MIX_WIDTH), BF16),
               pltpu.VMEM((n_blk, H_B, t, t), BF16),
               pltpu.VMEM((n_blk, H_B, DK_B, DK_B), F32),
               pltpu.VMEM((step_rows, MIX_WIDTH), F32)]
    return pl.pallas_call(
        functools.partial(_hgrn2_kernel, t=t, n_lvl=n_lvl, n_blk=n_blk),
        grid=(bsz, rows // step_rows),
        in_specs=[tok, tok, tok, state_in,
                  pl.BlockSpec(lbc.shape, lambda bi, ci: (0, 0)),
                  pl.BlockSpec((1, MIX_WIDTH), lambda bi, ci: (0, 0)),
                  pl.BlockSpec(w.shape, lambda bi, ci: (0, 0)),
                  pl.BlockSpec(masks.shape, lambda bi, ci: (0, 0, 0))],
        out_specs=[tok, state],
        out_shape=[jax.ShapeDtypeStruct((bsz, rows, MIX_WIDTH), BF16),
                   jax.ShapeDtypeStruct((bsz, H_B, DK_B, DK_B), F32)],
        scratch_shapes=scratch,
        name=f"hgrn2_t{t}",
        compiler_params=_params("arbitrary", "arbitrary"),
    )(a, b, c, s0, lbc, g.reshape(1, MIX_WIDTH), w, masks)


def _post_body(x_ref, mix_ref, xq_ref, gate_ref, mkt_ref, mvt_ref, w_ref, s_scr, p_scr, cross_scr,
               n_seg):
    seg_rows = x_ref.shape[0] // n_seg
    n_pairs = H_X // 2
    for seg in range(n_seg):
        rows = slice(seg * seg_rows, (seg + 1) * seg_rows)
        for hp in range(n_pairs):
            lanes = slice(LANES * hp, LANES * (hp + 1))
            q2 = _stack_pair(xq_ref[rows, lanes] * (HD_X ** -0.5))
            s_scr[seg * n_pairs + hp] = jnp.dot(q2, mkt_ref[seg, lanes, :].astype(BF16),
                                                preferred_element_type=F32)
    for i in range(n_seg * n_pairs):
        s = s_scr[i]
        p_scr[i] = jnp.exp(s - jnp.max(s, axis=-1, keepdims=True)).astype(BF16)
    ones = jnp.ones((LANES, N_MEM), BF16)
    for seg in range(n_seg):
        rows = slice(seg * seg_rows, (seg + 1) * seg_rows)
        for hp in range(n_pairs):
            lanes = slice(LANES * hp, LANES * (hp + 1))
            cross_scr[rows, lanes] = _pair_values_t(
                p_scr[seg * n_pairs + hp],
                jnp.concatenate([mvt_ref[seg, lanes, :].astype(BF16), ones], axis=0))

    gate = gate_ref[...]
    sg = gate / (1.0 + jnp.exp(-gate))
    y_mix = (mix_ref[...].astype(F32) * sg[:, 0:MIX_WIDTH]).astype(BF16)
    y_cross = (cross_scr[...] * sg[:, MIX_WIDTH:]).astype(BF16)
    return (x_ref[...] + jnp.dot(y_mix, w_ref[0:MIX_WIDTH, :], preferred_element_type=F32)
            + jnp.dot(y_cross, w_ref[MIX_WIDTH:, :], preferred_element_type=F32))


def _post_final_kernel(x_ref, mix_ref, xq_ref, gate_ref, mkt_ref, mvt_ref, w_ref, fg_ref, o_ref,
                       s_scr, p_scr, cross_scr, *, n_seg):
    acc = _post_body(x_ref, mix_ref, xq_ref, gate_ref, mkt_ref, mvt_ref, w_ref,
                     s_scr, p_scr, cross_scr, n_seg)
    ms = jnp.mean(acc * acc, axis=-1, keepdims=True)
    o_ref[...] = acc * lax.rsqrt(ms + EPS) * fg_ref[...]


def _post_pre_kernel(x_ref, mix_ref, xq_ref, gate_ref, mkt_ref, mvt_ref, wo_ref, g_ref, wi_ref,
                     xo_ref, *rest, n_seg, splits, tails):
    n_out = len(splits) + len(tails)
    s_scr, p_scr, cross_scr = rest[n_out:]
    acc = _post_body(x_ref, mix_ref, xq_ref, gate_ref, mkt_ref, mvt_ref, wo_ref,
                     s_scr, p_scr, cross_scr, n_seg)
    xo_ref[...] = acc
    _norm_proj_body(acc, g_ref, wi_ref, rest[:n_out], splits, tails, False)


def _post_specs(tm, n_seg, mem_layer, w_layer):
    def tok(n):
        return pl.BlockSpec((tm, n), lambda i: (i, 0))

    mem = pl.BlockSpec((None, n_seg, X_WIDTH, N_MEM), lambda i: (mem_layer, 0, 0, 0))
    w_out = pl.BlockSpec((None, D_INNER, D_MODEL), lambda i: (w_layer, 0, 0),
                         pipeline_mode=pl.Buffered(1))
    in_specs = [tok(D_MODEL), tok(MIX_WIDTH), tok(X_WIDTH), tok(D_INNER), mem, mem, w_out]
    seg_rows = tm // n_seg
    n_items = n_seg * (H_X // 2)
    scratch = [pltpu.VMEM((n_items, 2 * seg_rows, N_MEM), F32),
               pltpu.VMEM((n_items, 2 * seg_rows, N_MEM), BF16),
               pltpu.VMEM((tm, X_WIDTH), F32)]
    return in_specs, scratch, tok(D_MODEL)


def post_final(x, mix, xq, gate, mkt, mvt, mem_layer, w_all, w_layer, final_g, tm, n_seg):
    rows = x.shape[0]
    in_specs, scratch, x_spec = _post_specs(tm, n_seg, mem_layer, w_layer)
    return pl.pallas_call(
        functools.partial(_post_final_kernel, n_seg=n_seg),
        grid=(rows // tm,),
        in_specs=in_specs + [pl.BlockSpec((1, D_MODEL), lambda i: (0, 0))],
        out_specs=x_spec,
        out_shape=jax.ShapeDtypeStruct((rows, D_MODEL), F32),
        scratch_shapes=scratch,
        name=f"post_final_{rows}",
        compiler_params=_params("arbitrary"),
    )(x, mix, xq, gate, mkt, mvt, w_all, final_g.reshape(1, D_MODEL))


def post_pre(x, mix, xq, gate, mkt, mvt, mem_layer, w_out_all, layer, g_next, w_in_all,
             dtypes, tails, tm, n_seg):
    rows = x.shape[0]
    in_specs, scratch, x_spec = _post_specs(tm, n_seg, mem_layer, layer)
    in_specs += [pl.BlockSpec((1, D_MODEL), lambda i: (0, 0)),
                 pl.BlockSpec((None, D_MODEL, w_in_all.shape[2]), lambda i: (layer + 1, 0, 0),
                              pipeline_mode=pl.Buffered(1))]
    proj_specs, proj_shapes = _proj_out_specs(rows, tm, PROJ_SPLITS, dtypes, tails)
    return pl.pallas_call(
        functools.partial(_post_pre_kernel, n_seg=n_seg, splits=PROJ_SPLITS, tails=tuple(tails)),
        grid=(rows // tm,),
        in_specs=in_specs,
        out_specs=[x_spec] + proj_specs,
        out_shape=[jax.ShapeDtypeStruct((rows, D_MODEL), F32)] + proj_shapes,
        scratch_shapes=scratch,
        name=f"post_pre_{rows}",
        compiler_params=_params("arbitrary"),
    )(x, mix, xq, gate, mkt, mvt, w_out_all, g_next.reshape(1, D_MODEL), w_in_all)


def _rel_bias(table):
    band = PAST_BAND + CHUNK
    n_diag = band + CHUNK - 1
    offs = np.arange(n_diag) - (CHUNK - 1)
    idx = np.clip(PAST_BAND - offs, -REL_CLIP, REL_CLIP) + REL_CLIP
    diag = jnp.pad(table[:, idx].astype(F32), ((0, 0), (0, 1)))
    skew = jnp.tile(diag, (1, CHUNK))[:, :CHUNK * n_diag].reshape(H_A, CHUNK, n_diag)
    return skew[:, :, CHUNK - 1:CHUNK - 1 + band]


def _per_head_transposed(cache):
    return jnp.moveaxis(cache, -3, -1)


def kernel(x_prompt, x_sample, cache_a_k, cache_a_v, state_b, cache_mem_k, cache_mem_v, mem_prompt,
           ln_g, w_in, w_out, rel_bias_table, lower_bounds, hgrn_norm_g, mem_norm_g, w_mem_kv, final_g):
    bp, seq, _ = x_prompt.shape
    bs, dec_seq, _ = x_sample.shape
    assert bp == 1
    cache_len = cache_a_k.shape[2]
    n_s = bs * dec_seq
    keep = min(PAST_BAND, seq)

    w_in_b = w_in.astype(BF16)
    w_out_b = w_out.astype(BF16)
    w_mem_b = w_mem_kv.astype(BF16)

    lb_all = jnp.cumsum(jax.nn.softmax(lower_bounds.astype(F32), axis=0), axis=0)
    lb_all = lb_all - lb_all[:1]

    cache_a_kt = _per_head_transposed(cache_a_k)
    cache_a_vt = _per_head_transposed(cache_a_v)
    cache_mem_kt = _per_head_transposed(cache_mem_k).reshape(DEPTH, bs, X_WIDTH, N_MEM)
    cache_mem_vt = _per_head_transposed(cache_mem_v).reshape(DEPTH, bs, X_WIDTH, N_MEM)
    zero_state = jnp.zeros((1, 1, H_B, DK_B, DK_B), F32)

    def per_batch(u):
        return u.reshape(bs, dec_seq, u.shape[-1])

    def layer_io(l):
        attn_layer = l % 2 == 0
        return (BF16, BF16 if attn_layer else F32, BF16, BF16, F32), (1, 2) if attn_layer else ()

    xp = x_prompt.reshape(seq, D_MODEL)
    xs = x_sample.reshape(n_s, D_MODEL)
    dtypes, tails = layer_io(0)
    outs_p = norm_proj(xp, ln_g[0], w_in_b, 0, PROJ_SPLITS, dtypes, PAST_BAND, tails)
    outs_s = norm_proj(xs, ln_g[0], w_in_b, 0, PROJ_SPLITS, dtypes, n_s, tails)
    ak_p, av_p, sb_p, mk_pl, mv_pl, ak_s, av_s, sb_s = [], [], [], [], [], [], [], []
    for l in range(DEPTH):
        j = l // 2
        mkt_p, mvt_p = norm_proj(mem_prompt.reshape(N_MEM, D_MODEL), mem_norm_g[l], w_mem_b, l,
                                 (X_WIDTH, X_WIDTH), (F32, F32), N_MEM, transpose=True)
        a_p, b_p, c_p, xq_p, g_p = outs_p[:5]
        a_s, b_s, c_s, xq_s, g_s = outs_s[:5]

        if l % 2 == 0:
            bias = _rel_bias(rel_bias_table[j])
            band = PAST_BAND + CHUNK
            o_p = band_attn_prompt(a_p, b_p, c_p, bias.reshape(H_A // 2, 2 * CHUNK, band), PAST_BAND)
            o_s = band_attn_sample(per_batch(a_s), per_batch(b_s), per_batch(c_s),
                                   cache_a_kt, cache_a_vt, j,
                                   bias[:, :dec_seq, :cache_len + dec_seq].reshape(
                                       H_A // 2, 2 * dec_seq, cache_len + dec_seq))
            assert keep == PAST_BAND
            ak_p.append(outs_p[5].reshape(1, keep, H_A, HD_A))
            av_p.append(outs_p[6].reshape(1, keep, H_A, HD_A))
            ak_s.append(outs_s[5].reshape(bs, dec_seq, H_A, HD_A))
            av_s.append(outs_s[6].reshape(bs, dec_seq, H_A, HD_A))
        else:
            lb = lb_all[j]
            lbc = jnp.stack([jnp.log(lb), jnp.log1p(-lb), 1.0 - lb])
            o_p, s_p = hgrn2(a_p.reshape(1, seq, MIX_WIDTH), b_p.reshape(1, seq, MIX_WIDTH),
                             c_p.reshape(1, seq, MIX_WIDTH), zero_state, 0, lbc, hgrn_norm_g[j],
                             CHUNK, HGRN_BLOCKS_PER_STEP)
            o_s, s_s = hgrn2(per_batch(a_s), per_batch(b_s), per_batch(c_s),
                             state_b.astype(F32), j, lbc, hgrn_norm_g[j], dec_seq, 1)
            sb_p.append(s_p)
            sb_s.append(s_s)
        mem_p = (mkt_p.reshape(1, 1, X_WIDTH, N_MEM), mvt_p.reshape(1, 1, X_WIDTH, N_MEM), 0)
        mem_s = (cache_mem_kt, cache_mem_vt, l)
        o_p = o_p.reshape(seq, MIX_WIDTH)
        o_s = o_s.reshape(n_s, MIX_WIDTH)
        if l == DEPTH - 1:
            xp = post_final(xp, o_p, xq_p, g_p, *mem_p, w_out_b, l, final_g, PAST_BAND, 1)
            xs = post_final(xs, o_s, xq_s, g_s, *mem_s, w_out_b, l, final_g, n_s, bs)
        else:
            dtypes, tails = layer_io(l + 1)
            xp, *outs_p = post_pre(xp, o_p, xq_p, g_p, *mem_p, w_out_b, l, ln_g[l + 1], w_in_b,
                                   dtypes, tails, PAST_BAND, 1)
            xs, *outs_s = post_pre(xs, o_s, xq_s, g_s, *mem_s, w_out_b, l, ln_g[l + 1], w_in_b,
                                   dtypes, tails, n_s, bs)
        mk_pl.append(jnp.moveaxis(mkt_p.reshape(1, H_X, HD_X, N_MEM), -1, 1))
        mv_pl.append(jnp.moveaxis(mvt_p.reshape(1, H_X, HD_X, N_MEM), -1, 1))
    return (xp.reshape(1, seq, D_MODEL), xs.reshape(bs, dec_seq, D_MODEL),
            jnp.stack(ak_p), jnp.stack(av_p), jnp.stack(sb_p), jnp.stack(mk_pl), jnp.stack(mv_pl),
            jnp.stack(ak_s), jnp.stack(av_s), jnp.stack(sb_s))
```

```python
import functools

import numpy as np
import jax
import jax.numpy as jnp
from jax import lax
from jax.experimental import pallas as pl
from jax.experimental.pallas import tpu as pltpu

D_MODEL = 1024
DEPTH = 4
CHUNK = 64
N_PAST_CHUNKS = 8
PAST_BAND = N_PAST_CHUNKS * CHUNK
MIX_WIDTH = 768
X_WIDTH = 256
D_INNER = MIX_WIDTH + X_WIDTH
HD_A = 64
H_A = MIX_WIDTH // HD_A
REL_CLIP = 128
DK_B = 128
H_B = MIX_WIDTH // DK_B
H_X = 4
HD_X = 64
N_MEM = 256
EPS = 1e-6
NEG = -1e30
LOG2_E = 1.4426950408889634
F32 = jnp.float32
BF16 = jnp.bfloat16

LANES = 128
VMEM_LIMIT_BYTES = 56 * 1024 * 1024
PROJ_SPLITS = (MIX_WIDTH, MIX_WIDTH, MIX_WIDTH, X_WIDTH, D_INNER)
CONTRACT_LAST = (((1,), (1,)), ((), ()))
HGRN_BLOCKS_PER_STEP = 8
BAND_UNROLL = 4


def _params(*sem):
    return pltpu.CompilerParams(dimension_semantics=sem, vmem_limit_bytes=VMEM_LIMIT_BYTES)


def _norm_proj_body(x, g_ref, w_ref, out_refs, splits, tails, transpose):
    ms = jnp.mean(x * x, axis=-1, keepdims=True)
    xn = (x * lax.rsqrt(ms + EPS) * g_ref[...]).astype(BF16)
    tail_refs = out_refs[len(splits):]
    off = 0
    for idx, (o_ref, n) in enumerate(zip(out_refs, splits)):
        r = jnp.dot(xn, w_ref[:, off:off + n], preferred_element_type=F32)
        o_ref[...] = (r.T if transpose else r).astype(o_ref.dtype)
        if idx in tails:
            tail_refs[tails.index(idx)][...] = r
        off += n


def _norm_proj_kernel(x_ref, g_ref, w_ref, *out_refs, splits, tails, transpose):
    _norm_proj_body(x_ref[...], g_ref, w_ref, out_refs, splits, tails, transpose)


def _proj_out_specs(rows, tm, splits, dtypes, tails):
    out_specs = [pl.BlockSpec((tm, n), lambda i: (i, 0)) for n in splits]
    out_shape = [jax.ShapeDtypeStruct((rows, n), dt) for n, dt in zip(splits, dtypes)]
    out_specs += [pl.BlockSpec((tm, splits[idx]), lambda i: (0, 0)) for idx in tails]
    out_shape += [jax.ShapeDtypeStruct((tm, splits[idx]), F32) for idx in tails]
    return out_specs, out_shape


def norm_proj(x2d, g, w_all, layer, splits, dtypes, tm, tails=(), transpose=False):
    rows, d = x2d.shape
    n_total = w_all.shape[2]
    assert rows % tm == 0 and (not transpose or rows == tm)
    if transpose:
        out_specs = [pl.BlockSpec((n, tm), lambda i: (0, 0)) for n in splits]
        out_shape = [jax.ShapeDtypeStruct((n, rows), dt) for n, dt in zip(splits, dtypes)]
    else:
        out_specs, out_shape = _proj_out_specs(rows, tm, splits, dtypes, tails)
    return pl.pallas_call(
        functools.partial(_norm_proj_kernel, splits=splits, tails=tuple(tails), transpose=transpose),
        grid=(rows // tm,),
        in_specs=[
            pl.BlockSpec((tm, d), lambda i: (i, 0)),
            pl.BlockSpec((1, d), lambda i: (0, 0)),
            pl.BlockSpec((None, d, n_total), lambda i: (layer, 0, 0)),
        ],
        out_specs=out_specs,
        out_shape=out_shape,
        name=f"norm_proj_{rows}x{n_total}",
        compiler_params=_params("arbitrary"),
    )(x2d, g.reshape(1, d), w_all)


def _stack_pair(q_pair):
    first = lax.broadcasted_iota(jnp.int32, (1, LANES), 1) < HD_A
    keep0 = jnp.where(first, 1.0, 0.0).astype(BF16)
    keep1 = jnp.where(first, 0.0, 1.0).astype(BF16)
    return jnp.concatenate([q_pair * keep0, q_pair * keep1], axis=0)


def _unstack_pair(o2):
    m = o2.shape[0] // 2
    first = lax.broadcasted_iota(jnp.int32, (m, LANES), 1) < HD_A
    return jnp.where(first, o2[:m], o2[m:])


def _pair_values(p, v_ext):
    r = jnp.dot(p, v_ext, preferred_element_type=F32)
    return _unstack_pair(r[:, :LANES] / r[:, LANES:])


def _pair_values_t(p, vt_ext):
    r = lax.dot_general(p, vt_ext, CONTRACT_LAST, preferred_element_type=F32)
    return _unstack_pair(r[:, :LANES] / r[:, LANES:])


def _band_attn_kernel(q_ref, k_ref, v_ref, bias_ref, o_ref,
                      kwin, vext, s_scr, p_scr, *, n_chunks, unroll):
    step = pl.program_id(0)
    chunk, prev_rows = CHUNK, PAST_BAND
    cur_rows = n_chunks * chunk
    band = prev_rows + chunk
    n_pairs = H_A // 2
    assert cur_rows == prev_rows

    @pl.when(step == 0)
    def _():
        kwin[...] = jnp.zeros_like(kwin)
        vext[...] = jnp.zeros_like(vext)

    kwin[0:prev_rows, :] = kwin[prev_rows:prev_rows + cur_rows, :]
    kwin[prev_rows:prev_rows + cur_rows, :] = k_ref[...]
    ones = jnp.ones((prev_rows + cur_rows, LANES), BF16)
    for hp in range(n_pairs):
        lanes = slice(LANES * hp, LANES * (hp + 1))
        vcol = slice(2 * LANES * hp, 2 * LANES * hp + LANES)
        vext[0:prev_rows, vcol] = vext[prev_rows:prev_rows + cur_rows, vcol]
        vext[prev_rows:prev_rows + cur_rows, vcol] = v_ref[:, lanes]
        vext[:, 2 * LANES * hp + LANES:2 * LANES * (hp + 1)] = ones

    def chunks_body(ci, carry):
        starts = [pl.multiple_of((ci * unroll + u) * chunk, chunk) for u in range(unroll)]
        for u, r0 in enumerate(starts):
            for hp in range(n_pairs):
                lanes = slice(LANES * hp, LANES * (hp + 1))
                q2 = _stack_pair(q_ref[pl.ds(r0, chunk), lanes] * (HD_A ** -0.5))
                s_scr[u * n_pairs + hp] = lax.dot_general(
                    q2, kwin[pl.ds(r0, band), lanes], CONTRACT_LAST, preferred_element_type=F32)
        for u, r0 in enumerate(starts):
            key_row = lax.broadcasted_iota(jnp.int32, (1, band), 1) + r0 + (step - 1) * prev_rows
            neg = jnp.where(key_row < 0, NEG, 0.0)
            for hp in range(n_pairs):
                s = s_scr[u * n_pairs + hp] + bias_ref[hp] + neg
                m = jnp.max(s, axis=-1, keepdims=True)
                p_scr[u * n_pairs + hp] = jnp.exp(s - m).astype(BF16)
        for u, r0 in enumerate(starts):
            for hp in range(n_pairs):
                o_ref[pl.ds(r0, chunk), LANES * hp:LANES * (hp + 1)] = _pair_values(
                    p_scr[u * n_pairs + hp],
                    vext[pl.ds(r0, band), 2 * LANES * hp:2 * LANES * (hp + 1)]
                ).astype(o_ref.dtype)
        return carry

    lax.fori_loop(0, n_chunks // unroll, chunks_body, 0)


def band_attn_prompt(q, k, v, bias, block_rows):
    rows = q.shape[0]
    n_chunks = block_rows // CHUNK
    n_pairs = H_A // 2
    band = PAST_BAND + CHUNK
    assert block_rows == PAST_BAND
    cur = pl.BlockSpec((block_rows, MIX_WIDTH), lambda i: (i, 0))
    return pl.pallas_call(
        functools.partial(_band_attn_kernel, n_chunks=n_chunks, unroll=BAND_UNROLL),
        grid=(rows // block_rows,),
        in_specs=[cur, cur, cur, pl.BlockSpec(bias.shape, lambda i: (0, 0, 0))],
        out_specs=cur,
        out_shape=jax.ShapeDtypeStruct((rows, MIX_WIDTH), BF16),
        scratch_shapes=[pltpu.VMEM((PAST_BAND + block_rows, MIX_WIDTH), BF16),
                        pltpu.VMEM((PAST_BAND + block_rows, 2 * MIX_WIDTH), BF16),
                        pltpu.VMEM((BAND_UNROLL * n_pairs, 2 * CHUNK, band), F32),
                        pltpu.VMEM((BAND_UNROLL * n_pairs, 2 * CHUNK, band), BF16)],
        name="band_attn_prompt",
        compiler_params=_params("arbitrary"),
    )(q, k, v, bias)


def _band_sample_kernel(q_ref, k_ref, v_ref, kct_ref, vct_ref, bias_ref, o_ref, s_scr, p_scr,
                        *, cache_len):
    n_pairs = H_A // 2
    s_len = q_ref.shape[0]
    for hp in range(n_pairs):
        lanes = slice(LANES * hp, LANES * (hp + 1))
        q2 = _stack_pair(q_ref[:, lanes] * (HD_A ** -0.5))
        kct = kct_ref[2 * hp:2 * hp + 2].reshape(LANES, cache_len).astype(BF16)
        s_scr[hp, :, 0:cache_len] = jnp.dot(q2, kct, preferred_element_type=F32)
        s_scr[hp, :, cache_len:] = lax.dot_general(q2, k_ref[:, lanes], CONTRACT_LAST,
                                                   preferred_element_type=F32)
    for hp in range(n_pairs):
        s = s_scr[hp] + bias_ref[hp]
        p_scr[hp] = jnp.exp(s - jnp.max(s, axis=-1, keepdims=True)).astype(BF16)
    ones_old = jnp.ones((LANES, cache_len), BF16)
    ones_new = jnp.ones((s_len, LANES), BF16)
    for hp in range(n_pairs):
        lanes = slice(LANES * hp, LANES * (hp + 1))
        vct = vct_ref[2 * hp:2 * hp + 2].reshape(LANES, cache_len).astype(BF16)
        r = (lax.dot_general(p_scr[hp, :, 0:cache_len], jnp.concatenate([vct, ones_old], axis=0),
                             CONTRACT_LAST, preferred_element_type=F32)
             + jnp.dot(p_scr[hp, :, cache_len:], jnp.concatenate([v_ref[:, lanes], ones_new], axis=1),
                       preferred_element_type=F32))
        o_ref[:, lanes] = _unstack_pair(r[:, :LANES] / r[:, LANES:]).astype(o_ref.dtype)


def band_attn_sample(q, k, v, k_cache_t, v_cache_t, layer, bias):
    b, s_len, _ = q.shape
    cache_len = k_cache_t.shape[-1]
    new = pl.BlockSpec((None, s_len, MIX_WIDTH), lambda i: (i, 0, 0))
    old = pl.BlockSpec((None, None, H_A, HD_A, cache_len), lambda i: (layer, i, 0, 0, 0))
    return pl.pallas_call(
        functools.partial(_band_sample_kernel, cache_len=cache_len),
        grid=(b,),
        in_specs=[new, new, new, old, old, pl.BlockSpec(bias.shape, lambda i: (0, 0, 0))],
        out_specs=new,
        out_shape=jax.ShapeDtypeStruct((b, s_len, MIX_WIDTH), BF16),
        scratch_shapes=[pltpu.VMEM((H_A // 2, 2 * s_len, cache_len + s_len), F32),
                        pltpu.VMEM((H_A // 2, 2 * s_len, cache_len + s_len), BF16)],
        name="band_attn_sample",
        compiler_params=_params("arbitrary"),
    )(q, k, v, k_cache_t, v_cache_t, bias)


def _hgrn2_constants(t):
    halves = []
    h = t // 2
    while h >= 1:
        halves.append(h)
        h //= 2
    n_lvl = len(halves)
    w = np.zeros(((n_lvl + 2) * t, t), np.float32)
    masks = np.zeros((n_lvl + 1, t, t), np.float32)
    for row in range(t):
        w[row, :row + 1] = 1.0
        w[t + row, row + 1:] = 1.0
    for li, h in enumerate(halves):
        base = (2 + li) * t
        for row in range(t):
            r = (row // (2 * h)) * 2 * h + h - 1
            if row > r:
                w[base + row, r + 1:row + 1] = 1.0
            else:
                w[base + row, row + 1:r + 1] = 1.0
        for tq in range(t):
            for sk in range(t):
                if tq // (2 * h) == sk // (2 * h):
                    r = (tq // (2 * h)) * 2 * h + h - 1
                    if tq > r and sk <= r:
                        masks[li, tq, sk] = 1.0
    masks[n_lvl] = np.eye(t, dtype=np.float32)
    return jnp.asarray(np.tile(w, (1, 3)), BF16), jnp.asarray(masks, F32), n_lvl


def _hgrn2_kernel(a_ref, b_ref, c_ref, s0_ref, lbc_ref, g_ref, w_ref, m_ref,
                  o_ref, sout_ref,
                  st_ref, lf_scr, kk_scr, sums_scr, qs_scr, ks_scr, attn_scr, u_scr, oi_scr,
                  *, t, n_lvl, n_blk):
    ci = pl.program_id(1)
    inter = n_lvl + 1

    @pl.when(ci == 0)
    def _():
        for h in range(H_B):
            st_ref[h] = s0_ref[h].T

    def blk(g):
        return slice(g * t, (g + 1) * t)

    def head(h):
        return slice(LANES * h, LANES * (h + 1))

    z = b_ref[...]
    log_lb = lbc_ref[0:1, :]
    u = lbc_ref[1:2, :] + jnp.minimum(z, 0.0) - jnp.log(1.0 + jnp.exp(-jnp.abs(z)))
    log_f = jnp.maximum(log_lb, u) + jnp.log(1.0 + jnp.exp(-jnp.abs(log_lb - u)))
    kk_scr[...] = lbc_ref[2:3, :] / (1.0 + jnp.exp(z))
    log2_f = log_f * LOG2_E
    hi = log2_f.astype(BF16)
    r1 = log2_f - hi.astype(F32)
    mid = r1.astype(BF16)
    lo = (r1 - mid.astype(F32)).astype(BF16)
    for g in range(n_blk):
        lf_scr[g, 0:t] = hi[blk(g)]
        lf_scr[g, t:2 * t] = mid[blk(g)]
        lf_scr[g, 2 * t:3 * t] = lo[blk(g)]

    for g in range(n_blk):
        sums_scr[g] = jnp.dot(w_ref[...], lf_scr[g], preferred_element_type=F32)

    for g in range(n_blk):
        q = a_ref[blk(g), :].astype(F32)
        kk = kk_scr[blk(g), :]
        for li in range(n_lvl):
            e = jnp.exp2(sums_scr[g, (2 + li) * t:(3 + li) * t])
            qs_scr[g, li] = (q * e).astype(BF16)
            ks_scr[g, li] = (kk * e).astype(BF16)
        qs_scr[g, n_lvl] = a_ref[blk(g), :]
        ks_scr[g, n_lvl] = kk.astype(BF16)
        qs_scr[g, inter] = (q * jnp.exp2(sums_scr[g, 0:t])).astype(BF16)
        ks_scr[g, inter] = (kk * jnp.exp2(sums_scr[g, t:2 * t])).astype(BF16)

    in_level = [m_ref[li] != 0.0 for li in range(n_lvl + 1)]
    for g in range(n_blk):
        for h in range(H_B):
            acc = jnp.zeros((t, t), F32)
            for li in range(n_lvl + 1):
                part = lax.dot_general(
                    qs_scr[g, li, :, head(h)], ks_scr[g, li, :, head(h)],
                    CONTRACT_LAST, preferred_element_type=F32)
                acc = jnp.where(in_level[li], part, acc)
            attn_scr[g, h] = acc.astype(BF16)

    for g in range(n_blk):
        for h in range(H_B):
            iv = c_ref[blk(g), head(h)]
            oi_scr[blk(g), head(h)] = jnp.dot(attn_scr[g, h], iv, preferred_element_type=F32)
            u_scr[g, h] = jnp.dot(iv.astype(F32).T.astype(BF16), ks_scr[g, inter, :, head(h)],
                                  preferred_element_type=F32)

    for g in range(n_blk):
        e_last = jnp.exp2(sums_scr[g, t - 1:t, :])
        for h in range(H_B):
            st = st_ref[h]
            o = oi_scr[blk(g), head(h)] + lax.dot_general(
                qs_scr[g, inter, :, head(h)], st.astype(BF16), CONTRACT_LAST,
                preferred_element_type=F32)
            st_ref[h] = e_last[:, head(h)] * st + u_scr[g, h]
            ms = jnp.mean(o * o, axis=-1, keepdims=True)
            o_ref[blk(g), head(h)] = (o * lax.rsqrt(ms + EPS) * g_ref[:, head(h)]).astype(o_ref.dtype)

    @pl.when(ci == pl.num_programs(1) - 1)
    def _():
        for h in range(H_B):
            sout_ref[h] = st_ref[h].T


def hgrn2(a, b, c, s0, layer, lbc, g, t, n_blk):
    bsz, rows, _ = a.shape
    w, masks, n_lvl = _hgrn2_constants(t)
    step_rows = n_blk * t
    tok = pl.BlockSpec((None, step_rows, MIX_WIDTH), lambda bi, ci: (bi, ci, 0))
    state = pl.BlockSpec((None, H_B, DK_B, DK_B), lambda bi, ci: (bi, 0, 0, 0))
    state_in = pl.BlockSpec((None, None, H_B, DK_B, DK_B), lambda bi, ci: (layer, bi, 0, 0, 0))
    scratch = [pltpu.VMEM((H_B, DK_B, DK_B), F32),
               pltpu.VMEM((n_blk, 3 * t, MIX_WIDTH), BF16),
               pltpu.VMEM((step_rows, MIX_WIDTH), F32),
               pltpu.VMEM((n_blk, (n_lvl + 2) * t, MIX_WIDTH), F32),
               pltpu.VMEM((n_blk, n_lvl + 2, t, MIX_WIDTH), BF16),
               pltpu.VMEM((n_blk, n_lvl + 2, t, MIX_WIDTH), BF16),
               pltpu.VMEM((n_blk, H_B, t, t), BF16),
               pltpu.VMEM((n_blk, H_B, DK_B, DK_B), F32),
               pltpu.VMEM((step_rows, MIX_WIDTH), F32)]
    return pl.pallas_call(
        functools.partial(_hgrn2_kernel, t=t, n_lvl=n_lvl, n_blk=n_blk),
        grid=(bsz, rows // step_rows),
        in_specs=[tok, tok, tok, state_in,
                  pl.BlockSpec(lbc.shape, lambda bi, ci: (0, 0)),
                  pl.BlockSpec((1, MIX_WIDTH), lambda bi, ci: (0, 0)),
                  pl.BlockSpec(w.shape, lambda bi, ci: (0, 0)),
                  pl.BlockSpec(masks.shape, lambda bi, ci: (0, 0, 0))],
        out_specs=[tok, state],
        out_shape=[jax.ShapeDtypeStruct((bsz, rows, MIX_WIDTH), BF16),
                   jax.ShapeDtypeStruct((bsz, H_B, DK_B, DK_B), F32)],
        scratch_shapes=scratch,
        name=f"hgrn2_t{t}",
        compiler_params=_params("arbitrary", "arbitrary"),
    )(a, b, c, s0, lbc, g.reshape(1, MIX_WIDTH), w, masks)


def _post_body(x_ref, mix_ref, xq_ref, gate_ref, mkt_ref, mvt_ref, w_ref, s_scr, p_scr, cross_scr,
               n_seg):
    seg_rows = x_ref.shape[0] // n_seg
    n_pairs = H_X // 2
    for seg in range(n_seg):
        rows = slice(seg * seg_rows, (seg + 1) * seg_rows)
        for hp in range(n_pairs):
            lanes = slice(LANES * hp, LANES * (hp + 1))
            q2 = _stack_pair(xq_ref[rows, lanes] * (HD_X ** -0.5))
            s_scr[seg * n_pairs + hp] = jnp.dot(q2, mkt_ref[seg, lanes, :].astype(BF16),
                                                preferred_element_type=F32)
    for i in range(n_seg * n_pairs):
        s = s_scr[i]
        p_scr[i] = jnp.exp(s - jnp.max(s, axis=-1, keepdims=True)).astype(BF16)
    ones = jnp.ones((LANES, N_MEM), BF16)
    for seg in range(n_seg):
        rows = slice(seg * seg_rows, (seg + 1) * seg_rows)
        for hp in range(n_pairs):
            lanes = slice(LANES * hp, LANES * (hp + 1))
            cross_scr[rows, lanes] = _pair_values_t(
                p_scr[seg * n_pairs + hp],
                jnp.concatenate([mvt_ref[seg, lanes, :].astype(BF16), ones], axis=0))

    gate = gate_ref[...]
    sg = gate / (1.0 + jnp.exp(-gate))
    y_mix = (mix_ref[...].astype(F32) * sg[:, 0:MIX_WIDTH]).astype(BF16)
    y_cross = (cross_scr[...] * sg[:, MIX_WIDTH:]).astype(BF16)
    return (x_ref[...] + jnp.dot(y_mix, w_ref[0:MIX_WIDTH, :], preferred_element_type=F32)
            + jnp.dot(y_cross, w_ref[MIX_WIDTH:, :], preferred_element_type=F32))


def _post_final_kernel(x_ref, mix_ref, xq_ref, gate_ref, mkt_ref, mvt_ref, w_ref, fg_ref, o_ref,
                       s_scr, p_scr, cross_scr, *, n_seg):
    acc = _post_body(x_ref, mix_ref, xq_ref, gate_ref, mkt_ref, mvt_ref, w_ref,
                     s_scr, p_scr, cross_scr, n_seg)
    ms = jnp.mean(acc * acc, axis=-1, keepdims=True)
    o_ref[...] = acc * lax.rsqrt(ms + EPS) * fg_ref[...]


def _post_pre_kernel(x_ref, mix_ref, xq_ref, gate_ref, mkt_ref, mvt_ref, wo_ref, g_ref, wi_ref,
                     xo_ref, *rest, n_seg, splits, tails):
    n_out = len(splits) + len(tails)
    s_scr, p_scr, cross_scr = rest[n_out:]
    acc = _post_body(x_ref, mix_ref, xq_ref, gate_ref, mkt_ref, mvt_ref, wo_ref,
                     s_scr, p_scr, cross_scr, n_seg)
    xo_ref[...] = acc
    _norm_proj_body(acc, g_ref, wi_ref, rest[:n_out], splits, tails, False)


def _post_specs(tm, n_seg, mem_layer, w_layer):
    def tok(n):
        return pl.BlockSpec((tm, n), lambda i: (i, 0))

    mem = pl.BlockSpec((None, n_seg, X_WIDTH, N_MEM), lambda i: (mem_layer, 0, 0, 0))
    w_out = pl.BlockSpec((None, D_INNER, D_MODEL), lambda i: (w_layer, 0, 0),
                         pipeline_mode=pl.Buffered(1))
    in_specs = [tok(D_MODEL), tok(MIX_WIDTH), tok(X_WIDTH), tok(D_INNER), mem, mem, w_out]
    seg_rows = tm // n_seg
    n_items = n_seg * (H_X // 2)
    scratch = [pltpu.VMEM((n_items, 2 * seg_rows, N_MEM), F32),
               pltpu.VMEM((n_items, 2 * seg_rows, N_MEM), BF16),
               pltpu.VMEM((tm, X_WIDTH), F32)]
    return in_specs, scratch, tok(D_MODEL)


def post_final(x, mix, xq, gate, mkt, mvt, mem_layer, w_all, w_layer, final_g, tm, n_seg):
    rows = x.shape[0]
    in_specs, scratch, x_spec = _post_specs(tm, n_seg, mem_layer, w_layer)
    return pl.pallas_call(
        functools.partial(_post_final_kernel, n_seg=n_seg),
        grid=(rows // tm,),
        in_specs=in_specs + [pl.BlockSpec((1, D_MODEL), lambda i: (0, 0))],
        out_specs=x_spec,
        out_shape=jax.ShapeDtypeStruct((rows, D_MODEL), F32),
        scratch_shapes=scratch,
        name=f"post_final_{rows}",
        compiler_params=_params("arbitrary"),
    )(x, mix, xq, gate, mkt, mvt, w_all, final_g.reshape(1, D_MODEL))


def post_pre(x, mix, xq, gate, mkt, mvt, mem_layer, w_out_all, layer, g_next, w_in_all,
             dtypes, tails, tm, n_seg):
    rows = x.shape[0]
    in_specs, scratch, x_spec = _post_specs(tm, n_seg, mem_layer, layer)
    in_specs += [pl.BlockSpec((1, D_MODEL), lambda i: (0, 0)),
                 pl.BlockSpec((None, D_MODEL, w_in_all.shape[2]), lambda i: (layer + 1, 0, 0),
                              pipeline_mode=pl.Buffered(1))]
    proj_specs, proj_shapes = _proj_out_specs(rows, tm, PROJ_SPLITS, dtypes, tails)
    return pl.pallas_call(
        functools.partial(_post_pre_kernel, n_seg=n_seg, splits=PROJ_SPLITS, tails=tuple(tails)),
        grid=(rows // tm,),
        in_specs=in_specs,
        out_specs=[x_spec] + proj_specs,
        out_shape=[jax.ShapeDtypeStruct((rows, D_MODEL), F32)] + proj_shapes,
        scratch_shapes=scratch,
        name=f"post_pre_{rows}",
        compiler_params=_params("arbitrary"),
    )(x, mix, xq, gate, mkt, mvt, w_out_all, g_next.reshape(1, D_MODEL), w_in_all)


def _rel_bias(table):
    band = PAST_BAND + CHUNK
    n_diag = band + CHUNK - 1
    offs = np.arange(n_diag) - (CHUNK - 1)
    idx = np.clip(PAST_BAND - offs, -REL_CLIP, REL_CLIP) + REL_CLIP
    diag = jnp.pad(table[:, idx].astype(F32), ((0, 0), (0, 1)))
    skew = jnp.tile(diag, (1, CHUNK))[:, :CHUNK * n_diag].reshape(H_A, CHUNK, n_diag)
    return skew[:, :, CHUNK - 1:CHUNK - 1 + band]


def _per_head_transposed(cache):
    return jnp.moveaxis(cache, -3, -1)


def kernel(x_prompt, x_sample, cache_a_k, cache_a_v, state_b, cache_mem_k, cache_mem_v, mem_prompt,
           ln_g, w_in, w_out, rel_bias_table, lower_bounds, hgrn_norm_g, mem_norm_g, w_mem_kv, final_g):
    bp, seq, _ = x_prompt.shape
    bs, dec_seq, _ = x_sample.shape
    assert bp == 1
    cache_len = cache_a_k.shape[2]
    n_s = bs * dec_seq
    keep = min(PAST_BAND, seq)

    w_in_b = w_in.astype(BF16)
    w_out_b = w_out.astype(BF16)
    w_mem_b = w_mem_kv.astype(BF16)

    lb_all = jnp.cumsum(jax.nn.softmax(lower_bounds.astype(F32), axis=0), axis=0)
    lb_all = lb_all - lb_all[:1]

    cache_a_kt = _per_head_transposed(cache_a_k)
    cache_a_vt = _per_head_transposed(cache_a_v)
    cache_mem_kt = _per_head_transposed(cache_mem_k).reshape(DEPTH, bs, X_WIDTH, N_MEM)
    cache_mem_vt = _per_head_transposed(cache_mem_v).reshape(DEPTH, bs, X_WIDTH, N_MEM)
    zero_state = jnp.zeros((1, 1, H_B, DK_B, DK_B), F32)

    def per_batch(u):
        return u.reshape(bs, dec_seq, u.shape[-1])

    def layer_io(l):
        attn_layer = l % 2 == 0
        return (BF16, BF16 if attn_layer else F32, BF16, BF16, F32), (1, 2) if attn_layer else ()

    xp = x_prompt.reshape(seq, D_MODEL)
    xs = x_sample.reshape(n_s, D_MODEL)
    dtypes, tails = layer_io(0)
    outs_p = norm_proj(xp, ln_g[0], w_in_b, 0, PROJ_SPLITS, dtypes, PAST_BAND, tails)
    outs_s = norm_proj(xs, ln_g[0], w_in_b, 0, PROJ_SPLITS, dtypes, n_s, tails)
    ak_p, av_p, sb_p, mk_pl, mv_pl, ak_s, av_s, sb_s = [], [], [], [], [], [], [], []
    for l in range(DEPTH):
        j = l // 2
        mkt_p, mvt_p = norm_proj(mem_prompt.reshape(N_MEM, D_MODEL), mem_norm_g[l], w_mem_b, l,
                                 (X_WIDTH, X_WIDTH), (F32, F32), N_MEM, transpose=True)
        a_p, b_p, c_p, xq_p, g_p = outs_p[:5]
        a_s, b_s, c_s, xq_s, g_s = outs_s[:5]

        if l % 2 == 0:
            bias = _rel_bias(rel_bias_table[j])
            band = PAST_BAND + CHUNK
            o_p = band_attn_prompt(a_p, b_p, c_p, bias.reshape(H_A // 2, 2 * CHUNK, band), PAST_BAND)
            o_s = band_attn_sample(per_batch(a_s), per_batch(b_s), per_batch(c_s),
                                   cache_a_kt, cache_a_vt, j,
                                   bias[:, :dec_seq, :cache_len + dec_seq].reshape(
                                       H_A // 2, 2 * dec_seq, cache_len + dec_seq))
            assert keep == PAST_BAND
            ak_p.append(outs_p[5].reshape(1, keep, H_A, HD_A))
            av_p.append(outs_p[6].reshape(1, keep, H_A, HD_A))
            ak_s.append(outs_s[5].reshape(bs, dec_seq, H_A, HD_A))
            av_s.append(outs_s[6].reshape(bs, dec_seq, H_A, HD_A))
        else:
            lb = lb_all[j]
            lbc = jnp.stack([jnp.log(lb), jnp.log1p(-lb), 1.0 - lb])
            o_p, s_p = hgrn2(a_p.reshape(1, seq, MIX_WIDTH), b_p.reshape(1, seq, MIX_WIDTH),
                             c_p.reshape(1, seq, MIX_WIDTH), zero_state, 0, lbc, hgrn_norm_g[j],
                             CHUNK, HGRN_BLOCKS_PER_STEP)
            o_s, s_s = hgrn2(per_batch(a_s), per_batch(b_s), per_batch(c_s),
                             state_b.astype(F32), j, lbc, hgrn_norm_g[j], dec_seq, 1)
            sb_p.append(s_p)
            sb_s.append(s_s)
        mem_p = (mkt_p.reshape(1, 1, X_WIDTH, N_MEM), mvt_p.reshape(1, 1, X_WIDTH, N_MEM), 0)
        mem_s = (cache_mem_kt, cache_mem_vt, l)
        o_p = o_p.reshape(seq, MIX_WIDTH)
        o_s = o_s.reshape(n_s, MIX_WIDTH)
        if l == DEPTH - 1:
            xp = post_final(xp, o_p, xq_p, g_p, *mem_p, w_out_b, l, final_g, PAST_BAND, 1)
            xs = post_final(xs, o_s, xq_s, g_s, *mem_s, w_out_b, l, final_g, n_s, bs)
        else:
            dtypes, tails = layer_io(l + 1)
            xp, *outs_p = post_pre(xp, o_p, xq_p, g_p, *mem_p, w_out_b, l, ln_g[l + 1], w_in_b,
                                   dtypes, tails, PAST_BAND, 1)
            xs, *outs_s = post_pre(xs, o_s, xq_s, g_s, *mem_s, w_out_b, l, ln_g[l + 1], w_in_b,
                                   dtypes, tails, n_s, bs)
        mk_pl.append(jnp.moveaxis(mkt_p.reshape(1, H_X, HD_X, N_MEM), -1, 1))
        mv_pl.append(jnp.moveaxis(mvt_p.reshape(1, H_X, HD_X, N_MEM), -1, 1))
    return (xp.reshape(1, seq, D_MODEL), xs.reshape(bs, dec_seq, D_MODEL),
            jnp.stack(ak_p), jnp.stack(av_p), jnp.stack(sb_p), jnp.stack(mk_pl), jnp.stack(mv_pl),
            jnp.stack(ak_s), jnp.stack(av_s), jnp.stack(sb_s))
```

```python
import functools

import numpy as np
import jax
import jax.numpy as jnp
from jax import lax
from jax.experimental import pallas as pl
from jax.experimental.pallas import tpu as pltpu

D_MODEL = 1024
DEPTH = 4
CHUNK = 64
N_PAST_CHUNKS = 8
PAST_BAND = N_PAST_CHUNKS * CHUNK
MIX_WIDTH = 768
X_WIDTH = 256
D_INNER = MIX_WIDTH + X_WIDTH
HD_A = 64
H_A = MIX_WIDTH // HD_A
REL_CLIP = 128
DK_B = 128
H_B = MIX_WIDTH // DK_B
H_X = 4
HD_X = 64
N_MEM = 256
EPS = 1e-6
NEG = -1e30
LOG2_E = 1.4426950408889634
F32 = jnp.float32
BF16 = jnp.bfloat16

LANES = 128
VMEM_LIMIT_BYTES = 56 * 1024 * 1024
PROJ_SPLITS = (MIX_WIDTH, MIX_WIDTH, MIX_WIDTH, X_WIDTH, D_INNER)
CONTRACT_LAST = (((1,), (1,)), ((), ()))
HGRN_BLOCKS_PER_STEP = 8
BAND_UNROLL = 4


def _params(*sem):
    return pltpu.CompilerParams(dimension_semantics=sem, vmem_limit_bytes=VMEM_LIMIT_BYTES)


def _norm_proj_body(x, g_ref, w_ref, out_refs, splits, tails, transpose):
    ms = jnp.mean(x * x, axis=-1, keepdims=True)
    xn = (x * lax.rsqrt(ms + EPS) * g_ref[...]).astype(BF16)
    tail_refs = out_refs[len(splits):]
    off = 0
    for idx, (o_ref, n) in enumerate(zip(out_refs, splits)):
        r = jnp.dot(xn, w_ref[:, off:off + n], preferred_element_type=F32)
        o_ref[...] = (r.T if transpose else r).astype(o_ref.dtype)
        if idx in tails:
            tail_refs[tails.index(idx)][...] = r
        off += n


def _norm_proj_kernel(x_ref, g_ref, w_ref, *out_refs, splits, tails, transpose):
    _norm_proj_body(x_ref[...], g_ref, w_ref, out_refs, splits, tails, transpose)


def _proj_out_specs(rows, tm, splits, dtypes, tails):
    out_specs = [pl.BlockSpec((tm, n), lambda i: (i, 0)) for n in splits]
    out_shape = [jax.ShapeDtypeStruct((rows, n), dt) for n, dt in zip(splits, dtypes)]
    out_specs += [pl.BlockSpec((tm, splits[idx]), lambda i: (0, 0)) for idx in tails]
    out_shape += [jax.ShapeDtypeStruct((tm, splits[idx]), F32) for idx in tails]
    return out_specs, out_shape


def norm_proj(x2d, g, w_all, layer, splits, dtypes, tm, tails=(), transpose=False):
    rows, d = x2d.shape
    n_total = w_all.shape[2]
    assert rows % tm == 0 and (not transpose or rows == tm)
    if transpose:
        out_specs = [pl.BlockSpec((n, tm), lambda i: (0, 0)) for n in splits]
        out_shape = [jax.ShapeDtypeStruct((n, rows), dt) for n, dt in zip(splits, dtypes)]
    else:
        out_specs, out_shape = _proj_out_specs(rows, tm, splits, dtypes, tails)
    return pl.pallas_call(
        functools.partial(_norm_proj_kernel, splits=splits, tails=tuple(tails), transpose=transpose),
        grid=(rows // tm,),
        in_specs=[
            pl.BlockSpec((tm, d), lambda i: (i, 0)),
            pl.BlockSpec((1, d), lambda i: (0, 0)),
            pl.BlockSpec((None, d, n_total), lambda i: (layer, 0, 0)),
        ],
        out_specs=out_specs,
        out_shape=out_shape,
        name=f"norm_proj_{rows}x{n_total}",
        compiler_params=_params("arbitrary"),
    )(x2d, g.reshape(1, d), w_all)


def _stack_pair(q_pair):
    first = lax.broadcasted_iota(jnp.int32, (1, LANES), 1) < HD_A
    keep0 = jnp.where(first, 1.0, 0.0).astype(BF16)
    keep1 = jnp.where(first, 0.0, 1.0).astype(BF16)
    return jnp.concatenate([q_pair * keep0, q_pair * keep1], axis=0)


def _unstack_pair(o2):
    m = o2.shape[0] // 2
    first = lax.broadcasted_iota(jnp.int32, (m, LANES), 1) < HD_A
    return jnp.where(first, o2[:m], o2[m:])


def _pair_values(p, v_ext):
    r = jnp.dot(p, v_ext, preferred_element_type=F32)
    return _unstack_pair(r[:, :LANES] / r[:, LANES:])


def _pair_values_t(p, vt_ext):
    r = lax.dot_general(p, vt_ext, CONTRACT_LAST, preferred_element_type=F32)
    return _unstack_pair(r[:, :LANES] / r[:, LANES:])


def _band_attn_kernel(q_ref, k_ref, v_ref, bias_ref, o_ref,
                      kwin, vext, s_scr, p_scr, *, n_chunks, unroll):
    step = pl.program_id(0)
    chunk, prev_rows = CHUNK, PAST_BAND
    cur_rows = n_chunks * chunk
    band = prev_rows + chunk
    n_pairs = H_A // 2
    assert cur_rows == prev_rows

    @pl.when(step == 0)
    def _():
        kwin[...] = jnp.zeros_like(kwin)
        vext[...] = jnp.zeros_like(vext)

    ones = jnp.ones((prev_rows + cur_rows, LANES), BF16)
    for hp in range(n_pairs):
        lanes = slice(LANES * hp, LANES * (hp + 1))
        kwin[hp, 0:prev_rows, :] = kwin[hp, prev_rows:prev_rows + cur_rows, :]
        kwin[hp, prev_rows:prev_rows + cur_rows, :] = k_ref[:, lanes]
        vext[hp, 0:prev_rows, 0:LANES] = vext[hp, prev_rows:prev_rows + cur_rows, 0:LANES]
        vext[hp, prev_rows:prev_rows + cur_rows, 0:LANES] = v_ref[:, lanes]
        vext[hp, :, LANES:] = ones

    def chunks_body(ci, carry):
        starts = [pl.multiple_of((ci * unroll + u) * chunk, chunk) for u in range(unroll)]
        for u, r0 in enumerate(starts):
            for hp in range(n_pairs):
                lanes = slice(LANES * hp, LANES * (hp + 1))
                q2 = _stack_pair(q_ref[pl.ds(r0, chunk), lanes] * (HD_A ** -0.5))
                s_scr[u * n_pairs + hp] = lax.dot_general(
                    q2, kwin[hp, pl.ds(r0, band), :], CONTRACT_LAST, preferred_element_type=F32)
        for u, r0 in enumerate(starts):
            key_row = lax.broadcasted_iota(jnp.int32, (1, band), 1) + r0 + (step - 1) * prev_rows
            neg = jnp.where(key_row < 0, NEG, 0.0)
            for hp in range(n_pairs):
                s = s_scr[u * n_pairs + hp] + bias_ref[hp] + neg
                m = jnp.max(s, axis=-1, keepdims=True)
                p_scr[u * n_pairs + hp] = jnp.exp(s - m).astype(BF16)
        for u, r0 in enumerate(starts):
            for hp in range(n_pairs):
                o_ref[pl.ds(r0, chunk), LANES * hp:LANES * (hp + 1)] = _pair_values(
                    p_scr[u * n_pairs + hp],
                    vext[hp, pl.ds(r0, band), :]
                ).astype(o_ref.dtype)
        return carry

    lax.fori_loop(0, n_chunks // unroll, chunks_body, 0)


def band_attn_prompt(q, k, v, bias, block_rows):
    rows = q.shape[0]
    n_chunks = block_rows // CHUNK
    n_pairs = H_A // 2
    band = PAST_BAND + CHUNK
    assert block_rows == PAST_BAND
    cur = pl.BlockSpec((block_rows, MIX_WIDTH), lambda i: (i, 0))
    return pl.pallas_call(
        functools.partial(_band_attn_kernel, n_chunks=n_chunks, unroll=BAND_UNROLL),
        grid=(rows // block_rows,),
        in_specs=[cur, cur, cur, pl.BlockSpec(bias.shape, lambda i: (0, 0, 0))],
        out_specs=cur,
        out_shape=jax.ShapeDtypeStruct((rows, MIX_WIDTH), BF16),
        scratch_shapes=[pltpu.VMEM((n_pairs, PAST_BAND + block_rows, LANES), BF16),
                        pltpu.VMEM((n_pairs, PAST_BAND + block_rows, 2 * LANES), BF16),
                        pltpu.VMEM((BAND_UNROLL * n_pairs, 2 * CHUNK, band), F32),
                        pltpu.VMEM((BAND_UNROLL * n_pairs, 2 * CHUNK, band), BF16)],
        name="band_attn_prompt",
        compiler_params=_params("arbitrary"),
    )(q, k, v, bias)


def _band_sample_kernel(q_ref, k_ref, v_ref, kct_ref, vct_ref, bias_ref, o_ref, s_scr, p_scr,
                        *, cache_len):
    n_pairs = H_A // 2
    s_len = q_ref.shape[0]
    for hp in range(n_pairs):
        lanes = slice(LANES * hp, LANES * (hp + 1))
        q2 = _stack_pair(q_ref[:, lanes] * (HD_A ** -0.5))
        kct = kct_ref[2 * hp:2 * hp + 2].reshape(LANES, cache_len).astype(BF16)
        s_scr[hp, :, 0:cache_len] = jnp.dot(q2, kct, preferred_element_type=F32)
        s_scr[hp, :, cache_len:] = lax.dot_general(q2, k_ref[:, lanes], CONTRACT_LAST,
                                                   preferred_element_type=F32)
    for hp in range(n_pairs):
        s = s_scr[hp] + bias_ref[hp]
        p_scr[hp] = jnp.exp(s - jnp.max(s, axis=-1, keepdims=True)).astype(BF16)
    ones_old = jnp.ones((LANES, cache_len), BF16)
    ones_new = jnp.ones((s_len, LANES), BF16)
    for hp in range(n_pairs):
        lanes = slice(LANES * hp, LANES * (hp + 1))
        vct = vct_ref[2 * hp:2 * hp + 2].reshape(LANES, cache_len).astype(BF16)
        r = (lax.dot_general(p_scr[hp, :, 0:cache_len], jnp.concatenate([vct, ones_old], axis=0),
                             CONTRACT_LAST, preferred_element_type=F32)
             + jnp.dot(p_scr[hp, :, cache_len:], jnp.concatenate([v_ref[:, lanes], ones_new], axis=1),
                       preferred_element_type=F32))
        o_ref[:, lanes] = _unstack_pair(r[:, :LANES] / r[:, LANES:]).astype(o_ref.dtype)


def band_attn_sample(q, k, v, k_cache_t, v_cache_t, layer, bias):
    b, s_len, _ = q.shape
    cache_len = k_cache_t.shape[-1]
    new = pl.BlockSpec((None, s_len, MIX_WIDTH), lambda i: (i, 0, 0))
    old = pl.BlockSpec((None, None, H_A, HD_A, cache_len), lambda i: (layer, i, 0, 0, 0))
    return pl.pallas_call(
        functools.partial(_band_sample_kernel, cache_len=cache_len),
        grid=(b,),
        in_specs=[new, new, new, old, old, pl.BlockSpec(bias.shape, lambda i: (0, 0, 0))],
        out_specs=new,
        out_shape=jax.ShapeDtypeStruct((b, s_len, MIX_WIDTH), BF16),
        scratch_shapes=[pltpu.VMEM((H_A // 2, 2 * s_len, cache_len + s_len), F32),
                        pltpu.VMEM((H_A // 2, 2 * s_len, cache_len + s_len), BF16)],
        name="band_attn_sample",
        compiler_params=_params("arbitrary"),
    )(q, k, v, k_cache_t, v_cache_t, bias)


def _hgrn2_constants(t):
    halves = []
    h = t // 2
    while h >= 1:
        halves.append(h)
        h //= 2
    n_lvl = len(halves)
    w = np.zeros(((n_lvl + 2) * t, t), np.float32)
    masks = np.zeros((n_lvl + 1, t, t), np.float32)
    for row in range(t):
        w[row, :row + 1] = 1.0
        w[t + row, row + 1:] = 1.0
    for li, h in enumerate(halves):
        base = (2 + li) * t
        for row in range(t):
            r = (row // (2 * h)) * 2 * h + h - 1
            if row > r:
                w[base + row, r + 1:row + 1] = 1.0
            else:
                w[base + row, row + 1:r + 1] = 1.0
        for tq in range(t):
            for sk in range(t):
                if tq // (2 * h) == sk // (2 * h):
                    r = (tq // (2 * h)) * 2 * h + h - 1
                    if tq > r and sk <= r:
                        masks[li, tq, sk] = 1.0
    masks[n_lvl] = np.eye(t, dtype=np.float32)
    return jnp.asarray(np.tile(w, (1, 3)), BF16), jnp.asarray(masks, F32), n_lvl


def _hgrn2_kernel(a_ref, b_ref, c_ref, s0_ref, lbc_ref, g_ref, w_ref, m_ref,
                  o_ref, sout_ref,
                  st_ref, lf_scr, kk_scr, sums_scr, qs_scr, ks_scr, attn_scr, u_scr, oi_scr,
                  *, t, n_lvl, n_blk):
    ci = pl.program_id(1)
    inter = n_lvl + 1

    @pl.when(ci == 0)
    def _():
        for h in range(H_B):
            st_ref[h] = s0_ref[h].T

    def blk(g):
        return slice(g * t, (g + 1) * t)

    def head(h):
        return slice(LANES * h, LANES * (h + 1))

    z = b_ref[...]
    log_lb = lbc_ref[0:1, :]
    u = lbc_ref[1:2, :] + jnp.minimum(z, 0.0) - jnp.log(1.0 + jnp.exp(-jnp.abs(z)))
    log_f = jnp.maximum(log_lb, u) + jnp.log(1.0 + jnp.exp(-jnp.abs(log_lb - u)))
    kk_scr[...] = lbc_ref[2:3, :] / (1.0 + jnp.exp(z))
    log2_f = log_f * LOG2_E
    hi = log2_f.astype(BF16)
    r1 = log2_f - hi.astype(F32)
    mid = r1.astype(BF16)
    lo = (r1 - mid.astype(F32)).astype(BF16)
    for g in range(n_blk):
        lf_scr[g, 0:t] = hi[blk(g)]
        lf_scr[g, t:2 * t] = mid[blk(g)]
        lf_scr[g, 2 * t:3 * t] = lo[blk(g)]

    for g in range(n_blk):
        sums_scr[g] = jnp.dot(w_ref[...], lf_scr[g], preferred_element_type=F32)

    for g in range(n_blk):
        q = a_ref[blk(g), :].astype(F32)
        kk = kk_scr[blk(g), :]

        def put(slot, qv, kv):
            for h in range(H_B):
                qs_scr[g, slot, h] = qv[:, head(h)].astype(BF16)
                ks_scr[g, slot, h] = kv[:, head(h)].astype(BF16)

        for li in range(n_lvl):
            e = jnp.exp2(sums_scr[g, (2 + li) * t:(3 + li) * t])
            put(li, q * e, kk * e)
        put(n_lvl, q, kk)
        put(inter, q * jnp.exp2(sums_scr[g, 0:t]), kk * jnp.exp2(sums_scr[g, t:2 * t]))

    in_level = [m_ref[li] != 0.0 for li in range(n_lvl + 1)]
    for g in range(n_blk):
        for h in range(H_B):
            acc = jnp.zeros((t, t), F32)
            for li in range(n_lvl + 1):
                part = lax.dot_general(
                    qs_scr[g, li, h], ks_scr[g, li, h],
                    CONTRACT_LAST, preferred_element_type=F32)
                acc = jnp.where(in_level[li], part, acc)
            attn_scr[g, h] = acc.astype(BF16)

    for g in range(n_blk):
        for h in range(H_B):
            iv = c_ref[blk(g), head(h)]
            oi_scr[blk(g), head(h)] = jnp.dot(attn_scr[g, h], iv, preferred_element_type=F32)
            u_scr[g, h] = jnp.dot(iv.astype(F32).T.astype(BF16), ks_scr[g, inter, h],
                                  preferred_element_type=F32)

    for g in range(n_blk):
        e_last = jnp.exp2(sums_scr[g, t - 1:t, :])
        for h in range(H_B):
            st = st_ref[h]
            o = oi_scr[blk(g), head(h)] + lax.dot_general(
                qs_scr[g, inter, h], st.astype(BF16), CONTRACT_LAST,
                preferred_element_type=F32)
            st_ref[h] = e_last[:, head(h)] * st + u_scr[g, h]
            ms = jnp.mean(o * o, axis=-1, keepdims=True)
            o_ref[blk(g), head(h)] = (o * lax.rsqrt(ms + EPS) * g_ref[:, head(h)]).astype(o_ref.dtype)

    @pl.when(ci == pl.num_programs(1) - 1)
    def _():
        for h in range(H_B):
            sout_ref[h] = st_ref[h].T


def hgrn2(a, b, c, s0, layer, lbc, g, t, n_blk):
    bsz, rows, _ = a.shape
    w, masks, n_lvl = _hgrn2_constants(t)
    step_rows = n_blk * t
    tok = pl.BlockSpec((None, step_rows, MIX_WIDTH), lambda bi, ci: (bi, ci, 0))
    state = pl.BlockSpec((None, H_B, DK_B, DK_B), lambda bi, ci: (bi, 0, 0, 0))
    state_in = pl.BlockSpec((None, None, H_B, DK_B, DK_B), lambda bi, ci: (layer, bi, 0, 0, 0))
    scratch = [pltpu.VMEM((H_B, DK_B, DK_B), F32),
               pltpu.VMEM((n_blk, 3 * t, MIX_WIDTH), BF16),
               pltpu.VMEM((step_rows, MIX_WIDTH), F32),
               pltpu.VMEM((n_blk, (n_lvl + 2) * t, MIX_WIDTH), F32),
               pltpu.VMEM((n_blk, n_lvl + 2, H_B, t, DK_B), BF16),
               pltpu.VMEM((n_blk, n_lvl + 2, H_B, t, DK_B), BF16),
               pltpu.VMEM((n_blk, H_B, t, t), BF16),
               pltpu.VMEM((n_blk, H_B, DK_B, DK_B), F32),
               pltpu.VMEM((step_rows, MIX_WIDTH), F32)]
    return pl.pallas_call(
        functools.partial(_hgrn2_kernel, t=t, n_lvl=n_lvl, n_blk=n_blk),
        grid=(bsz, rows // step_rows),
        in_specs=[tok, tok, tok, state_in,
                  pl.BlockSpec(lbc.shape, lambda bi, ci: (0, 0)),
                  pl.BlockSpec((1, MIX_WIDTH), lambda bi, ci: (0, 0)),
                  pl.BlockSpec(w.shape, lambda bi, ci: (0, 0)),
                  pl.BlockSpec(masks.shape, lambda bi, ci: (0, 0, 0))],
        out_specs=[tok, state],
        out_shape=[jax.ShapeDtypeStruct((bsz, rows, MIX_WIDTH), BF16),
                   jax.ShapeDtypeStruct((bsz, H_B, DK_B, DK_B), F32)],
        scratch_shapes=scratch,
        name=f"hgrn2_t{t}",
        compiler_params=_params("arbitrary", "arbitrary"),
    )(a, b, c, s0, lbc, g.reshape(1, MIX_WIDTH), w, masks)


def _post_body(x_ref, mix_ref, xq_ref, gate_ref, mkt_ref, mvt_ref, w_ref, s_scr, p_scr, cross_scr,
               n_seg):
    seg_rows = x_ref.shape[0] // n_seg
    n_pairs = H_X // 2
    for seg in range(n_seg):
        rows = slice(seg * seg_rows, (seg + 1) * seg_rows)
        for hp in range(n_pairs):
            lanes = slice(LANES * hp, LANES * (hp + 1))
            q2 = _stack_pair(xq_ref[rows, lanes] * (HD_X ** -0.5))
            s_scr[seg * n_pairs + hp] = jnp.dot(q2, mkt_ref[seg, lanes, :].astype(BF16),
                                                preferred_element_type=F32)
    for i in range(n_seg * n_pairs):
        s = s_scr[i]
        p_scr[i] = jnp.exp(s - jnp.max(s, axis=-1, keepdims=True)).astype(BF16)
    ones = jnp.ones((LANES, N_MEM), BF16)
    for seg in range(n_seg):
        rows = slice(seg * seg_rows, (seg + 1) * seg_rows)
        for hp in range(n_pairs):
            lanes = slice(LANES * hp, LANES * (hp + 1))
            cross_scr[rows, lanes] = _pair_values_t(
                p_scr[seg * n_pairs + hp],
                jnp.concatenate([mvt_ref[seg, lanes, :].astype(BF16), ones], axis=0))

    gate = gate_ref[...]
    sg = gate / (1.0 + jnp.exp(-gate))
    y_mix = (mix_ref[...].astype(F32) * sg[:, 0:MIX_WIDTH]).astype(BF16)
    y_cross = (cross_scr[...] * sg[:, MIX_WIDTH:]).astype(BF16)
    return (x_ref[...] + jnp.dot(y_mix, w_ref[0:MIX_WIDTH, :], preferred_element_type=F32)
            + jnp.dot(y_cross, w_ref[MIX_WIDTH:, :], preferred_element_type=F32))


def _post_final_kernel(x_ref, mix_ref, xq_ref, gate_ref, mkt_ref, mvt_ref, w_ref, fg_ref, o_ref,
                       s_scr, p_scr, cross_scr, *, n_seg):
    acc = _post_body(x_ref, mix_ref, xq_ref, gate_ref, mkt_ref, mvt_ref, w_ref,
                     s_scr, p_scr, cross_scr, n_seg)
    ms = jnp.mean(acc * acc, axis=-1, keepdims=True)
    o_ref[...] = acc * lax.rsqrt(ms + EPS) * fg_ref[...]


def _post_pre_kernel(x_ref, mix_ref, xq_ref, gate_ref, mkt_ref, mvt_ref, wo_ref, g_ref, wi_ref,
                     xo_ref, *rest, n_seg, splits, tails):
    n_out = len(splits) + len(tails)
    s_scr, p_scr, cross_scr = rest[n_out:]
    acc = _post_body(x_ref, mix_ref, xq_ref, gate_ref, mkt_ref, mvt_ref, wo_ref,
                     s_scr, p_scr, cross_scr, n_seg)
    xo_ref[...] = acc
    _norm_proj_body(acc, g_ref, wi_ref, rest[:n_out], splits, tails, False)


def _post_specs(tm, n_seg, mem_layer, w_layer):
    def tok(n):
        return pl.BlockSpec((tm, n), lambda i: (i, 0))

    mem = pl.BlockSpec((None, n_seg, X_WIDTH, N_MEM), lambda i: (mem_layer, 0, 0, 0))
    w_out = pl.BlockSpec((None, D_INNER, D_MODEL), lambda i: (w_layer, 0, 0),
                         pipeline_mode=pl.Buffered(1))
    in_specs = [tok(D_MODEL), tok(MIX_WIDTH), tok(X_WIDTH), tok(D_INNER), mem, mem, w_out]
    seg_rows = tm // n_seg
    n_items = n_seg * (H_X // 2)
    scratch = [pltpu.VMEM((n_items, 2 * seg_rows, N_MEM), F32),
               pltpu.VMEM((n_items, 2 * seg_rows, N_MEM), BF16),
               pltpu.VMEM((tm, X_WIDTH), F32)]
    return in_specs, scratch, tok(D_MODEL)


def post_final(x, mix, xq, gate, mkt, mvt, mem_layer, w_all, w_layer, final_g, tm, n_seg):
    rows = x.shape[0]
    in_specs, scratch, x_spec = _post_specs(tm, n_seg, mem_layer, w_layer)
    return pl.pallas_call(
        functools.partial(_post_final_kernel, n_seg=n_seg),
        grid=(rows // tm,),
        in_specs=in_specs + [pl.BlockSpec((1, D_MODEL), lambda i: (0, 0))],
        out_specs=x_spec,
        out_shape=jax.ShapeDtypeStruct((rows, D_MODEL), F32),
        scratch_shapes=scratch,
        name=f"post_final_{rows}",
        compiler_params=_params("arbitrary"),
    )(x, mix, xq, gate, mkt, mvt, w_all, final_g.reshape(1, D_MODEL))


def post_pre(x, mix, xq, gate, mkt, mvt, mem_layer, w_out_all, layer, g_next, w_in_all,
             dtypes, tails, tm, n_seg):
    rows = x.shape[0]
    in_specs, scratch, x_spec = _post_specs(tm, n_seg, mem_layer, layer)
    in_specs += [pl.BlockSpec((1, D_MODEL), lambda i: (0, 0)),
                 pl.BlockSpec((None, D_MODEL, w_in_all.shape[2]), lambda i: (layer + 1, 0, 0),
                              pipeline_mode=pl.Buffered(1))]
    proj_specs, proj_shapes = _proj_out_specs(rows, tm, PROJ_SPLITS, dtypes, tails)
    return pl.pallas_call(
        functools.partial(_post_pre_kernel, n_seg=n_seg, splits=PROJ_SPLITS, tails=tuple(tails)),
        grid=(rows // tm,),
        in_specs=in_specs,
        out_specs=[x_spec] + proj_specs,
        out_shape=[jax.ShapeDtypeStruct((rows, D_MODEL), F32)] + proj_shapes,
        scratch_shapes=scratch,
        name=f"post_pre_{rows}",
        compiler_params=_params("arbitrary"),
    )(x, mix, xq, gate, mkt, mvt, w_out_all, g_next.reshape(1, D_MODEL), w_in_all)


def _rel_bias(table):
    band = PAST_BAND + CHUNK
    n_diag = band + CHUNK - 1
    offs = np.arange(n_diag) - (CHUNK - 1)
    idx = np.clip(PAST_BAND - offs, -REL_CLIP, REL_CLIP) + REL_CLIP
    diag = jnp.pad(table[:, idx].astype(F32), ((0, 0), (0, 1)))
    skew = jnp.tile(diag, (1, CHUNK))[:, :CHUNK * n_diag].reshape(H_A, CHUNK, n_diag)
    return skew[:, :, CHUNK - 1:CHUNK - 1 + band]


def _per_head_transposed(cache):
    return jnp.moveaxis(cache, -3, -1)


def kernel(x_prompt, x_sample, cache_a_k, cache_a_v, state_b, cache_mem_k, cache_mem_v, mem_prompt,
           ln_g, w_in, w_out, rel_bias_table, lower_bounds, hgrn_norm_g, mem_norm_g, w_mem_kv, final_g):
    bp, seq, _ = x_prompt.shape
    bs, dec_seq, _ = x_sample.shape
    assert bp == 1
    cache_len = cache_a_k.shape[2]
    n_s = bs * dec_seq
    keep = min(PAST_BAND, seq)

    w_in_b = w_in.astype(BF16)
    w_out_b = w_out.astype(BF16)
    w_mem_b = w_mem_kv.astype(BF16)

    lb_all = jnp.cumsum(jax.nn.softmax(lower_bounds.astype(F32), axis=0), axis=0)
    lb_all = lb_all - lb_all[:1]

    cache_a_kt = _per_head_transposed(cache_a_k)
    cache_a_vt = _per_head_transposed(cache_a_v)
    cache_mem_kt = _per_head_transposed(cache_mem_k).reshape(DEPTH, bs, X_WIDTH, N_MEM)
    cache_mem_vt = _per_head_transposed(cache_mem_v).reshape(DEPTH, bs, X_WIDTH, N_MEM)
    zero_state = jnp.zeros((1, 1, H_B, DK_B, DK_B), F32)

    def per_batch(u):
        return u.reshape(bs, dec_seq, u.shape[-1])

    def layer_io(l):
        attn_layer = l % 2 == 0
        return (BF16, BF16 if attn_layer else F32, BF16, BF16, F32), (1, 2) if attn_layer else ()

    xp = x_prompt.reshape(seq, D_MODEL)
    xs = x_sample.reshape(n_s, D_MODEL)
    dtypes, tails = layer_io(0)
    outs_p = norm_proj(xp, ln_g[0], w_in_b, 0, PROJ_SPLITS, dtypes, PAST_BAND, tails)
    outs_s = norm_proj(xs, ln_g[0], w_in_b, 0, PROJ_SPLITS, dtypes, n_s, tails)
    ak_p, av_p, sb_p, mk_pl, mv_pl, ak_s, av_s, sb_s = [], [], [], [], [], [], [], []
    for l in range(DEPTH):
        j = l // 2
        mkt_p, mvt_p = norm_proj(mem_prompt.reshape(N_MEM, D_MODEL), mem_norm_g[l], w_mem_b, l,
                                 (X_WIDTH, X_WIDTH), (F32, F32), N_MEM, transpose=True)
        a_p, b_p, c_p, xq_p, g_p = outs_p[:5]
        a_s, b_s, c_s, xq_s, g_s = outs_s[:5]

        if l % 2 == 0:
            bias = _rel_bias(rel_bias_table[j])
            band = PAST_BAND + CHUNK
            o_p = band_attn_prompt(a_p, b_p, c_p, bias.reshape(H_A // 2, 2 * CHUNK, band), PAST_BAND)
            o_s = band_attn_sample(per_batch(a_s), per_batch(b_s), per_batch(c_s),
                                   cache_a_kt, cache_a_vt, j,
                                   bias[:, :dec_seq, :cache_len + dec_seq].reshape(
                                       H_A // 2, 2 * dec_seq, cache_len + dec_seq))
            assert keep == PAST_BAND
            ak_p.append(outs_p[5].reshape(1, keep, H_A, HD_A))
            av_p.append(outs_p[6].reshape(1, keep, H_A, HD_A))
            ak_s.append(outs_s[5].reshape(bs, dec_seq, H_A, HD_A))
            av_s.append(outs_s[6].reshape(bs, dec_seq, H_A, HD_A))
        else:
            lb = lb_all[j]
            lbc = jnp.stack([jnp.log(lb), jnp.log1p(-lb), 1.0 - lb])
            o_p, s_p = hgrn2(a_p.reshape(1, seq, MIX_WIDTH), b_p.reshape(1, seq, MIX_WIDTH),
                             c_p.reshape(1, seq, MIX_WIDTH), zero_state, 0, lbc, hgrn_norm_g[j],
                             CHUNK, HGRN_BLOCKS_PER_STEP)
            o_s, s_s = hgrn2(per_batch(a_s), per_batch(b_s), per_batch(c_s),
                             state_b.astype(F32), j, lbc, hgrn_norm_g[j], dec_seq, 1)
            sb_p.append(s_p)
            sb_s.append(s_s)
        mem_p = (mkt_p.reshape(1, 1, X_WIDTH, N_MEM), mvt_p.reshape(1, 1, X_WIDTH, N_MEM), 0)
        mem_s = (cache_mem_kt, cache_mem_vt, l)
        o_p = o_p.reshape(seq, MIX_WIDTH)
        o_s = o_s.reshape(n_s, MIX_WIDTH)
        if l == DEPTH - 1:
            xp = post_final(xp, o_p, xq_p, g_p, *mem_p, w_out_b, l, final_g, PAST_BAND, 1)
            xs = post_final(xs, o_s, xq_s, g_s, *mem_s, w_out_b, l, final_g, n_s, bs)
        else:
            dtypes, tails = layer_io(l + 1)
            xp, *outs_p = post_pre(xp, o_p, xq_p, g_p, *mem_p, w_out_b, l, ln_g[l + 1], w_in_b,
                                   dtypes, tails, PAST_BAND, 1)
            xs, *outs_s = post_pre(xs, o_s, xq_s, g_s, *mem_s, w_out_b, l, ln_g[l + 1], w_in_b,
                                   dtypes, tails, n_s, bs)
        mk_pl.append(jnp.moveaxis(mkt_p.reshape(1, H_X, HD_X, N_MEM), -1, 1))
        mv_pl.append(jnp.moveaxis(mvt_p.reshape(1, H_X, HD_X, N_MEM), -1, 1))
    return (xp.reshape(1, seq, D_MODEL), xs.reshape(bs, dec_seq, D_MODEL),
            jnp.stack(ak_p), jnp.stack(av_p), jnp.stack(sb_p), jnp.stack(mk_pl), jnp.stack(mv_pl),
            jnp.stack(ak_s), jnp.stack(av_s), jnp.stack(sb_s))
```

```python
import functools

import numpy as np
import jax
import jax.numpy as jnp
from jax import lax
from jax.experimental import pallas as pl
from jax.experimental.pallas import tpu as pltpu

D_MODEL = 1024
DEPTH = 4
CHUNK = 64
N_PAST_CHUNKS = 8
PAST_BAND = N_PAST_CHUNKS * CHUNK
MIX_WIDTH = 768
X_WIDTH = 256
D_INNER = MIX_WIDTH + X_WIDTH
HD_A = 64
H_A = MIX_WIDTH // HD_A
REL_CLIP = 128
DK_B = 128
H_B = MIX_WIDTH // DK_B
H_X = 4
HD_X = 64
N_MEM = 256
EPS = 1e-6
NEG = -1e30
LOG2_E = 1.4426950408889634
F32 = jnp.float32
BF16 = jnp.bfloat16

LANES = 128
VMEM_LIMIT_BYTES = 56 * 1024 * 1024
PROJ_SPLITS = (MIX_WIDTH, MIX_WIDTH, MIX_WIDTH, X_WIDTH, D_INNER)
CONTRACT_LAST = (((1,), (1,)), ((), ()))
HGRN_BLOCKS_PER_STEP = 8
BAND_UNROLL = 4


def _params(*sem):
    return pltpu.CompilerParams(dimension_semantics=sem, vmem_limit_bytes=VMEM_LIMIT_BYTES)


def _norm_proj_body(x, g_ref, w_ref, out_refs, splits, tails, transpose):
    ms = jnp.mean(x * x, axis=-1, keepdims=True)
    xn = (x * lax.rsqrt(ms + EPS) * g_ref[...]).astype(BF16)
    tail_refs = out_refs[len(splits):]
    off = 0
    for idx, (o_ref, n) in enumerate(zip(out_refs, splits)):
        r = jnp.dot(xn, w_ref[:, off:off + n], preferred_element_type=F32)
        o_ref[...] = (r.T if transpose else r).astype(o_ref.dtype)
        if idx in tails:
            tail_refs[tails.index(idx)][...] = r
        off += n


def _norm_proj_kernel(x_ref, g_ref, w_ref, *out_refs, splits, tails):
    _norm_proj_body(x_ref[...], g_ref, w_ref, out_refs, splits, tails, False)


def _proj_out_specs(rows, tm, splits, dtypes, tails):
    out_specs = [pl.BlockSpec((tm, n), lambda i: (i, 0)) for n in splits]
    out_shape = [jax.ShapeDtypeStruct((rows, n), dt) for n, dt in zip(splits, dtypes)]
    tail_rows = min(PAST_BAND, rows)
    first_tail = (rows - tail_rows) // tm
    out_specs += [pl.BlockSpec((tm, splits[idx]), lambda i: (jnp.maximum(i - first_tail, 0), 0))
                  for idx in tails]
    out_shape += [jax.ShapeDtypeStruct((tail_rows, splits[idx]), F32) for idx in tails]
    return out_specs, out_shape


def norm_proj(x2d, g, w_all, layer, splits, dtypes, tm, tails=()):
    rows, d = x2d.shape
    n_total = w_all.shape[2]
    assert rows % tm == 0
    out_specs, out_shape = _proj_out_specs(rows, tm, splits, dtypes, tails)
    return pl.pallas_call(
        functools.partial(_norm_proj_kernel, splits=splits, tails=tuple(tails)),
        grid=(rows // tm,),
        in_specs=[
            pl.BlockSpec((tm, d), lambda i: (i, 0)),
            pl.BlockSpec((1, d), lambda i: (0, 0)),
            pl.BlockSpec((None, d, n_total), lambda i: (layer, 0, 0)),
        ],
        out_specs=out_specs,
        out_shape=out_shape,
        name=f"norm_proj_{rows}x{n_total}",
        compiler_params=_params("arbitrary"),
    )(x2d, g.reshape(1, d), w_all)


def _mem_kv_kernel(x_ref, g_ref, w_ref, kt_ref, vt_ref):
    _norm_proj_body(x_ref[...], g_ref, w_ref, (kt_ref, vt_ref), (X_WIDTH, X_WIDTH), (), True)


def mem_kv(mem, g_all, w_all):
    n_mem, d = mem.shape
    depth = w_all.shape[0]
    out = pl.BlockSpec((None, None, X_WIDTH, n_mem), lambda l: (l, 0, 0, 0))
    return pl.pallas_call(
        _mem_kv_kernel,
        grid=(depth,),
        in_specs=[pl.BlockSpec((n_mem, d), lambda l: (0, 0)),
                  pl.BlockSpec((None, 1, d), lambda l: (l, 0, 0)),
                  pl.BlockSpec((None, d, 2 * X_WIDTH), lambda l: (l, 0, 0))],
        out_specs=[out, out],
        out_shape=[jax.ShapeDtypeStruct((depth, 1, X_WIDTH, n_mem), F32)] * 2,
        name="mem_kv",
        compiler_params=_params("arbitrary"),
    )(mem, g_all.reshape(depth, 1, d), w_all)


def _stack_pair(q_pair):
    first = lax.broadcasted_iota(jnp.int32, (1, LANES), 1) < HD_A
    keep0 = jnp.where(first, 1.0, 0.0).astype(BF16)
    keep1 = jnp.where(first, 0.0, 1.0).astype(BF16)
    return jnp.concatenate([q_pair * keep0, q_pair * keep1], axis=0)


def _unstack_pair(o2):
    m = o2.shape[0] // 2
    first = lax.broadcasted_iota(jnp.int32, (m, LANES), 1) < HD_A
    return jnp.where(first, o2[:m], o2[m:])


def _pair_values(p, v_ext):
    r = jnp.dot(p, v_ext, preferred_element_type=F32)
    return _unstack_pair(r[:, :LANES] / r[:, LANES:])


def _pair_values_t(p, vt_ext):
    r = lax.dot_general(p, vt_ext, CONTRACT_LAST, preferred_element_type=F32)
    return _unstack_pair(r[:, :LANES] / r[:, LANES:])


def _band_attn_kernel(q_ref, k_ref, v_ref, bias_ref, o_ref,
                      kwin, vext, s_scr, p_scr, *, n_chunks, unroll):
    step = pl.program_id(0)
    chunk, prev_rows = CHUNK, PAST_BAND
    cur_rows = n_chunks * chunk
    band = prev_rows + chunk
    n_pairs = H_A // 2
    assert cur_rows == prev_rows

    @pl.when(step == 0)
    def _():
        kwin[...] = jnp.zeros_like(kwin)
        vext[...] = jnp.zeros_like(vext)

    kwin[0:prev_rows, :] = kwin[prev_rows:prev_rows + cur_rows, :]
    kwin[prev_rows:prev_rows + cur_rows, :] = k_ref[...]
    ones = jnp.ones((prev_rows + cur_rows, LANES), BF16)
    for hp in range(n_pairs):
        lanes = slice(LANES * hp, LANES * (hp + 1))
        vcol = slice(2 * LANES * hp, 2 * LANES * hp + LANES)
        vext[0:prev_rows, vcol] = vext[prev_rows:prev_rows + cur_rows, vcol]
        vext[prev_rows:prev_rows + cur_rows, vcol] = v_ref[:, lanes]
        vext[:, 2 * LANES * hp + LANES:2 * LANES * (hp + 1)] = ones

    def chunks_body(ci, carry):
        starts = [pl.multiple_of((ci * unroll + u) * chunk, chunk) for u in range(unroll)]
        for u, r0 in enumerate(starts):
            for hp in range(n_pairs):
                lanes = slice(LANES * hp, LANES * (hp + 1))
                q2 = _stack_pair(q_ref[pl.ds(r0, chunk), lanes] * (HD_A ** -0.5))
                s_scr[u * n_pairs + hp] = lax.dot_general(
                    q2, kwin[pl.ds(r0, band), lanes], CONTRACT_LAST, preferred_element_type=F32)
        for u, r0 in enumerate(starts):
            key_row = lax.broadcasted_iota(jnp.int32, (1, band), 1) + r0 + (step - 1) * prev_rows
            neg = jnp.where(key_row < 0, NEG, 0.0)
            for hp in range(n_pairs):
                s = s_scr[u * n_pairs + hp] + bias_ref[hp] + neg
                m = jnp.max(s, axis=-1, keepdims=True)
                p_scr[u * n_pairs + hp] = jnp.exp(s - m).astype(BF16)
        for u, r0 in enumerate(starts):
            for hp in range(n_pairs):
                o_ref[pl.ds(r0, chunk), LANES * hp:LANES * (hp + 1)] = _pair_values(
                    p_scr[u * n_pairs + hp],
                    vext[pl.ds(r0, band), 2 * LANES * hp:2 * LANES * (hp + 1)]
                ).astype(o_ref.dtype)
        return carry

    lax.fori_loop(0, n_chunks // unroll, chunks_body, 0)


def band_attn_prompt(q, k, v, bias, block_rows):
    rows = q.shape[0]
    n_chunks = block_rows // CHUNK
    n_pairs = H_A // 2
    band = PAST_BAND + CHUNK
    assert block_rows == PAST_BAND
    cur = pl.BlockSpec((block_rows, MIX_WIDTH), lambda i: (i, 0))
    return pl.pallas_call(
        functools.partial(_band_attn_kernel, n_chunks=n_chunks, unroll=BAND_UNROLL),
        grid=(rows // block_rows,),
        in_specs=[cur, cur, cur, pl.BlockSpec(bias.shape, lambda i: (0, 0, 0))],
        out_specs=cur,
        out_shape=jax.ShapeDtypeStruct((rows, MIX_WIDTH), BF16),
        scratch_shapes=[pltpu.VMEM((PAST_BAND + block_rows, MIX_WIDTH), BF16),
                        pltpu.VMEM((PAST_BAND + block_rows, 2 * MIX_WIDTH), BF16),
                        pltpu.VMEM((BAND_UNROLL * n_pairs, 2 * CHUNK, band), F32),
                        pltpu.VMEM((BAND_UNROLL * n_pairs, 2 * CHUNK, band), BF16)],
        name="band_attn_prompt",
        compiler_params=_params("arbitrary"),
    )(q, k, v, bias)


def _band_sample_kernel(q_ref, k_ref, v_ref, kct_ref, vct_ref, bias_ref, o_ref, s_scr, p_scr,
                        *, cache_len):
    n_pairs = H_A // 2
    s_len = q_ref.shape[0]
    for hp in range(n_pairs):
        lanes = slice(LANES * hp, LANES * (hp + 1))
        q2 = _stack_pair(q_ref[:, lanes] * (HD_A ** -0.5))
        kct = kct_ref[2 * hp:2 * hp + 2].reshape(LANES, cache_len).astype(BF16)
        s_scr[hp, :, 0:cache_len] = jnp.dot(q2, kct, preferred_element_type=F32)
        s_scr[hp, :, cache_len:] = lax.dot_general(q2, k_ref[:, lanes], CONTRACT_LAST,
                                                   preferred_element_type=F32)
    for hp in range(n_pairs):
        s = s_scr[hp] + bias_ref[hp]
        p_scr[hp] = jnp.exp(s - jnp.max(s, axis=-1, keepdims=True)).astype(BF16)
    ones_old = jnp.ones((LANES, cache_len), BF16)
    ones_new = jnp.ones((s_len, LANES), BF16)
    for hp in range(n_pairs):
        lanes = slice(LANES * hp, LANES * (hp + 1))
        vct = vct_ref[2 * hp:2 * hp + 2].reshape(LANES, cache_len).astype(BF16)
        r = (lax.dot_general(p_scr[hp, :, 0:cache_len], jnp.concatenate([vct, ones_old], axis=0),
                             CONTRACT_LAST, preferred_element_type=F32)
             + jnp.dot(p_scr[hp, :, cache_len:], jnp.concatenate([v_ref[:, lanes], ones_new], axis=1),
                       preferred_element_type=F32))
        o_ref[:, lanes] = _unstack_pair(r[:, :LANES] / r[:, LANES:]).astype(o_ref.dtype)


def band_attn_sample(q, k, v, k_cache_t, v_cache_t, layer, bias):
    b, s_len, _ = q.shape
    cache_len = k_cache_t.shape[-1]
    new = pl.BlockSpec((None, s_len, MIX_WIDTH), lambda i: (i, 0, 0))
    old = pl.BlockSpec((None, None, H_A, HD_A, cache_len), lambda i: (layer, i, 0, 0, 0))
    return pl.pallas_call(
        functools.partial(_band_sample_kernel, cache_len=cache_len),
        grid=(b,),
        in_specs=[new, new, new, old, old, pl.BlockSpec(bias.shape, lambda i: (0, 0, 0))],
        out_specs=new,
        out_shape=jax.ShapeDtypeStruct((b, s_len, MIX_WIDTH), BF16),
        scratch_shapes=[pltpu.VMEM((H_A // 2, 2 * s_len, cache_len + s_len), F32),
                        pltpu.VMEM((H_A // 2, 2 * s_len, cache_len + s_len), BF16)],
        name="band_attn_sample",
        compiler_params=_params("arbitrary"),
    )(q, k, v, k_cache_t, v_cache_t, bias)


def _hgrn2_constants(t):
    halves = []
    h = t // 2
    while h >= 1:
        halves.append(h)
        h //= 2
    n_lvl = len(halves)
    w = np.zeros(((n_lvl + 2) * t, t), np.float32)
    masks = np.zeros((n_lvl + 1, t, t), np.float32)
    for row in range(t):
        w[row, :row + 1] = 1.0
        w[t + row, row + 1:] = 1.0
    for li, h in enumerate(halves):
        base = (2 + li) * t
        for row in range(t):
            r = (row // (2 * h)) * 2 * h + h - 1
            if row > r:
                w[base + row, r + 1:row + 1] = 1.0
            else:
                w[base + row, row + 1:r + 1] = 1.0
        for tq in range(t):
            for sk in range(t):
                if tq // (2 * h) == sk // (2 * h):
                    r = (tq // (2 * h)) * 2 * h + h - 1
                    if tq > r and sk <= r:
                        masks[li, tq, sk] = 1.0
    masks[n_lvl] = np.eye(t, dtype=np.float32)
    return jnp.asarray(np.tile(w, (1, 3)), BF16), jnp.asarray(masks, F32), n_lvl


def _hgrn2_body(a_ref, b_ref, c_ref, s0_ref, lbc_ref, g_ref, w_ref, m_ref, o_ref, sout_ref,
                st_ref, lf_scr, kk_scr, sums_scr, qs_scr, ks_scr, attn_scr, u_scr, oi_scr,
                *, t, n_lvl, n_blk, ci, n_steps):
    inter = n_lvl + 1

    @pl.when(ci == 0)
    def _():
        for h in range(H_B):
            st_ref[h] = s0_ref[h].T

    def blk(g):
        return slice(g * t, (g + 1) * t)

    def head(h):
        return slice(LANES * h, LANES * (h + 1))

    z = b_ref[...]
    log_lb = lbc_ref[0:1, :]
    u = lbc_ref[1:2, :] + jnp.minimum(z, 0.0) - jnp.log(1.0 + jnp.exp(-jnp.abs(z)))
    log_f = jnp.maximum(log_lb, u) + jnp.log(1.0 + jnp.exp(-jnp.abs(log_lb - u)))
    kk_scr[...] = lbc_ref[2:3, :] / (1.0 + jnp.exp(z))
    log2_f = log_f * LOG2_E
    hi = log2_f.astype(BF16)
    r1 = log2_f - hi.astype(F32)
    mid = r1.astype(BF16)
    lo = (r1 - mid.astype(F32)).astype(BF16)
    for g in range(n_blk):
        lf_scr[g, 0:t] = hi[blk(g)]
        lf_scr[g, t:2 * t] = mid[blk(g)]
        lf_scr[g, 2 * t:3 * t] = lo[blk(g)]

    for g in range(n_blk):
        sums_scr[g] = jnp.dot(w_ref[...], lf_scr[g], preferred_element_type=F32)

    for g in range(n_blk):
        q = a_ref[blk(g), :].astype(F32)
        kk = kk_scr[blk(g), :]
        for li in range(n_lvl):
            e = jnp.exp2(sums_scr[g, (2 + li) * t:(3 + li) * t])
            qs_scr[g, li] = (q * e).astype(BF16)
            ks_scr[g, li] = (kk * e).astype(BF16)
        qs_scr[g, n_lvl] = a_ref[blk(g), :]
        ks_scr[g, n_lvl] = kk.astype(BF16)
        qs_scr[g, inter] = (q * jnp.exp2(sums_scr[g, 0:t])).astype(BF16)
        ks_scr[g, inter] = (kk * jnp.exp2(sums_scr[g, t:2 * t])).astype(BF16)

    in_level = [m_ref[li] != 0.0 for li in range(n_lvl + 1)]
    for g in range(n_blk):
        for h in range(H_B):
            acc = jnp.zeros((t, t), F32)
            for li in range(n_lvl + 1):
                part = lax.dot_general(
                    qs_scr[g, li, :, head(h)], ks_scr[g, li, :, head(h)],
                    CONTRACT_LAST, preferred_element_type=F32)
                acc = jnp.where(in_level[li], part, acc)
            attn_scr[g, h] = acc.astype(BF16)

    for g in range(n_blk):
        for h in range(H_B):
            iv = c_ref[blk(g), head(h)]
            oi_scr[blk(g), head(h)] = jnp.dot(attn_scr[g, h], iv, preferred_element_type=F32)
            u_scr[g, h] = jnp.dot(iv.astype(F32).T.astype(BF16), ks_scr[g, inter, :, head(h)],
                                  preferred_element_type=F32)

    for g in range(n_blk):
        e_last = jnp.exp2(sums_scr[g, t - 1:t, :])
        for h in range(H_B):
            st = st_ref[h]
            o = oi_scr[blk(g), head(h)] + lax.dot_general(
                qs_scr[g, inter, :, head(h)], st.astype(BF16), CONTRACT_LAST,
                preferred_element_type=F32)
            st_ref[h] = e_last[:, head(h)] * st + u_scr[g, h]
            ms = jnp.mean(o * o, axis=-1, keepdims=True)
            o_ref[blk(g), head(h)] = (o * lax.rsqrt(ms + EPS) * g_ref[:, head(h)]).astype(o_ref.dtype)

    @pl.when(ci == n_steps - 1)
    def _():
        for h in range(H_B):
            sout_ref[h] = st_ref[h].T


def _hgrn2_kernel(*refs, t, n_lvl, n_blk):
    _hgrn2_body(*refs, t=t, n_lvl=n_lvl, n_blk=n_blk, ci=pl.program_id(1), n_steps=pl.num_programs(1))


def _hgrn2_scratch(t, n_lvl, n_blk):
    step_rows = n_blk * t
    return [pltpu.VMEM((H_B, DK_B, DK_B), F32),
            pltpu.VMEM((n_blk, 3 * t, MIX_WIDTH), BF16),
            pltpu.VMEM((step_rows, MIX_WIDTH), F32),
            pltpu.VMEM((n_blk, (n_lvl + 2) * t, MIX_WIDTH), F32),
            pltpu.VMEM((n_blk, n_lvl + 2, t, MIX_WIDTH), BF16),
            pltpu.VMEM((n_blk, n_lvl + 2, t, MIX_WIDTH), BF16),
            pltpu.VMEM((n_blk, H_B, t, t), BF16),
            pltpu.VMEM((n_blk, H_B, DK_B, DK_B), F32),
            pltpu.VMEM((step_rows, MIX_WIDTH), F32)]


def hgrn2(a, b, c, s0, layer, lbc, g, t, n_blk):
    bsz, rows, _ = a.shape
    w, masks, n_lvl = _hgrn2_constants(t)
    step_rows = n_blk * t
    tok = pl.BlockSpec((None, step_rows, MIX_WIDTH), lambda bi, ci: (bi, ci, 0))
    state = pl.BlockSpec((None, H_B, DK_B, DK_B), lambda bi, ci: (bi, 0, 0, 0))
    state_in = pl.BlockSpec((None, None, H_B, DK_B, DK_B), lambda bi, ci: (layer, bi, 0, 0, 0))
    return pl.pallas_call(
        functools.partial(_hgrn2_kernel, t=t, n_lvl=n_lvl, n_blk=n_blk),
        grid=(bsz, rows // step_rows),
        in_specs=[tok, tok, tok, state_in,
                  pl.BlockSpec(lbc.shape, lambda bi, ci: (0, 0)),
                  pl.BlockSpec((1, MIX_WIDTH), lambda bi, ci: (0, 0)),
                  pl.BlockSpec(w.shape, lambda bi, ci: (0, 0)),
                  pl.BlockSpec(masks.shape, lambda bi, ci: (0, 0, 0))],
        out_specs=[tok, state],
        out_shape=[jax.ShapeDtypeStruct((bsz, rows, MIX_WIDTH), BF16),
                   jax.ShapeDtypeStruct((bsz, H_B, DK_B, DK_B), F32)],
        scratch_shapes=_hgrn2_scratch(t, n_lvl, n_blk),
        name=f"hgrn2_t{t}",
        compiler_params=_params("arbitrary", "arbitrary"),
    )(a, b, c, s0, lbc, g.reshape(1, MIX_WIDTH), w, masks)


def _post_body(x_ref, mix_ref, xq_ref, gate_ref, mkt_ref, mvt_ref, w_ref, s_scr, p_scr, cross_scr,
               n_seg):
    seg_rows = x_ref.shape[0] // n_seg
    n_pairs = H_X // 2
    for seg in range(n_seg):
        rows = slice(seg * seg_rows, (seg + 1) * seg_rows)
        for hp in range(n_pairs):
            lanes = slice(LANES * hp, LANES * (hp + 1))
            q2 = _stack_pair(xq_ref[rows, lanes] * (HD_X ** -0.5))
            s_scr[seg * n_pairs + hp] = jnp.dot(q2, mkt_ref[seg, lanes, :].astype(BF16),
                                                preferred_element_type=F32)
    for i in range(n_seg * n_pairs):
        s = s_scr[i]
        p_scr[i] = jnp.exp(s - jnp.max(s, axis=-1, keepdims=True)).astype(BF16)
    ones = jnp.ones((LANES, N_MEM), BF16)
    for seg in range(n_seg):
        rows = slice(seg * seg_rows, (seg + 1) * seg_rows)
        for hp in range(n_pairs):
            lanes = slice(LANES * hp, LANES * (hp + 1))
            cross_scr[rows, lanes] = _pair_values_t(
                p_scr[seg * n_pairs + hp],
                jnp.concatenate([mvt_ref[seg, lanes, :].astype(BF16), ones], axis=0))

    gate = gate_ref[...]
    sg = gate / (1.0 + jnp.exp(-gate))
    y_mix = (mix_ref[...].astype(F32) * sg[:, 0:MIX_WIDTH]).astype(BF16)
    y_cross = (cross_scr[...] * sg[:, MIX_WIDTH:]).astype(BF16)
    return (x_ref[...] + jnp.dot(y_mix, w_ref[0:MIX_WIDTH, :], preferred_element_type=F32)
            + jnp.dot(y_cross, w_ref[MIX_WIDTH:, :], preferred_element_type=F32))


def _post_final_kernel(x_ref, mix_ref, xq_ref, gate_ref, mkt_ref, mvt_ref, w_ref, fg_ref, o_ref,
                       s_scr, p_scr, cross_scr, *, n_seg):
    acc = _post_body(x_ref, mix_ref, xq_ref, gate_ref, mkt_ref, mvt_ref, w_ref,
                     s_scr, p_scr, cross_scr, n_seg)
    ms = jnp.mean(acc * acc, axis=-1, keepdims=True)
    o_ref[...] = acc * lax.rsqrt(ms + EPS) * fg_ref[...]


def _post_pre_kernel(x_ref, mix_ref, xq_ref, gate_ref, mkt_ref, mvt_ref, wo_ref, g_ref, wi_ref,
                     xo_ref, *rest, n_seg, splits, tails):
    n_out = len(splits) + len(tails)
    s_scr, p_scr, cross_scr = rest[n_out:]
    acc = _post_body(x_ref, mix_ref, xq_ref, gate_ref, mkt_ref, mvt_ref, wo_ref,
                     s_scr, p_scr, cross_scr, n_seg)
    xo_ref[...] = acc
    _norm_proj_body(acc, g_ref, wi_ref, rest[:n_out], splits, tails, False)


def _post_specs(tm, n_seg, mem_layer, w_layer):
    def tok(n):
        return pl.BlockSpec((tm, n), lambda i: (i, 0))

    mem = pl.BlockSpec((None, n_seg, X_WIDTH, N_MEM), lambda i: (mem_layer, 0, 0, 0))
    w_out = pl.BlockSpec((None, D_INNER, D_MODEL), lambda i: (w_layer, 0, 0),
                         pipeline_mode=pl.Buffered(1))
    in_specs = [tok(D_MODEL), tok(MIX_WIDTH), tok(X_WIDTH), tok(D_INNER), mem, mem, w_out]
    seg_rows = tm // n_seg
    n_items = n_seg * (H_X // 2)
    scratch = [pltpu.VMEM((n_items, 2 * seg_rows, N_MEM), F32),
               pltpu.VMEM((n_items, 2 * seg_rows, N_MEM), BF16),
               pltpu.VMEM((tm, X_WIDTH), F32)]
    return in_specs, scratch, tok(D_MODEL)


def post_final(x, mix, xq, gate, mkt, mvt, mem_layer, w_all, w_layer, final_g, tm, n_seg):
    rows = x.shape[0]
    in_specs, scratch, x_spec = _post_specs(tm, n_seg, mem_layer, w_layer)
    return pl.pallas_call(
        functools.partial(_post_final_kernel, n_seg=n_seg),
        grid=(rows // tm,),
        in_specs=in_specs + [pl.BlockSpec((1, D_MODEL), lambda i: (0, 0))],
        out_specs=x_spec,
        out_shape=jax.ShapeDtypeStruct((rows, D_MODEL), F32),
        scratch_shapes=scratch,
        name=f"post_final_{rows}",
        compiler_params=_params("arbitrary"),
    )(x, mix, xq, gate, mkt, mvt, w_all, final_g.reshape(1, D_MODEL))


def post_pre(x, mix, xq, gate, mkt, mvt, mem_layer, w_out_all, layer, g_next, w_in_all,
             dtypes, tails, tm, n_seg):
    rows = x.shape[0]
    in_specs, scratch, x_spec = _post_specs(tm, n_seg, mem_layer, layer)
    in_specs += [pl.BlockSpec((1, D_MODEL), lambda i: (0, 0)),
                 pl.BlockSpec((None, D_MODEL, w_in_all.shape[2]), lambda i: (layer + 1, 0, 0),
                              pipeline_mode=pl.Buffered(1))]
    proj_specs, proj_shapes = _proj_out_specs(rows, tm, PROJ_SPLITS, dtypes, tails)
    return pl.pallas_call(
        functools.partial(_post_pre_kernel, n_seg=n_seg, splits=PROJ_SPLITS, tails=tuple(tails)),
        grid=(rows // tm,),
        in_specs=in_specs,
        out_specs=[x_spec] + proj_specs,
        out_shape=[jax.ShapeDtypeStruct((rows, D_MODEL), F32)] + proj_shapes,
        scratch_shapes=scratch,
        name=f"post_pre_{rows}",
        compiler_params=_params("arbitrary"),
    )(x, mix, xq, gate, mkt, mvt, w_out_all, g_next.reshape(1, D_MODEL), w_in_all)


def _rel_bias(table):
    band = PAST_BAND + CHUNK
    n_diag = band + CHUNK - 1
    offs = np.arange(n_diag) - (CHUNK - 1)
    idx = np.clip(PAST_BAND - offs, -REL_CLIP, REL_CLIP) + REL_CLIP
    diag = jnp.pad(table[:, idx].astype(F32), ((0, 0), (0, 1)))
    skew = jnp.tile(diag, (1, CHUNK))[:, :CHUNK * n_diag].reshape(H_A, CHUNK, n_diag)
    return skew[:, :, CHUNK - 1:CHUNK - 1 + band]


def _per_head_transposed(cache):
    return jnp.moveaxis(cache, -3, -1)


def kernel(x_prompt, x_sample, cache_a_k, cache_a_v, state_b, cache_mem_k, cache_mem_v, mem_prompt,
           ln_g, w_in, w_out, rel_bias_table, lower_bounds, hgrn_norm_g, mem_norm_g, w_mem_kv, final_g):
    bp, seq, _ = x_prompt.shape
    bs, dec_seq, _ = x_sample.shape
    assert bp == 1
    cache_len = cache_a_k.shape[2]
    n_s = bs * dec_seq
    keep = min(PAST_BAND, seq)

    w_in_b = w_in.astype(BF16)
    w_out_b = w_out.astype(BF16)
    w_mem_b = w_mem_kv.astype(BF16)

    lb_all = jnp.cumsum(jax.nn.softmax(lower_bounds.astype(F32), axis=0), axis=0)
    lb_all = lb_all - lb_all[:1]

    cache_a_kt = _per_head_transposed(cache_a_k)
    cache_a_vt = _per_head_transposed(cache_a_v)
    cache_mem_kt = _per_head_transposed(cache_mem_k).reshape(DEPTH, bs, X_WIDTH, N_MEM)
    cache_mem_vt = _per_head_transposed(cache_mem_v).reshape(DEPTH, bs, X_WIDTH, N_MEM)
    zero_state = jnp.zeros((1, 1, H_B, DK_B, DK_B), F32)

    def per_batch(u):
        return u.reshape(bs, dec_seq, u.shape[-1])

    def layer_io(l):
        attn_layer = l % 2 == 0
        return (BF16, BF16 if attn_layer else F32, BF16, BF16, F32), (1, 2) if attn_layer else ()

    mkt_all, mvt_all = mem_kv(mem_prompt.reshape(N_MEM, D_MODEL), mem_norm_g, w_mem_b)
    mem_k_prompt = jnp.moveaxis(mkt_all.reshape(DEPTH, 1, H_X, HD_X, N_MEM), -1, 2)
    mem_v_prompt = jnp.moveaxis(mvt_all.reshape(DEPTH, 1, H_X, HD_X, N_MEM), -1, 2)

    xp = x_prompt.reshape(seq, D_MODEL)
    xs = x_sample.reshape(n_s, D_MODEL)
    dtypes, tails = layer_io(0)
    outs_p = norm_proj(xp, ln_g[0], w_in_b, 0, PROJ_SPLITS, dtypes, PAST_BAND, tails)
    outs_s = norm_proj(xs, ln_g[0], w_in_b, 0, PROJ_SPLITS, dtypes, n_s, tails)
    ak_p, av_p, sb_p, ak_s, av_s, sb_s = [], [], [], [], [], []
    for l in range(DEPTH):
        j = l // 2
        a_p, b_p, c_p, xq_p, g_p = outs_p[:5]
        a_s, b_s, c_s, xq_s, g_s = outs_s[:5]

        if l % 2 == 0:
            bias = _rel_bias(rel_bias_table[j])
            band = PAST_BAND + CHUNK
            o_p = band_attn_prompt(a_p, b_p, c_p, bias.reshape(H_A // 2, 2 * CHUNK, band), PAST_BAND)
            o_s = band_attn_sample(per_batch(a_s), per_batch(b_s), per_batch(c_s),
                                   cache_a_kt, cache_a_vt, j,
                                   bias[:, :dec_seq, :cache_len + dec_seq].reshape(
                                       H_A // 2, 2 * dec_seq, cache_len + dec_seq))
            assert keep == PAST_BAND
            ak_p.append(outs_p[5].reshape(1, keep, H_A, HD_A))
            av_p.append(outs_p[6].reshape(1, keep, H_A, HD_A))
            ak_s.append(outs_s[5].reshape(bs, dec_seq, H_A, HD_A))
            av_s.append(outs_s[6].reshape(bs, dec_seq, H_A, HD_A))
        else:
            lb = lb_all[j]
            lbc = jnp.stack([jnp.log(lb), jnp.log1p(-lb), 1.0 - lb])
            o_p, s_p = hgrn2(a_p.reshape(1, seq, MIX_WIDTH), b_p.reshape(1, seq, MIX_WIDTH),
                             c_p.reshape(1, seq, MIX_WIDTH), zero_state, 0, lbc, hgrn_norm_g[j],
                             CHUNK, HGRN_BLOCKS_PER_STEP)
            o_s, s_s = hgrn2(per_batch(a_s), per_batch(b_s), per_batch(c_s),
                             state_b.astype(F32), j, lbc, hgrn_norm_g[j], dec_seq, 1)
            sb_p.append(s_p)
            sb_s.append(s_s)
        mem_p = (mkt_all, mvt_all, l)
        mem_s = (cache_mem_kt, cache_mem_vt, l)
        o_p = o_p.reshape(seq, MIX_WIDTH)
        o_s = o_s.reshape(n_s, MIX_WIDTH)
        if l == DEPTH - 1:
            xp = post_final(xp, o_p, xq_p, g_p, *mem_p, w_out_b, l, final_g, PAST_BAND, 1)
            xs = post_final(xs, o_s, xq_s, g_s, *mem_s, w_out_b, l, final_g, n_s, bs)
        else:
            dtypes, tails = layer_io(l + 1)
            xp, *outs_p = post_pre(xp, o_p, xq_p, g_p, *mem_p, w_out_b, l, ln_g[l + 1], w_in_b,
                                   dtypes, tails, PAST_BAND, 1)
            xs, *outs_s = post_pre(xs, o_s, xq_s, g_s, *mem_s, w_out_b, l, ln_g[l + 1], w_in_b,
                                   dtypes, tails, n_s, bs)
    return (xp.reshape(1, seq, D_MODEL), xs.reshape(bs, dec_seq, D_MODEL),
            jnp.stack(ak_p), jnp.stack(av_p), jnp.stack(sb_p), mem_k_prompt, mem_v_prompt,
            jnp.stack(ak_s), jnp.stack(av_s), jnp.stack(sb_s))
```

```python
import functools

import numpy as np
import jax
import jax.numpy as jnp
from jax import lax
from jax.experimental import pallas as pl
from jax.experimental.pallas import tpu as pltpu

D_MODEL = 1024
DEPTH = 4
CHUNK = 64
N_PAST_CHUNKS = 8
PAST_BAND = N_PAST_CHUNKS * CHUNK
MIX_WIDTH = 768
X_WIDTH = 256
D_INNER = MIX_WIDTH + X_WIDTH
HD_A = 64
H_A = MIX_WIDTH // HD_A
REL_CLIP = 128
DK_B = 128
H_B = MIX_WIDTH // DK_B
H_X = 4
HD_X = 64
N_MEM = 256
EPS = 1e-6
NEG = -1e30
LOG2_E = 1.4426950408889634
F32 = jnp.float32
BF16 = jnp.bfloat16

LANES = 128
VMEM_LIMIT_BYTES = 56 * 1024 * 1024
PROJ_SPLITS = (MIX_WIDTH, MIX_WIDTH, MIX_WIDTH, X_WIDTH, D_INNER)
CONTRACT_LAST = (((1,), (1,)), ((), ()))
HGRN_BLOCKS_PER_STEP = 8
BAND_UNROLL = 4


def _params(*sem):
    return pltpu.CompilerParams(dimension_semantics=sem, vmem_limit_bytes=VMEM_LIMIT_BYTES)


def _norm_proj_body(x, g_ref, w_ref, out_refs, splits, tails, transpose):
    ms = jnp.mean(x * x, axis=-1, keepdims=True)
    xn = (x * lax.rsqrt(ms + EPS) * g_ref[...]).astype(BF16)
    tail_refs = out_refs[len(splits):]
    off = 0
    for idx, (o_ref, n) in enumerate(zip(out_refs, splits)):
        r = jnp.dot(xn, w_ref[:, off:off + n], preferred_element_type=F32)
        o_ref[...] = (r.T if transpose else r).astype(o_ref.dtype)
        if idx in tails:
            tail_refs[tails.index(idx)][...] = r
        off += n


def _norm_proj_kernel(x_ref, g_ref, w_ref, *out_refs, splits, tails):
    _norm_proj_body(x_ref[...], g_ref, w_ref, out_refs, splits, tails, False)


def _proj_out_specs(rows, tm, splits, dtypes, tails):
    out_specs = [pl.BlockSpec((tm, n), lambda i: (i, 0)) for n in splits]
    out_shape = [jax.ShapeDtypeStruct((rows, n), dt) for n, dt in zip(splits, dtypes)]
    tail_rows = min(PAST_BAND, rows)
    first_tail = (rows - tail_rows) // tm
    out_specs += [pl.BlockSpec((tm, splits[idx]), lambda i: (jnp.maximum(i - first_tail, 0), 0))
                  for idx in tails]
    out_shape += [jax.ShapeDtypeStruct((tail_rows, splits[idx]), F32) for idx in tails]
    return out_specs, out_shape


def norm_proj(x2d, g, w_all, layer, splits, dtypes, tm, tails=()):
    rows, d = x2d.shape
    n_total = w_all.shape[2]
    assert rows % tm == 0
    out_specs, out_shape = _proj_out_specs(rows, tm, splits, dtypes, tails)
    return pl.pallas_call(
        functools.partial(_norm_proj_kernel, splits=splits, tails=tuple(tails)),
        grid=(rows // tm,),
        in_specs=[
            pl.BlockSpec((tm, d), lambda i: (i, 0)),
            pl.BlockSpec((1, d), lambda i: (0, 0)),
            pl.BlockSpec((None, d, n_total), lambda i: (layer, 0, 0)),
        ],
        out_specs=out_specs,
        out_shape=out_shape,
        name=f"norm_proj_{rows}x{n_total}",
        compiler_params=_params("arbitrary"),
    )(x2d, g.reshape(1, d), w_all)


def _mem_kv_kernel(x_ref, g_ref, w_ref, kt_ref, vt_ref):
    _norm_proj_body(x_ref[...], g_ref, w_ref, (kt_ref, vt_ref), (X_WIDTH, X_WIDTH), (), True)


def mem_kv(mem, g_all, w_all):
    n_mem, d = mem.shape
    depth = w_all.shape[0]
    out = pl.BlockSpec((None, None, X_WIDTH, n_mem), lambda l: (l, 0, 0, 0))
    return pl.pallas_call(
        _mem_kv_kernel,
        grid=(depth,),
        in_specs=[pl.BlockSpec((n_mem, d), lambda l: (0, 0)),
                  pl.BlockSpec((None, 1, d), lambda l: (l, 0, 0)),
                  pl.BlockSpec((None, d, 2 * X_WIDTH), lambda l: (l, 0, 0))],
        out_specs=[out, out],
        out_shape=[jax.ShapeDtypeStruct((depth, 1, X_WIDTH, n_mem), F32)] * 2,
        name="mem_kv",
        compiler_params=_params("arbitrary"),
    )(mem, g_all.reshape(depth, 1, d), w_all)


def _stack_pair(q_pair):
    first = lax.broadcasted_iota(jnp.int32, (1, LANES), 1) < HD_A
    keep0 = jnp.where(first, 1.0, 0.0).astype(BF16)
    keep1 = jnp.where(first, 0.0, 1.0).astype(BF16)
    return jnp.concatenate([q_pair * keep0, q_pair * keep1], axis=0)


def _unstack_pair(o2):
    m = o2.shape[0] // 2
    first = lax.broadcasted_iota(jnp.int32, (m, LANES), 1) < HD_A
    return jnp.where(first, o2[:m], o2[m:])


def _pair_values(p, v_ext):
    r = jnp.dot(p, v_ext, preferred_element_type=F32)
    return _unstack_pair(r[:, :LANES] / r[:, LANES:])


def _pair_values_t(p, vt_ext):
    r = lax.dot_general(p, vt_ext, CONTRACT_LAST, preferred_element_type=F32)
    return _unstack_pair(r[:, :LANES] / r[:, LANES:])


def _band_attn_kernel(q_ref, k_ref, v_ref, bias_ref, o_ref,
                      kwin, vext, s_scr, p_scr, *, n_chunks, unroll):
    step = pl.program_id(0)
    chunk, prev_rows = CHUNK, PAST_BAND
    cur_rows = n_chunks * chunk
    band = prev_rows + chunk
    n_pairs = H_A // 2
    assert cur_rows == prev_rows

    @pl.when(step == 0)
    def _():
        kwin[...] = jnp.zeros_like(kwin)
        vext[...] = jnp.zeros_like(vext)

    kwin[0:prev_rows, :] = kwin[prev_rows:prev_rows + cur_rows, :]
    kwin[prev_rows:prev_rows + cur_rows, :] = k_ref[...]
    ones = jnp.ones((prev_rows + cur_rows, LANES), BF16)
    for hp in range(n_pairs):
        lanes = slice(LANES * hp, LANES * (hp + 1))
        vcol = slice(2 * LANES * hp, 2 * LANES * hp + LANES)
        vext[0:prev_rows, vcol] = vext[prev_rows:prev_rows + cur_rows, vcol]
        vext[prev_rows:prev_rows + cur_rows, vcol] = v_ref[:, lanes]
        vext[:, 2 * LANES * hp + LANES:2 * LANES * (hp + 1)] = ones

    def chunks_body(ci, carry):
        starts = [pl.multiple_of((ci * unroll + u) * chunk, chunk) for u in range(unroll)]
        for u, r0 in enumerate(starts):
            for hp in range(n_pairs):
                lanes = slice(LANES * hp, LANES * (hp + 1))
                q2 = _stack_pair(q_ref[pl.ds(r0, chunk), lanes] * (HD_A ** -0.5))
                s_scr[u * n_pairs + hp] = lax.dot_general(
                    q2, kwin[pl.ds(r0, band), lanes], CONTRACT_LAST, preferred_element_type=F32)
        for u, r0 in enumerate(starts):
            key_row = lax.broadcasted_iota(jnp.int32, (1, band), 1) + r0 + (step - 1) * prev_rows
            neg = jnp.where(key_row < 0, NEG, 0.0)
            for hp in range(n_pairs):
                s = s_scr[u * n_pairs + hp] + bias_ref[hp] + neg
                m = jnp.max(s, axis=-1, keepdims=True)
                p_scr[u * n_pairs + hp] = jnp.exp(s - m).astype(BF16)
        for u, r0 in enumerate(starts):
            for hp in range(n_pairs):
                o_ref[pl.ds(r0, chunk), LANES * hp:LANES * (hp + 1)] = _pair_values(
                    p_scr[u * n_pairs + hp],
                    vext[pl.ds(r0, band), 2 * LANES * hp:2 * LANES * (hp + 1)]
                ).astype(o_ref.dtype)
        return carry

    lax.fori_loop(0, n_chunks // unroll, chunks_body, 0)


def band_attn_prompt(q, k, v, bias, block_rows):
    rows = q.shape[0]
    n_chunks = block_rows // CHUNK
    n_pairs = H_A // 2
    band = PAST_BAND + CHUNK
    assert block_rows == PAST_BAND
    cur = pl.BlockSpec((block_rows, MIX_WIDTH), lambda i: (i, 0))
    return pl.pallas_call(
        functools.partial(_band_attn_kernel, n_chunks=n_chunks, unroll=BAND_UNROLL),
        grid=(rows // block_rows,),
        in_specs=[cur, cur, cur, pl.BlockSpec(bias.shape, lambda i: (0, 0, 0))],
        out_specs=cur,
        out_shape=jax.ShapeDtypeStruct((rows, MIX_WIDTH), BF16),
        scratch_shapes=[pltpu.VMEM((PAST_BAND + block_rows, MIX_WIDTH), BF16),
                        pltpu.VMEM((PAST_BAND + block_rows, 2 * MIX_WIDTH), BF16),
                        pltpu.VMEM((BAND_UNROLL * n_pairs, 2 * CHUNK, band), F32),
                        pltpu.VMEM((BAND_UNROLL * n_pairs, 2 * CHUNK, band), BF16)],
        name="band_attn_prompt",
        compiler_params=_params("arbitrary"),
    )(q, k, v, bias)


def _band_sample_kernel(q_ref, k_ref, v_ref, kct_ref, vct_ref, bias_ref, o_ref, s_scr, p_scr,
                        *, cache_len):
    n_pairs = H_A // 2
    s_len = q_ref.shape[0]
    for hp in range(n_pairs):
        lanes = slice(LANES * hp, LANES * (hp + 1))
        q2 = _stack_pair(q_ref[:, lanes] * (HD_A ** -0.5))
        kct = kct_ref[2 * hp:2 * hp + 2].reshape(LANES, cache_len).astype(BF16)
        s_scr[hp, :, 0:cache_len] = jnp.dot(q2, kct, preferred_element_type=F32)
        s_scr[hp, :, cache_len:] = lax.dot_general(q2, k_ref[:, lanes], CONTRACT_LAST,
                                                   preferred_element_type=F32)
    for hp in range(n_pairs):
        s = s_scr[hp] + bias_ref[hp]
        p_scr[hp] = jnp.exp(s - jnp.max(s, axis=-1, keepdims=True)).astype(BF16)
    ones_old = jnp.ones((LANES, cache_len), BF16)
    ones_new = jnp.ones((s_len, LANES), BF16)
    for hp in range(n_pairs):
        lanes = slice(LANES * hp, LANES * (hp + 1))
        vct = vct_ref[2 * hp:2 * hp + 2].reshape(LANES, cache_len).astype(BF16)
        r = (lax.dot_general(p_scr[hp, :, 0:cache_len], jnp.concatenate([vct, ones_old], axis=0),
                             CONTRACT_LAST, preferred_element_type=F32)
             + jnp.dot(p_scr[hp, :, cache_len:], jnp.concatenate([v_ref[:, lanes], ones_new], axis=1),
                       preferred_element_type=F32))
        o_ref[:, lanes] = _unstack_pair(r[:, :LANES] / r[:, LANES:]).astype(o_ref.dtype)


def band_attn_sample(q, k, v, k_cache_t, v_cache_t, layer, bias):
    b, s_len, _ = q.shape
    cache_len = k_cache_t.shape[-1]
    new = pl.BlockSpec((None, s_len, MIX_WIDTH), lambda i: (i, 0, 0))
    old = pl.BlockSpec((None, None, H_A, HD_A, cache_len), lambda i: (layer, i, 0, 0, 0))
    return pl.pallas_call(
        functools.partial(_band_sample_kernel, cache_len=cache_len),
        grid=(b,),
        in_specs=[new, new, new, old, old, pl.BlockSpec(bias.shape, lambda i: (0, 0, 0))],
        out_specs=new,
        out_shape=jax.ShapeDtypeStruct((b, s_len, MIX_WIDTH), BF16),
        scratch_shapes=[pltpu.VMEM((H_A // 2, 2 * s_len, cache_len + s_len), F32),
                        pltpu.VMEM((H_A // 2, 2 * s_len, cache_len + s_len), BF16)],
        name="band_attn_sample",
        compiler_params=_params("arbitrary"),
    )(q, k, v, k_cache_t, v_cache_t, bias)


def _hgrn2_constants(t):
    halves = []
    h = t // 2
    while h >= 1:
        halves.append(h)
        h //= 2
    n_lvl = len(halves)
    w = np.zeros((n_lvl * t, t), np.float32)
    masks = np.zeros((n_lvl + 1, t, t), np.float32)
    for row in range(t):
        w[row, :row + 1] = 1.0
    for li, h in enumerate(halves):
        base = (1 + li) * t
        for row in range(t):
            r = (row // (2 * h)) * 2 * h + h - 1
            if h == 1:
                continue
            if row > r:
                w[base + row, r + 1:row + 1] = 1.0
            else:
                w[base + row, row + 1:r + 1] = 1.0
        for tq in range(t):
            for sk in range(t):
                if tq // (2 * h) == sk // (2 * h):
                    r = (tq // (2 * h)) * 2 * h + h - 1
                    if tq > r and sk <= r:
                        masks[li, tq, sk] = 1.0
    masks[n_lvl] = np.eye(t, dtype=np.float32)
    return jnp.asarray(np.tile(w, (1, 2)), BF16), jnp.asarray(masks, F32), n_lvl


def _hgrn2_body(a_ref, b_ref, c_ref, s0_ref, lbc_ref, g_ref, w_ref, m_ref, o_ref, sout_ref,
                st_ref, lf_scr, kk_scr, l2f_scr, sums_scr, qs_scr, ks_scr, attn_scr, u_scr, oi_scr,
                *, t, n_lvl, n_blk, ci, n_steps):
    inter = n_lvl + 1

    @pl.when(ci == 0)
    def _():
        for h in range(H_B):
            st_ref[h] = s0_ref[h].T

    def blk(g):
        return slice(g * t, (g + 1) * t)

    def head(h):
        return slice(LANES * h, LANES * (h + 1))

    z = b_ref[...]
    e_neg = jnp.exp2(jnp.abs(z) * (-LOG2_E))
    u = lbc_ref[1:2, :] + jnp.minimum(z, 0.0) * LOG2_E - jnp.log2(1.0 + e_neg)
    log2_lb = lbc_ref[0:1, :]
    log2_f = jnp.maximum(log2_lb, u) + jnp.log2(1.0 + jnp.exp2(-jnp.abs(log2_lb - u)))
    kk_scr[...] = lbc_ref[2:3, :] * jnp.where(z > 0.0, e_neg, 1.0) / (1.0 + e_neg)
    l2f_scr[...] = log2_f
    hi = log2_f.astype(BF16)
    lo = (log2_f - hi.astype(F32)).astype(BF16)
    for g in range(n_blk):
        lf_scr[g, 0:t] = hi[blk(g)]
        lf_scr[g, t:2 * t] = lo[blk(g)]

    for g in range(n_blk):
        sums_scr[g] = jnp.dot(w_ref[...], lf_scr[g], preferred_element_type=F32)

    odd_row = (lax.broadcasted_iota(jnp.int32, (t, 1), 0) & 1) == 1
    for g in range(n_blk):
        q = a_ref[blk(g), :].astype(F32)
        kk = kk_scr[blk(g), :]

        def put(slot, e):
            qs_scr[g, slot] = (q * e).astype(BF16)
            ks_scr[g, slot] = (kk * e).astype(BF16)

        for li in range(n_lvl - 1):
            put(li, jnp.exp2(sums_scr[g, (1 + li) * t:(2 + li) * t]))
        put(n_lvl - 1, jnp.exp2(jnp.where(odd_row, l2f_scr[blk(g), :], 0.0)))
        qs_scr[g, n_lvl] = a_ref[blk(g), :]
        ks_scr[g, n_lvl] = kk.astype(BF16)
        a_cum = sums_scr[g, 0:t]
        qs_scr[g, inter] = (q * jnp.exp2(a_cum)).astype(BF16)
        ks_scr[g, inter] = (kk * jnp.exp2(a_cum[t - 1:t] - a_cum)).astype(BF16)

    in_level = [m_ref[li] != 0.0 for li in range(n_lvl + 1)]
    for g in range(n_blk):
        for h in range(H_B):
            acc = jnp.zeros((t, t), F32)
            for li in range(n_lvl + 1):
                part = lax.dot_general(
                    qs_scr[g, li, :, head(h)], ks_scr[g, li, :, head(h)],
                    CONTRACT_LAST, preferred_element_type=F32)
                acc = jnp.where(in_level[li], part, acc)
            attn_scr[g, h] = acc.astype(BF16)

    for g in range(n_blk):
        for h in range(H_B):
            iv = c_ref[blk(g), head(h)]
            oi_scr[blk(g), head(h)] = jnp.dot(attn_scr[g, h], iv, preferred_element_type=F32)
            u_scr[g, h] = jnp.dot(iv.astype(F32).T.astype(BF16), ks_scr[g, inter, :, head(h)],
                                  preferred_element_type=F32)

    for g in range(n_blk):
        e_last = jnp.exp2(sums_scr[g, t - 1:t, :])
        for h in range(H_B):
            st = st_ref[h]
            o = oi_scr[blk(g), head(h)] + lax.dot_general(
                qs_scr[g, inter, :, head(h)], st.astype(BF16), CONTRACT_LAST,
                preferred_element_type=F32)
            st_ref[h] = e_last[:, head(h)] * st + u_scr[g, h]
            ms = jnp.mean(o * o, axis=-1, keepdims=True)
            o_ref[blk(g), head(h)] = (o * lax.rsqrt(ms + EPS) * g_ref[:, head(h)]).astype(o_ref.dtype)

    @pl.when(ci == n_steps - 1)
    def _():
        for h in range(H_B):
            sout_ref[h] = st_ref[h].T


def _hgrn2_kernel(*refs, t, n_lvl, n_blk):
    _hgrn2_body(*refs, t=t, n_lvl=n_lvl, n_blk=n_blk, ci=pl.program_id(1), n_steps=pl.num_programs(1))


def _hgrn2_scratch(t, n_lvl, n_blk):
    step_rows = n_blk * t
    return [pltpu.VMEM((H_B, DK_B, DK_B), F32),
            pltpu.VMEM((n_blk, 2 * t, MIX_WIDTH), BF16),
            pltpu.VMEM((step_rows, MIX_WIDTH), F32),
            pltpu.VMEM((step_rows, MIX_WIDTH), F32),
            pltpu.VMEM((n_blk, n_lvl * t, MIX_WIDTH), F32),
            pltpu.VMEM((n_blk, n_lvl + 2, t, MIX_WIDTH), BF16),
            pltpu.VMEM((n_blk, n_lvl + 2, t, MIX_WIDTH), BF16),
            pltpu.VMEM((n_blk, H_B, t, t), BF16),
            pltpu.VMEM((n_blk, H_B, DK_B, DK_B), F32),
            pltpu.VMEM((step_rows, MIX_WIDTH), F32)]


def hgrn2(a, b, c, s0, layer, lbc, g, t, n_blk):
    bsz, rows, _ = a.shape
    w, masks, n_lvl = _hgrn2_constants(t)
    step_rows = n_blk * t
    tok = pl.BlockSpec((None, step_rows, MIX_WIDTH), lambda bi, ci: (bi, ci, 0))
    state = pl.BlockSpec((None, H_B, DK_B, DK_B), lambda bi, ci: (bi, 0, 0, 0))
    state_in = pl.BlockSpec((None, None, H_B, DK_B, DK_B), lambda bi, ci: (layer, bi, 0, 0, 0))
    return pl.pallas_call(
        functools.partial(_hgrn2_kernel, t=t, n_lvl=n_lvl, n_blk=n_blk),
        grid=(bsz, rows // step_rows),
        in_specs=[tok, tok, tok, state_in,
                  pl.BlockSpec(lbc.shape, lambda bi, ci: (0, 0)),
                  pl.BlockSpec((1, MIX_WIDTH), lambda bi, ci: (0, 0)),
                  pl.BlockSpec(w.shape, lambda bi, ci: (0, 0)),
                  pl.BlockSpec(masks.shape, lambda bi, ci: (0, 0, 0))],
        out_specs=[tok, state],
        out_shape=[jax.ShapeDtypeStruct((bsz, rows, MIX_WIDTH), BF16),
                   jax.ShapeDtypeStruct((bsz, H_B, DK_B, DK_B), F32)],
        scratch_shapes=_hgrn2_scratch(t, n_lvl, n_blk),
        name=f"hgrn2_t{t}",
        compiler_params=_params("arbitrary", "arbitrary"),
    )(a, b, c, s0, lbc, g.reshape(1, MIX_WIDTH), w, masks)


def _post_body(x_ref, mix_ref, xq_ref, gate_ref, mkt_ref, mvt_ref, w_ref, s_scr, p_scr, cross_scr,
               n_seg):
    seg_rows = x_ref.shape[0] // n_seg
    n_pairs = H_X // 2
    for seg in range(n_seg):
        rows = slice(seg * seg_rows, (seg + 1) * seg_rows)
        for hp in range(n_pairs):
            lanes = slice(LANES * hp, LANES * (hp + 1))
            q2 = _stack_pair(xq_ref[rows, lanes] * (HD_X ** -0.5))
            s_scr[seg * n_pairs + hp] = jnp.dot(q2, mkt_ref[seg, lanes, :].astype(BF16),
                                                preferred_element_type=F32)
    for i in range(n_seg * n_pairs):
        s = s_scr[i]
        p_scr[i] = jnp.exp(s - jnp.max(s, axis=-1, keepdims=True)).astype(BF16)
    ones = jnp.ones((LANES, N_MEM), BF16)
    for seg in range(n_seg):
        rows = slice(seg * seg_rows, (seg + 1) * seg_rows)
        for hp in range(n_pairs):
            lanes = slice(LANES * hp, LANES * (hp + 1))
            cross_scr[rows, lanes] = _pair_values_t(
                p_scr[seg * n_pairs + hp],
                jnp.concatenate([mvt_ref[seg, lanes, :].astype(BF16), ones], axis=0))

    gate = gate_ref[...]
    sg = gate / (1.0 + jnp.exp(-gate))
    y_mix = (mix_ref[...].astype(F32) * sg[:, 0:MIX_WIDTH]).astype(BF16)
    y_cross = (cross_scr[...] * sg[:, MIX_WIDTH:]).astype(BF16)
    return (x_ref[...] + jnp.dot(y_mix, w_ref[0:MIX_WIDTH, :], preferred_element_type=F32)
            + jnp.dot(y_cross, w_ref[MIX_WIDTH:, :], preferred_element_type=F32))


def _post_final_kernel(x_ref, mix_ref, xq_ref, gate_ref, mkt_ref, mvt_ref, w_ref, fg_ref, o_ref,
                       s_scr, p_scr, cross_scr, *, n_seg):
    acc = _post_body(x_ref, mix_ref, xq_ref, gate_ref, mkt_ref, mvt_ref, w_ref,
                     s_scr, p_scr, cross_scr, n_seg)
    ms = jnp.mean(acc * acc, axis=-1, keepdims=True)
    o_ref[...] = acc * lax.rsqrt(ms + EPS) * fg_ref[...]


def _post_pre_kernel(x_ref, mix_ref, xq_ref, gate_ref, mkt_ref, mvt_ref, wo_ref, g_ref, wi_ref,
                     xo_ref, *rest, n_seg, splits, tails):
    n_out = len(splits) + len(tails)
    s_scr, p_scr, cross_scr = rest[n_out:]
    acc = _post_body(x_ref, mix_ref, xq_ref, gate_ref, mkt_ref, mvt_ref, wo_ref,
                     s_scr, p_scr, cross_scr, n_seg)
    xo_ref[...] = acc
    _norm_proj_body(acc, g_ref, wi_ref, rest[:n_out], splits, tails, False)


def _post_specs(tm, n_seg, mem_layer, w_layer):
    def tok(n):
        return pl.BlockSpec((tm, n), lambda i: (i, 0))

    mem = pl.BlockSpec((None, n_seg, X_WIDTH, N_MEM), lambda i: (mem_layer, 0, 0, 0))
    w_out = pl.BlockSpec((None, D_INNER, D_MODEL), lambda i: (w_layer, 0, 0),
                         pipeline_mode=pl.Buffered(1))
    in_specs = [tok(D_MODEL), tok(MIX_WIDTH), tok(X_WIDTH), tok(D_INNER), mem, mem, w_out]
    seg_rows = tm // n_seg
    n_items = n_seg * (H_X // 2)
    scratch = [pltpu.VMEM((n_items, 2 * seg_rows, N_MEM), F32),
               pltpu.VMEM((n_items, 2 * seg_rows, N_MEM), BF16),
               pltpu.VMEM((tm, X_WIDTH), F32)]
    return in_specs, scratch, tok(D_MODEL)


def post_final(x, mix, xq, gate, mkt, mvt, mem_layer, w_all, w_layer, final_g, tm, n_seg):
    rows = x.shape[0]
    in_specs, scratch, x_spec = _post_specs(tm, n_seg, mem_layer, w_layer)
    return pl.pallas_call(
        functools.partial(_post_final_kernel, n_seg=n_seg),
        grid=(rows // tm,),
        in_specs=in_specs + [pl.BlockSpec((1, D_MODEL), lambda i: (0, 0))],
        out_specs=x_spec,
        out_shape=jax.ShapeDtypeStruct((rows, D_MODEL), F32),
        scratch_shapes=scratch,
        name=f"post_final_{rows}",
        compiler_params=_params("arbitrary"),
    )(x, mix, xq, gate, mkt, mvt, w_all, final_g.reshape(1, D_MODEL))


def post_pre(x, mix, xq, gate, mkt, mvt, mem_layer, w_out_all, layer, g_next, w_in_all,
             dtypes, tails, tm, n_seg):
    rows = x.shape[0]
    in_specs, scratch, x_spec = _post_specs(tm, n_seg, mem_layer, layer)
    in_specs += [pl.BlockSpec((1, D_MODEL), lambda i: (0, 0)),
                 pl.BlockSpec((None, D_MODEL, w_in_all.shape[2]), lambda i: (layer + 1, 0, 0),
                              pipeline_mode=pl.Buffered(1))]
    proj_specs, proj_shapes = _proj_out_specs(rows, tm, PROJ_SPLITS, dtypes, tails)
    return pl.pallas_call(
        functools.partial(_post_pre_kernel, n_seg=n_seg, splits=PROJ_SPLITS, tails=tuple(tails)),
        grid=(rows // tm,),
        in_specs=in_specs,
        out_specs=[x_spec] + proj_specs,
        out_shape=[jax.ShapeDtypeStruct((rows, D_MODEL), F32)] + proj_shapes,
        scratch_shapes=scratch,
        name=f"post_pre_{rows}",
        compiler_params=_params("arbitrary"),
    )(x, mix, xq, gate, mkt, mvt, w_out_all, g_next.reshape(1, D_MODEL), w_in_all)


def _rel_bias(table):
    band = PAST_BAND + CHUNK
    n_diag = band + CHUNK - 1
    offs = np.arange(n_diag) - (CHUNK - 1)
    idx = np.clip(PAST_BAND - offs, -REL_CLIP, REL_CLIP) + REL_CLIP
    diag = jnp.pad(table[:, idx].astype(F32), ((0, 0), (0, 1)))
    skew = jnp.tile(diag, (1, CHUNK))[:, :CHUNK * n_diag].reshape(H_A, CHUNK, n_diag)
    return skew[:, :, CHUNK - 1:CHUNK - 1 + band]


def _per_head_transposed(cache):
    return jnp.moveaxis(cache, -3, -1)


def kernel(x_prompt, x_sample, cache_a_k, cache_a_v, state_b, cache_mem_k, cache_mem_v, mem_prompt,
           ln_g, w_in, w_out, rel_bias_table, lower_bounds, hgrn_norm_g, mem_norm_g, w_mem_kv, final_g):
    bp, seq, _ = x_prompt.shape
    bs, dec_seq, _ = x_sample.shape
    assert bp == 1
    cache_len = cache_a_k.shape[2]
    n_s = bs * dec_seq
    keep = min(PAST_BAND, seq)

    w_in_b = w_in.astype(BF16)
    w_out_b = w_out.astype(BF16)
    w_mem_b = w_mem_kv.astype(BF16)

    lb_all = jnp.cumsum(jax.nn.softmax(lower_bounds.astype(F32), axis=0), axis=0)
    lb_all = lb_all - lb_all[:1]

    cache_a_kt = _per_head_transposed(cache_a_k)
    cache_a_vt = _per_head_transposed(cache_a_v)
    cache_mem_kt = _per_head_transposed(cache_mem_k).reshape(DEPTH, bs, X_WIDTH, N_MEM)
    cache_mem_vt = _per_head_transposed(cache_mem_v).reshape(DEPTH, bs, X_WIDTH, N_MEM)
    zero_state = jnp.zeros((1, 1, H_B, DK_B, DK_B), F32)

    def per_batch(u):
        return u.reshape(bs, dec_seq, u.shape[-1])

    def layer_io(l):
        attn_layer = l % 2 == 0
        return (BF16, BF16 if attn_layer else F32, BF16, BF16, F32), (1, 2) if attn_layer else ()

    mkt_all, mvt_all = mem_kv(mem_prompt.reshape(N_MEM, D_MODEL), mem_norm_g, w_mem_b)
    mem_k_prompt = jnp.moveaxis(mkt_all.reshape(DEPTH, 1, H_X, HD_X, N_MEM), -1, 2)
    mem_v_prompt = jnp.moveaxis(mvt_all.reshape(DEPTH, 1, H_X, HD_X, N_MEM), -1, 2)

    xp = x_prompt.reshape(seq, D_MODEL)
    xs = x_sample.reshape(n_s, D_MODEL)
    dtypes, tails = layer_io(0)
    outs_p = norm_proj(xp, ln_g[0], w_in_b, 0, PROJ_SPLITS, dtypes, PAST_BAND, tails)
    outs_s = norm_proj(xs, ln_g[0], w_in_b, 0, PROJ_SPLITS, dtypes, n_s, tails)
    ak_p, av_p, sb_p, ak_s, av_s, sb_s = [], [], [], [], [], []
    for l in range(DEPTH):
        j = l // 2
        a_p, b_p, c_p, xq_p, g_p = outs_p[:5]
        a_s, b_s, c_s, xq_s, g_s = outs_s[:5]

        if l % 2 == 0:
            bias = _rel_bias(rel_bias_table[j])
            band = PAST_BAND + CHUNK
            o_p = band_attn_prompt(a_p, b_p, c_p, bias.reshape(H_A // 2, 2 * CHUNK, band), PAST_BAND)
            o_s = band_attn_sample(per_batch(a_s), per_batch(b_s), per_batch(c_s),
                                   cache_a_kt, cache_a_vt, j,
                                   bias[:, :dec_seq, :cache_len + dec_seq].reshape(
                                       H_A // 2, 2 * dec_seq, cache_len + dec_seq))
            assert keep == PAST_BAND
            ak_p.append(outs_p[5].reshape(1, keep, H_A, HD_A))
            av_p.append(outs_p[6].reshape(1, keep, H_A, HD_A))
            ak_s.append(outs_s[5].reshape(bs, dec_seq, H_A, HD_A))
            av_s.append(outs_s[6].reshape(bs, dec_seq, H_A, HD_A))
        else:
            lb = lb_all[j]
            lbc = jnp.stack([jnp.log2(lb), jnp.log1p(-lb) * LOG2_E, 1.0 - lb])
            o_p, s_p = hgrn2(a_p.reshape(1, seq, MIX_WIDTH), b_p.reshape(1, seq, MIX_WIDTH),
                             c_p.reshape(1, seq, MIX_WIDTH), zero_state, 0, lbc, hgrn_norm_g[j],
                             CHUNK, HGRN_BLOCKS_PER_STEP)
            o_s, s_s = hgrn2(per_batch(a_s), per_batch(b_s), per_batch(c_s),
                             state_b.astype(F32), j, lbc, hgrn_norm_g[j], dec_seq, 1)
            sb_p.append(s_p)
            sb_s.append(s_s)
        mem_p = (mkt_all, mvt_all, l)
        mem_s = (cache_mem_kt, cache_mem_vt, l)
        o_p = o_p.reshape(seq, MIX_WIDTH)
        o_s = o_s.reshape(n_s, MIX_WIDTH)
        if l == DEPTH - 1:
            xp = post_final(xp, o_p, xq_p, g_p, *mem_p, w_out_b, l, final_g, PAST_BAND, 1)
            xs = post_final(xs, o_s, xq_s, g_s, *mem_s, w_out_b, l, final_g, n_s, bs)
        else:
            dtypes, tails = layer_io(l + 1)
            xp, *outs_p = post_pre(xp, o_p, xq_p, g_p, *mem_p, w_out_b, l, ln_g[l + 1], w_in_b,
                                   dtypes, tails, PAST_BAND, 1)
            xs, *outs_s = post_pre(xs, o_s, xq_s, g_s, *mem_s, w_out_b, l, ln_g[l + 1], w_in_b,
                                   dtypes, tails, n_s, bs)
    return (xp.reshape(1, seq, D_MODEL), xs.reshape(bs, dec_seq, D_MODEL),
            jnp.stack(ak_p), jnp.stack(av_p), jnp.stack(sb_p), mem_k_prompt, mem_v_prompt,
            jnp.stack(ak_s), jnp.stack(av_s), jnp.stack(sb_s))
```

```python
import functools

import numpy as np
import jax
import jax.numpy as jnp
from jax import lax
from jax.experimental import pallas as pl
from jax.experimental.pallas import tpu as pltpu

D_MODEL = 1024
DEPTH = 4
CHUNK = 64
N_PAST_CHUNKS = 8
PAST_BAND = N_PAST_CHUNKS * CHUNK
MIX_WIDTH = 768
X_WIDTH = 256
D_INNER = MIX_WIDTH + X_WIDTH
HD_A = 64
H_A = MIX_WIDTH // HD_A
REL_CLIP = 128
DK_B = 128
H_B = MIX_WIDTH // DK_B
H_X = 4
HD_X = 64
N_MEM = 256
EPS = 1e-6
NEG = -1e30
LOG2_E = 1.4426950408889634
F32 = jnp.float32
BF16 = jnp.bfloat16

LANES = 128
VMEM_LIMIT_BYTES = 56 * 1024 * 1024
PROJ_SPLITS = (MIX_WIDTH, MIX_WIDTH, MIX_WIDTH, X_WIDTH, D_INNER)
CONTRACT_LAST = (((1,), (1,)), ((), ()))
HGRN_BLOCKS_PER_STEP = 8
HGRN_SEQS_PER_STEP = 4
BAND_UNROLL = 4


def _params(*sem):
    return pltpu.CompilerParams(dimension_semantics=sem, vmem_limit_bytes=VMEM_LIMIT_BYTES)


def _norm_proj_body(x, g_ref, w_ref, out_refs, splits, tails, transpose):
    ms = jnp.mean(x * x, axis=-1, keepdims=True)
    xn = (x * lax.rsqrt(ms + EPS) * g_ref[...]).astype(BF16)
    tail_refs = out_refs[len(splits):]
    off = 0
    for idx, (o_ref, n) in enumerate(zip(out_refs, splits)):
        r = jnp.dot(xn, w_ref[:, off:off + n], preferred_element_type=F32)
        o_ref[...] = (r.T if transpose else r).astype(o_ref.dtype)
        if idx in tails:
            tail_refs[tails.index(idx)][...] = r
        off += n


def _norm_proj_kernel(x_ref, g_ref, w_ref, *out_refs, splits, tails):
    _norm_proj_body(x_ref[...], g_ref, w_ref, out_refs, splits, tails, False)


def _proj_out_specs(rows, tm, splits, dtypes, tails):
    out_specs = [pl.BlockSpec((tm, n), lambda i: (i, 0)) for n in splits]
    out_shape = [jax.ShapeDtypeStruct((rows, n), dt) for n, dt in zip(splits, dtypes)]
    tail_rows = min(PAST_BAND, rows)
    first_tail = (rows - tail_rows) // tm
    out_specs += [pl.BlockSpec((tm, splits[idx]), lambda i: (jnp.maximum(i - first_tail, 0), 0))
                  for idx in tails]
    out_shape += [jax.ShapeDtypeStruct((tail_rows, splits[idx]), F32) for idx in tails]
    return out_specs, out_shape


def norm_proj(x2d, g, w_all, layer, splits, dtypes, tm, tails=()):
    rows, d = x2d.shape
    n_total = w_all.shape[2]
    assert rows % tm == 0
    out_specs, out_shape = _proj_out_specs(rows, tm, splits, dtypes, tails)
    return pl.pallas_call(
        functools.partial(_norm_proj_kernel, splits=splits, tails=tuple(tails)),
        grid=(rows // tm,),
        in_specs=[
            pl.BlockSpec((tm, d), lambda i: (i, 0)),
            pl.BlockSpec((1, d), lambda i: (0, 0)),
            pl.BlockSpec((None, d, n_total), lambda i: (layer, 0, 0)),
        ],
        out_specs=out_specs,
        out_shape=out_shape,
        name=f"norm_proj_{rows}x{n_total}",
        compiler_params=_params("arbitrary"),
    )(x2d, g.reshape(1, d), w_all)


def _mem_kv_kernel(x_ref, g_ref, w_ref, kt_ref, vt_ref):
    _norm_proj_body(x_ref[...], g_ref, w_ref, (kt_ref, vt_ref), (X_WIDTH, X_WIDTH), (), True)


def mem_kv(mem, g_all, w_all):
    n_mem, d = mem.shape
    depth = w_all.shape[0]
    out = pl.BlockSpec((None, None, X_WIDTH, n_mem), lambda l: (l, 0, 0, 0))
    return pl.pallas_call(
        _mem_kv_kernel,
        grid=(depth,),
        in_specs=[pl.BlockSpec((n_mem, d), lambda l: (0, 0)),
                  pl.BlockSpec((None, 1, d), lambda l: (l, 0, 0)),
                  pl.BlockSpec((None, d, 2 * X_WIDTH), lambda l: (l, 0, 0))],
        out_specs=[out, out],
        out_shape=[jax.ShapeDtypeStruct((depth, 1, X_WIDTH, n_mem), F32)] * 2,
        name="mem_kv",
        compiler_params=_params("arbitrary"),
    )(mem, g_all.reshape(depth, 1, d), w_all)


def _stack_pair(q_pair):
    first = lax.broadcasted_iota(jnp.int32, (1, LANES), 1) < HD_A
    keep0 = jnp.where(first, 1.0, 0.0).astype(BF16)
    keep1 = jnp.where(first, 0.0, 1.0).astype(BF16)
    return jnp.concatenate([q_pair * keep0, q_pair * keep1], axis=0)


def _unstack_pair(o2):
    m = o2.shape[0] // 2
    first = lax.broadcasted_iota(jnp.int32, (m, LANES), 1) < HD_A
    return jnp.where(first, o2[:m], o2[m:])


def _pair_values(p, v_ext):
    r = jnp.dot(p, v_ext, preferred_element_type=F32)
    return _unstack_pair(r[:, :LANES] / r[:, LANES:])


def _pair_values_t(p, vt_ext):
    r = lax.dot_general(p, vt_ext, CONTRACT_LAST, preferred_element_type=F32)
    return _unstack_pair(r[:, :LANES] / r[:, LANES:])


def _band_attn_kernel(q_ref, k_ref, v_ref, bias_ref, o_ref,
                      kwin, vext, s_scr, p_scr, *, n_chunks, unroll):
    step = pl.program_id(0)
    chunk, prev_rows = CHUNK, PAST_BAND
    cur_rows = n_chunks * chunk
    band = prev_rows + chunk
    n_pairs = H_A // 2
    assert cur_rows == prev_rows

    @pl.when(step == 0)
    def _():
        kwin[...] = jnp.zeros_like(kwin)
        vext[...] = jnp.zeros_like(vext)

    kwin[0:prev_rows, :] = kwin[prev_rows:prev_rows + cur_rows, :]
    kwin[prev_rows:prev_rows + cur_rows, :] = k_ref[...]
    ones = jnp.ones((prev_rows + cur_rows, LANES), BF16)
    for hp in range(n_pairs):
        lanes = slice(LANES * hp, LANES * (hp + 1))
        vcol = slice(2 * LANES * hp, 2 * LANES * hp + LANES)
        vext[0:prev_rows, vcol] = vext[prev_rows:prev_rows + cur_rows, vcol]
        vext[prev_rows:prev_rows + cur_rows, vcol] = v_ref[:, lanes]
        vext[:, 2 * LANES * hp + LANES:2 * LANES * (hp + 1)] = ones

    def chunks_body(ci, carry):
        starts = [pl.multiple_of((ci * unroll + u) * chunk, chunk) for u in range(unroll)]
        for u, r0 in enumerate(starts):
            for hp in range(n_pairs):
                lanes = slice(LANES * hp, LANES * (hp + 1))
                q2 = _stack_pair(q_ref[pl.ds(r0, chunk), lanes] * (HD_A ** -0.5))
                s_scr[u * n_pairs + hp] = lax.dot_general(
                    q2, kwin[pl.ds(r0, band), lanes], CONTRACT_LAST, preferred_element_type=F32)
        for u, r0 in enumerate(starts):
            key_row = lax.broadcasted_iota(jnp.int32, (1, band), 1) + r0 + (step - 1) * prev_rows
            neg = jnp.where(key_row < 0, NEG, 0.0)
            for hp in range(n_pairs):
                s = s_scr[u * n_pairs + hp] + bias_ref[hp] + neg
                m = jnp.max(s, axis=-1, keepdims=True)
                p_scr[u * n_pairs + hp] = jnp.exp(s - m).astype(BF16)
        for u, r0 in enumerate(starts):
            for hp in range(n_pairs):
                o_ref[pl.ds(r0, chunk), LANES * hp:LANES * (hp + 1)] = _pair_values(
                    p_scr[u * n_pairs + hp],
                    vext[pl.ds(r0, band), 2 * LANES * hp:2 * LANES * (hp + 1)]
                ).astype(o_ref.dtype)
        return carry

    lax.fori_loop(0, n_chunks // unroll, chunks_body, 0)


def band_attn_prompt(q, k, v, bias, block_rows):
    rows = q.shape[0]
    n_chunks = block_rows // CHUNK
    n_pairs = H_A // 2
    band = PAST_BAND + CHUNK
    assert block_rows == PAST_BAND
    cur = pl.BlockSpec((block_rows, MIX_WIDTH), lambda i: (i, 0))
    return pl.pallas_call(
        functools.partial(_band_attn_kernel, n_chunks=n_chunks, unroll=BAND_UNROLL),
        grid=(rows // block_rows,),
        in_specs=[cur, cur, cur, pl.BlockSpec(bias.shape, lambda i: (0, 0, 0))],
        out_specs=cur,
        out_shape=jax.ShapeDtypeStruct((rows, MIX_WIDTH), BF16),
        scratch_shapes=[pltpu.VMEM((PAST_BAND + block_rows, MIX_WIDTH), BF16),
                        pltpu.VMEM((PAST_BAND + block_rows, 2 * MIX_WIDTH), BF16),
                        pltpu.VMEM((BAND_UNROLL * n_pairs, 2 * CHUNK, band), F32),
                        pltpu.VMEM((BAND_UNROLL * n_pairs, 2 * CHUNK, band), BF16)],
        name="band_attn_prompt",
        compiler_params=_params("arbitrary"),
    )(q, k, v, bias)


def _band_sample_kernel(q_ref, k_ref, v_ref, kct_ref, vct_ref, bias_ref, o_ref, s_scr, p_scr,
                        *, cache_len):
    n_pairs = H_A // 2
    s_len = q_ref.shape[0]
    for hp in range(n_pairs):
        lanes = slice(LANES * hp, LANES * (hp + 1))
        q2 = _stack_pair(q_ref[:, lanes] * (HD_A ** -0.5))
        kct = kct_ref[2 * hp:2 * hp + 2].reshape(LANES, cache_len).astype(BF16)
        s_scr[hp, :, 0:cache_len] = jnp.dot(q2, kct, preferred_element_type=F32)
        s_scr[hp, :, cache_len:] = lax.dot_general(q2, k_ref[:, lanes], CONTRACT_LAST,
                                                   preferred_element_type=F32)
    for hp in range(n_pairs):
        s = s_scr[hp] + bias_ref[hp]
        p_scr[hp] = jnp.exp(s - jnp.max(s, axis=-1, keepdims=True)).astype(BF16)
    ones_old = jnp.ones((LANES, cache_len), BF16)
    ones_new = jnp.ones((s_len, LANES), BF16)
    for hp in range(n_pairs):
        lanes = slice(LANES * hp, LANES * (hp + 1))
        vct = vct_ref[2 * hp:2 * hp + 2].reshape(LANES, cache_len).astype(BF16)
        r = (lax.dot_general(p_scr[hp, :, 0:cache_len], jnp.concatenate([vct, ones_old], axis=0),
                             CONTRACT_LAST, preferred_element_type=F32)
             + jnp.dot(p_scr[hp, :, cache_len:], jnp.concatenate([v_ref[:, lanes], ones_new], axis=1),
                       preferred_element_type=F32))
        o_ref[:, lanes] = _unstack_pair(r[:, :LANES] / r[:, LANES:]).astype(o_ref.dtype)


def band_attn_sample(q, k, v, k_cache_t, v_cache_t, layer, bias):
    b, s_len, _ = q.shape
    cache_len = k_cache_t.shape[-1]
    new = pl.BlockSpec((None, s_len, MIX_WIDTH), lambda i: (i, 0, 0))
    old = pl.BlockSpec((None, None, H_A, HD_A, cache_len), lambda i: (layer, i, 0, 0, 0))
    return pl.pallas_call(
        functools.partial(_band_sample_kernel, cache_len=cache_len),
        grid=(b,),
        in_specs=[new, new, new, old, old, pl.BlockSpec(bias.shape, lambda i: (0, 0, 0))],
        out_specs=new,
        out_shape=jax.ShapeDtypeStruct((b, s_len, MIX_WIDTH), BF16),
        scratch_shapes=[pltpu.VMEM((H_A // 2, 2 * s_len, cache_len + s_len), F32),
                        pltpu.VMEM((H_A // 2, 2 * s_len, cache_len + s_len), BF16)],
        name="band_attn_sample",
        compiler_params=_params("arbitrary"),
    )(q, k, v, k_cache_t, v_cache_t, bias)


def _hgrn2_constants(t):
    halves = []
    h = t // 2
    while h >= 1:
        halves.append(h)
        h //= 2
    n_lvl = len(halves)
    w = np.zeros((n_lvl * t, t), np.float32)
    masks = np.zeros((n_lvl + 1, t, t), np.float32)
    for row in range(t):
        w[row, :row + 1] = 1.0
    for li, h in enumerate(halves):
        base = (1 + li) * t
        for row in range(t):
            r = (row // (2 * h)) * 2 * h + h - 1
            if h == 1:
                continue
            if row > r:
                w[base + row, r + 1:row + 1] = 1.0
            else:
                w[base + row, row + 1:r + 1] = 1.0
        for tq in range(t):
            for sk in range(t):
                if tq // (2 * h) == sk // (2 * h):
                    r = (tq // (2 * h)) * 2 * h + h - 1
                    if tq > r and sk <= r:
                        masks[li, tq, sk] = 1.0
    masks[n_lvl] = np.eye(t, dtype=np.float32)
    return jnp.asarray(np.tile(w, (1, 2)), BF16), jnp.asarray(masks, F32), n_lvl


def _hgrn2_kernel(a_ref, b_ref, c_ref, s0_ref, lbc_ref, g_ref, w_ref, m_ref, o_ref, sout_ref,
                st_ref, lf_scr, kk_scr, l2f_scr, sums_scr, qs_scr, ks_scr, attn_scr, u_scr, oi_scr,
                *, t, n_lvl, n_blk, chained):
    inter = n_lvl + 1
    ci = pl.program_id(0)

    if chained:
        @pl.when(ci == 0)
        def _():
            for h in range(H_B):
                st_ref[h] = s0_ref[h].T

    def blk(g):
        return slice(g * t, (g + 1) * t)

    def head(h):
        return slice(LANES * h, LANES * (h + 1))

    z = b_ref[...]
    e_neg = jnp.exp2(jnp.abs(z) * (-LOG2_E))
    u = lbc_ref[1:2, :] + jnp.minimum(z, 0.0) * LOG2_E - jnp.log2(1.0 + e_neg)
    log2_lb = lbc_ref[0:1, :]
    log2_f = jnp.maximum(log2_lb, u) + jnp.log2(1.0 + jnp.exp2(-jnp.abs(log2_lb - u)))
    kk_scr[...] = lbc_ref[2:3, :] * jnp.where(z > 0.0, e_neg, 1.0) / (1.0 + e_neg)
    l2f_scr[...] = log2_f
    hi = log2_f.astype(BF16)
    lo = (log2_f - hi.astype(F32)).astype(BF16)
    for g in range(n_blk):
        lf_scr[g, 0:t] = hi[blk(g)]
        lf_scr[g, t:2 * t] = lo[blk(g)]

    for g in range(n_blk):
        sums_scr[g] = jnp.dot(w_ref[...], lf_scr[g], preferred_element_type=F32)

    odd_row = (lax.broadcasted_iota(jnp.int32, (t, 1), 0) & 1) == 1
    for g in range(n_blk):
        q = a_ref[blk(g), :].astype(F32)
        kk = kk_scr[blk(g), :]

        def put(slot, e):
            qs_scr[g, slot] = (q * e).astype(BF16)
            ks_scr[g, slot] = (kk * e).astype(BF16)

        for li in range(n_lvl - 1):
            put(li, jnp.exp2(sums_scr[g, (1 + li) * t:(2 + li) * t]))
        put(n_lvl - 1, jnp.exp2(jnp.where(odd_row, l2f_scr[blk(g), :], 0.0)))
        qs_scr[g, n_lvl] = a_ref[blk(g), :]
        ks_scr[g, n_lvl] = kk.astype(BF16)
        a_cum = sums_scr[g, 0:t]
        qs_scr[g, inter] = (q * jnp.exp2(a_cum)).astype(BF16)
        ks_scr[g, inter] = (kk * jnp.exp2(a_cum[t - 1:t] - a_cum)).astype(BF16)

    in_level = [m_ref[li] != 0.0 for li in range(n_lvl + 1)]
    for g in range(n_blk):
        for h in range(H_B):
            acc = jnp.zeros((t, t), F32)
            for li in range(n_lvl + 1):
                part = lax.dot_general(
                    qs_scr[g, li, :, head(h)], ks_scr[g, li, :, head(h)],
                    CONTRACT_LAST, preferred_element_type=F32)
                acc = jnp.where(in_level[li], part, acc)
            attn_scr[g, h] = acc.astype(BF16)

    for g in range(n_blk):
        for h in range(H_B):
            iv = c_ref[blk(g), head(h)]
            oi_scr[blk(g), head(h)] = jnp.dot(attn_scr[g, h], iv, preferred_element_type=F32)
            u_scr[g, h] = jnp.dot(iv.astype(F32).T.astype(BF16), ks_scr[g, inter, :, head(h)],
                                  preferred_element_type=F32)

    for g in range(n_blk):
        e_last = jnp.exp2(sums_scr[g, t - 1:t, :])
        for h in range(H_B):
            st = st_ref[h] if chained else s0_ref[g, h].T
            o = oi_scr[blk(g), head(h)] + lax.dot_general(
                qs_scr[g, inter, :, head(h)], st.astype(BF16), CONTRACT_LAST,
                preferred_element_type=F32)
            st_new = e_last[:, head(h)] * st + u_scr[g, h]
            if chained:
                st_ref[h] = st_new
            else:
                sout_ref[g, h] = st_new.T
            ms = jnp.mean(o * o, axis=-1, keepdims=True)
            o_ref[blk(g), head(h)] = (o * lax.rsqrt(ms + EPS) * g_ref[:, head(h)]).astype(o_ref.dtype)

    if chained:
        @pl.when(ci == pl.num_programs(0) - 1)
        def _():
            for h in range(H_B):
                sout_ref[h] = st_ref[h].T


def _hgrn2_scratch(t, n_lvl, n_blk):
    step_rows = n_blk * t
    return [pltpu.VMEM((H_B, DK_B, DK_B), F32),
            pltpu.VMEM((n_blk, 2 * t, MIX_WIDTH), BF16),
            pltpu.VMEM((step_rows, MIX_WIDTH), F32),
            pltpu.VMEM((step_rows, MIX_WIDTH), F32),
            pltpu.VMEM((n_blk, n_lvl * t, MIX_WIDTH), F32),
            pltpu.VMEM((n_blk, n_lvl + 2, t, MIX_WIDTH), BF16),
            pltpu.VMEM((n_blk, n_lvl + 2, t, MIX_WIDTH), BF16),
            pltpu.VMEM((n_blk, H_B, t, t), BF16),
            pltpu.VMEM((n_blk, H_B, DK_B, DK_B), F32),
            pltpu.VMEM((step_rows, MIX_WIDTH), F32)]


def hgrn2(a, b, c, s0, layer, lbc, g, t, n_blk, chained):
    rows = a.shape[0]
    w, masks, n_lvl = _hgrn2_constants(t)
    step_rows = n_blk * t
    n_seq = 1 if chained else rows // t
    tok = pl.BlockSpec((step_rows, MIX_WIDTH), lambda i: (i, 0))
    if chained:
        state_in = pl.BlockSpec((None, None, H_B, DK_B, DK_B), lambda i: (layer, 0, 0, 0, 0))
        state_out = pl.BlockSpec((None, H_B, DK_B, DK_B), lambda i: (0, 0, 0, 0))
    else:
        state_in = pl.BlockSpec((None, n_blk, H_B, DK_B, DK_B), lambda i: (layer, i, 0, 0, 0))
        state_out = pl.BlockSpec((n_blk, H_B, DK_B, DK_B), lambda i: (i, 0, 0, 0))

    def whole(arr):
        return pl.BlockSpec(arr.shape, lambda i: (0,) * arr.ndim)

    return pl.pallas_call(
        functools.partial(_hgrn2_kernel, t=t, n_lvl=n_lvl, n_blk=n_blk, chained=chained),
        grid=(rows // step_rows,),
        in_specs=[tok, tok, tok, state_in, whole(lbc),
                  pl.BlockSpec((1, MIX_WIDTH), lambda i: (0, 0)), whole(w), whole(masks)],
        out_specs=[tok, state_out],
        out_shape=[jax.ShapeDtypeStruct((rows, MIX_WIDTH), BF16),
                   jax.ShapeDtypeStruct((n_seq, H_B, DK_B, DK_B), F32)],
        scratch_shapes=_hgrn2_scratch(t, n_lvl, n_blk),
        name=f"hgrn2_t{t}",
        compiler_params=_params("arbitrary"),
    )(a, b, c, s0, lbc, g.reshape(1, MIX_WIDTH), w, masks)


def _post_body(x_ref, mix_ref, xq_ref, gate_ref, mkt_ref, mvt_ref, w_ref, s_scr, p_scr, cross_scr,
               n_seg):
    seg_rows = x_ref.shape[0] // n_seg
    n_pairs = H_X // 2
    for seg in range(n_seg):
        rows = slice(seg * seg_rows, (seg + 1) * seg_rows)
        for hp in range(n_pairs):
            lanes = slice(LANES * hp, LANES * (hp + 1))
            q2 = _stack_pair(xq_ref[rows, lanes] * (HD_X ** -0.5))
            s_scr[seg * n_pairs + hp] = jnp.dot(q2, mkt_ref[seg, lanes, :].astype(BF16),
                                                preferred_element_type=F32)
    for i in range(n_seg * n_pairs):
        s = s_scr[i]
        p_scr[i] = jnp.exp(s - jnp.max(s, axis=-1, keepdims=True)).astype(BF16)
    ones = jnp.ones((LANES, N_MEM), BF16)
    for seg in range(n_seg):
        rows = slice(seg * seg_rows, (seg + 1) * seg_rows)
        for hp in range(n_pairs):
            lanes = slice(LANES * hp, LANES * (hp + 1))
            cross_scr[rows, lanes] = _pair_values_t(
                p_scr[seg * n_pairs + hp],
                jnp.concatenate([mvt_ref[seg, lanes, :].astype(BF16), ones], axis=0))

    gate = gate_ref[...]
    sg = gate / (1.0 + jnp.exp(-gate))
    y_mix = (mix_ref[...].astype(F32) * sg[:, 0:MIX_WIDTH]).astype(BF16)
    y_cross = (cross_scr[...] * sg[:, MIX_WIDTH:]).astype(BF16)
    return (x_ref[...] + jnp.dot(y_mix, w_ref[0:MIX_WIDTH, :], preferred_element_type=F32)
            + jnp.dot(y_cross, w_ref[MIX_WIDTH:, :], preferred_element_type=F32))


def _post_final_kernel(x_ref, mix_ref, xq_ref, gate_ref, mkt_ref, mvt_ref, w_ref, fg_ref, o_ref,
                       s_scr, p_scr, cross_scr, *, n_seg):
    acc = _post_body(x_ref, mix_ref, xq_ref, gate_ref, mkt_ref, mvt_ref, w_ref,
                     s_scr, p_scr, cross_scr, n_seg)
    ms = jnp.mean(acc * acc, axis=-1, keepdims=True)
    o_ref[...] = acc * lax.rsqrt(ms + EPS) * fg_ref[...]


def _post_pre_kernel(x_ref, mix_ref, xq_ref, gate_ref, mkt_ref, mvt_ref, wo_ref, g_ref, wi_ref,
                     xo_ref, *rest, n_seg, splits, tails):
    n_out = len(splits) + len(tails)
    s_scr, p_scr, cross_scr = rest[n_out:]
    acc = _post_body(x_ref, mix_ref, xq_ref, gate_ref, mkt_ref, mvt_ref, wo_ref,
                     s_scr, p_scr, cross_scr, n_seg)
    xo_ref[...] = acc
    _norm_proj_body(acc, g_ref, wi_ref, rest[:n_out], splits, tails, False)


def _post_specs(tm, n_seg, mem_layer, w_layer):
    def tok(n):
        return pl.BlockSpec((tm, n), lambda i: (i, 0))

    mem = pl.BlockSpec((None, n_seg, X_WIDTH, N_MEM), lambda i: (mem_layer, 0, 0, 0))
    w_out = pl.BlockSpec((None, D_INNER, D_MODEL), lambda i: (w_layer, 0, 0),
                         pipeline_mode=pl.Buffered(1))
    in_specs = [tok(D_MODEL), tok(MIX_WIDTH), tok(X_WIDTH), tok(D_INNER), mem, mem, w_out]
    seg_rows = tm // n_seg
    n_items = n_seg * (H_X // 2)
    scratch = [pltpu.VMEM((n_items, 2 * seg_rows, N_MEM), F32),
               pltpu.VMEM((n_items, 2 * seg_rows, N_MEM), BF16),
               pltpu.VMEM((tm, X_WIDTH), F32)]
    return in_specs, scratch, tok(D_MODEL)


def post_final(x, mix, xq, gate, mkt, mvt, mem_layer, w_all, w_layer, final_g, tm, n_seg):
    rows = x.shape[0]
    in_specs, scratch, x_spec = _post_specs(tm, n_seg, mem_layer, w_layer)
    return pl.pallas_call(
        functools.partial(_post_final_kernel, n_seg=n_seg),
        grid=(rows // tm,),
        in_specs=in_specs + [pl.BlockSpec((1, D_MODEL), lambda i: (0, 0))],
        out_specs=x_spec,
        out_shape=jax.ShapeDtypeStruct((rows, D_MODEL), F32),
        scratch_shapes=scratch,
        name=f"post_final_{rows}",
        compiler_params=_params("arbitrary"),
    )(x, mix, xq, gate, mkt, mvt, w_all, final_g.reshape(1, D_MODEL))


def post_pre(x, mix, xq, gate, mkt, mvt, mem_layer, w_out_all, layer, g_next, w_in_all,
             dtypes, tails, tm, n_seg):
    rows = x.shape[0]
    in_specs, scratch, x_spec = _post_specs(tm, n_seg, mem_layer, layer)
    in_specs += [pl.BlockSpec((1, D_MODEL), lambda i: (0, 0)),
                 pl.BlockSpec((None, D_MODEL, w_in_all.shape[2]), lambda i: (layer + 1, 0, 0),
                              pipeline_mode=pl.Buffered(1))]
    proj_specs, proj_shapes = _proj_out_specs(rows, tm, PROJ_SPLITS, dtypes, tails)
    return pl.pallas_call(
        functools.partial(_post_pre_kernel, n_seg=n_seg, splits=PROJ_SPLITS, tails=tuple(tails)),
        grid=(rows // tm,),
        in_specs=in_specs,
        out_specs=[x_spec] + proj_specs,
        out_shape=[jax.ShapeDtypeStruct((rows, D_MODEL), F32)] + proj_shapes,
        scratch_shapes=scratch,
        name=f"post_pre_{rows}",
        compiler_params=_params("arbitrary"),
    )(x, mix, xq, gate, mkt, mvt, w_out_all, g_next.reshape(1, D_MODEL), w_in_all)


def _rel_bias(table):
    band = PAST_BAND + CHUNK
    n_diag = band + CHUNK - 1
    offs = np.arange(n_diag) - (CHUNK - 1)
    idx = np.clip(PAST_BAND - offs, -REL_CLIP, REL_CLIP) + REL_CLIP
    diag = jnp.pad(table[:, idx].astype(F32), ((0, 0), (0, 1)))
    skew = jnp.tile(diag, (1, CHUNK))[:, :CHUNK * n_diag].reshape(H_A, CHUNK, n_diag)
    return skew[:, :, CHUNK - 1:CHUNK - 1 + band]


def _per_head_transposed(cache):
    return jnp.moveaxis(cache, -3, -1)


def kernel(x_prompt, x_sample, cache_a_k, cache_a_v, state_b, cache_mem_k, cache_mem_v, mem_prompt,
           ln_g, w_in, w_out, rel_bias_table, lower_bounds, hgrn_norm_g, mem_norm_g, w_mem_kv, final_g):
    bp, seq, _ = x_prompt.shape
    bs, dec_seq, _ = x_sample.shape
    assert bp == 1
    cache_len = cache_a_k.shape[2]
    n_s = bs * dec_seq
    keep = min(PAST_BAND, seq)

    w_in_b = w_in.astype(BF16)
    w_out_b = w_out.astype(BF16)
    w_mem_b = w_mem_kv.astype(BF16)

    lb_all = jnp.cumsum(jax.nn.softmax(lower_bounds.astype(F32), axis=0), axis=0)
    lb_all = lb_all - lb_all[:1]

    cache_a_kt = _per_head_transposed(cache_a_k)
    cache_a_vt = _per_head_transposed(cache_a_v)
    cache_mem_kt = _per_head_transposed(cache_mem_k).reshape(DEPTH, bs, X_WIDTH, N_MEM)
    cache_mem_vt = _per_head_transposed(cache_mem_v).reshape(DEPTH, bs, X_WIDTH, N_MEM)
    zero_state = jnp.zeros((1, 1, H_B, DK_B, DK_B), F32)

    def per_batch(u):
        return u.reshape(bs, dec_seq, u.shape[-1])

    def layer_io(l):
        attn_layer = l % 2 == 0
        return (BF16, BF16 if attn_layer else F32, BF16, BF16, F32), (1, 2) if attn_layer else ()

    mkt_all, mvt_all = mem_kv(mem_prompt.reshape(N_MEM, D_MODEL), mem_norm_g, w_mem_b)
    mem_k_prompt = jnp.moveaxis(mkt_all.reshape(DEPTH, 1, H_X, HD_X, N_MEM), -1, 2)
    mem_v_prompt = jnp.moveaxis(mvt_all.reshape(DEPTH, 1, H_X, HD_X, N_MEM), -1, 2)

    xp = x_prompt.reshape(seq, D_MODEL)
    xs = x_sample.reshape(n_s, D_MODEL)
    dtypes, tails = layer_io(0)
    outs_p = norm_proj(xp, ln_g[0], w_in_b, 0, PROJ_SPLITS, dtypes, PAST_BAND, tails)
    outs_s = norm_proj(xs, ln_g[0], w_in_b, 0, PROJ_SPLITS, dtypes, n_s, tails)
    ak_p, av_p, sb_p, ak_s, av_s, sb_s = [], [], [], [], [], []
    for l in range(DEPTH):
        j = l // 2
        a_p, b_p, c_p, xq_p, g_p = outs_p[:5]
        a_s, b_s, c_s, xq_s, g_s = outs_s[:5]

        if l % 2 == 0:
            bias = _rel_bias(rel_bias_table[j])
            band = PAST_BAND + CHUNK
            o_p = band_attn_prompt(a_p, b_p, c_p, bias.reshape(H_A // 2, 2 * CHUNK, band), PAST_BAND)
            o_s = band_attn_sample(per_batch(a_s), per_batch(b_s), per_batch(c_s),
                                   cache_a_kt, cache_a_vt, j,
                                   bias[:, :dec_seq, :cache_len + dec_seq].reshape(
                                       H_A // 2, 2 * dec_seq, cache_len + dec_seq)).reshape(n_s, MIX_WIDTH)
            assert keep == PAST_BAND
            ak_p.append(outs_p[5].reshape(1, keep, H_A, HD_A))
            av_p.append(outs_p[6].reshape(1, keep, H_A, HD_A))
            ak_s.append(outs_s[5].reshape(bs, dec_seq, H_A, HD_A))
            av_s.append(outs_s[6].reshape(bs, dec_seq, H_A, HD_A))
        else:
            lb = lb_all[j]
            lbc = jnp.stack([jnp.log2(lb), jnp.log1p(-lb) * LOG2_E, 1.0 - lb])
            o_p, s_p = hgrn2(a_p, b_p, c_p, zero_state, 0, lbc, hgrn_norm_g[j],
                             CHUNK, HGRN_BLOCKS_PER_STEP, True)
            o_s, s_s = hgrn2(a_s, b_s, c_s, state_b.astype(F32), j, lbc, hgrn_norm_g[j],
                             dec_seq, HGRN_SEQS_PER_STEP, False)
            sb_p.append(s_p)
            sb_s.append(s_s)
        mem_p = (mkt_all, mvt_all, l)
        mem_s = (cache_mem_kt, cache_mem_vt, l)
        if l == DEPTH - 1:
            xp = post_final(xp, o_p, xq_p, g_p, *mem_p, w_out_b, l, final_g, PAST_BAND, 1)
            xs = post_final(xs, o_s, xq_s, g_s, *mem_s, w_out_b, l, final_g, n_s, bs)
        else:
            dtypes, tails = layer_io(l + 1)
            xp, *outs_p = post_pre(xp, o_p, xq_p, g_p, *mem_p, w_out_b, l, ln_g[l + 1], w_in_b,
                                   dtypes, tails, PAST_BAND, 1)
            xs, *outs_s = post_pre(xs, o_s, xq_s, g_s, *mem_s, w_out_b, l, ln_g[l + 1], w_in_b,
                                   dtypes, tails, n_s, bs)
    return (xp.reshape(1, seq, D_MODEL), xs.reshape(bs, dec_seq, D_MODEL),
            jnp.stack(ak_p), jnp.stack(av_p), jnp.stack(sb_p), mem_k_prompt, mem_v_prompt,
            jnp.stack(ak_s), jnp.stack(av_s), jnp.stack(sb_s))
```

```python
import functools

import numpy as np
import jax
import jax.numpy as jnp
from jax import lax
from jax.experimental import pallas as pl
from jax.experimental.pallas import tpu as pltpu

D_MODEL = 1024
DEPTH = 4
CHUNK = 64
N_PAST_CHUNKS = 8
PAST_BAND = N_PAST_CHUNKS * CHUNK
MIX_WIDTH = 768
X_WIDTH = 256
D_INNER = MIX_WIDTH + X_WIDTH
HD_A = 64
H_A = MIX_WIDTH // HD_A
REL_CLIP = 128
DK_B = 128
H_B = MIX_WIDTH // DK_B
H_X = 4
HD_X = 64
N_MEM = 256
EPS = 1e-6
NEG = -1e30
LOG2_E = 1.4426950408889634
F32 = jnp.float32
BF16 = jnp.bfloat16

LANES = 128
VMEM_LIMIT_BYTES = 56 * 1024 * 1024
PROJ_SPLITS = (MIX_WIDTH, MIX_WIDTH, MIX_WIDTH, X_WIDTH, D_INNER)
CONTRACT_LAST = (((1,), (1,)), ((), ()))
HGRN_BLOCKS_PER_STEP = 8
HGRN_SEQS_PER_STEP = 8
BAND_SEQS_PER_STEP = 4
BAND_UNROLL = 4


def _params(*sem):
    return pltpu.CompilerParams(dimension_semantics=sem, vmem_limit_bytes=VMEM_LIMIT_BYTES)


def _norm_proj_body(x, g_ref, w_ref, out_refs, splits, tails, transpose):
    ms = jnp.mean(x * x, axis=-1, keepdims=True)
    xn = (x * lax.rsqrt(ms + EPS) * g_ref[...]).astype(BF16)
    tail_refs = out_refs[len(splits):]
    off = 0
    for idx, (o_ref, n) in enumerate(zip(out_refs, splits)):
        r = jnp.dot(xn, w_ref[:, off:off + n].astype(BF16), preferred_element_type=F32)
        o_ref[...] = (r.T if transpose else r).astype(o_ref.dtype)
        if idx in tails:
            tail_refs[tails.index(idx)][...] = r
        off += n


def _norm_proj_kernel(x_ref, g_ref, w_ref, *out_refs, splits, tails):
    _norm_proj_body(x_ref[...], g_ref, w_ref, out_refs, splits, tails, False)


def _proj_out_specs(rows, tm, splits, dtypes, tails):
    out_specs = [pl.BlockSpec((tm, n), lambda i: (i, 0)) for n in splits]
    out_shape = [jax.ShapeDtypeStruct((rows, n), dt) for n, dt in zip(splits, dtypes)]
    tail_rows = min(PAST_BAND, rows)
    first_tail = (rows - tail_rows) // tm
    out_specs += [pl.BlockSpec((tm, splits[idx]), lambda i: (jnp.maximum(i - first_tail, 0), 0))
                  for idx in tails]
    out_shape += [jax.ShapeDtypeStruct((tail_rows, splits[idx]), F32) for idx in tails]
    return out_specs, out_shape


def norm_proj(x2d, g, w_all, layer, splits, dtypes, tm, tails=()):
    rows, d = x2d.shape
    n_total = w_all.shape[2]
    assert rows % tm == 0
    out_specs, out_shape = _proj_out_specs(rows, tm, splits, dtypes, tails)
    return pl.pallas_call(
        functools.partial(_norm_proj_kernel, splits=splits, tails=tuple(tails)),
        grid=(rows // tm,),
        in_specs=[
            pl.BlockSpec((tm, d), lambda i: (i, 0)),
            pl.BlockSpec((1, d), lambda i: (0, 0)),
            pl.BlockSpec((None, d, n_total), lambda i: (layer, 0, 0)),
        ],
        out_specs=out_specs,
        out_shape=out_shape,
        name=f"norm_proj_{rows}x{n_total}",
        compiler_params=_params("arbitrary"),
    )(x2d, g.reshape(1, d), w_all)


def _mem_kv_kernel(x_ref, g_ref, w_ref, kt_ref, vt_ref):
    _norm_proj_body(x_ref[...], g_ref, w_ref, (kt_ref, vt_ref), (X_WIDTH, X_WIDTH), (), True)


def mem_kv(mem, g_all, w_all):
    n_mem, d = mem.shape
    depth = w_all.shape[0]
    out = pl.BlockSpec((None, None, X_WIDTH, n_mem), lambda l: (l, 0, 0, 0))
    return pl.pallas_call(
        _mem_kv_kernel,
        grid=(depth,),
        in_specs=[pl.BlockSpec((n_mem, d), lambda l: (0, 0)),
                  pl.BlockSpec((None, 1, d), lambda l: (l, 0, 0)),
                  pl.BlockSpec((None, d, 2 * X_WIDTH), lambda l: (l, 0, 0))],
        out_specs=[out, out],
        out_shape=[jax.ShapeDtypeStruct((depth, 1, X_WIDTH, n_mem), F32)] * 2,
        name="mem_kv",
        compiler_params=_params("arbitrary"),
    )(mem, g_all.reshape(depth, 1, d), w_all)


def _stack_pair(q_pair):
    first = lax.broadcasted_iota(jnp.int32, (1, LANES), 1) < HD_A
    keep0 = jnp.where(first, 1.0, 0.0).astype(BF16)
    keep1 = jnp.where(first, 0.0, 1.0).astype(BF16)
    return jnp.concatenate([q_pair * keep0, q_pair * keep1], axis=0)


def _unstack_pair(o2):
    m = o2.shape[0] // 2
    first = lax.broadcasted_iota(jnp.int32, (m, LANES), 1) < HD_A
    return jnp.where(first, o2[:m], o2[m:])


def _pair_values(p, v_ext):
    r = jnp.dot(p, v_ext, preferred_element_type=F32)
    return _unstack_pair(r[:, :LANES] / r[:, LANES:])


def _pair_values_t(p, vt_ext):
    r = lax.dot_general(p, vt_ext, CONTRACT_LAST, preferred_element_type=F32)
    return _unstack_pair(r[:, :LANES] / r[:, LANES:])


def _band_attn_kernel(q_ref, k_ref, v_ref, bias_ref, o_ref,
                      kwin, vext, s_scr, p_scr, *, n_chunks, unroll):
    step = pl.program_id(0)
    chunk, prev_rows = CHUNK, PAST_BAND
    cur_rows = n_chunks * chunk
    band = prev_rows + chunk
    n_pairs = H_A // 2
    assert cur_rows == prev_rows

    @pl.when(step == 0)
    def _():
        kwin[...] = jnp.zeros_like(kwin)
        vext[...] = jnp.zeros_like(vext)

    kwin[0:prev_rows, :] = kwin[prev_rows:prev_rows + cur_rows, :]
    kwin[prev_rows:prev_rows + cur_rows, :] = k_ref[...]
    ones = jnp.ones((prev_rows + cur_rows, LANES), BF16)
    for hp in range(n_pairs):
        lanes = slice(LANES * hp, LANES * (hp + 1))
        vcol = slice(2 * LANES * hp, 2 * LANES * hp + LANES)
        vext[0:prev_rows, vcol] = vext[prev_rows:prev_rows + cur_rows, vcol]
        vext[prev_rows:prev_rows + cur_rows, vcol] = v_ref[:, lanes]
        vext[:, 2 * LANES * hp + LANES:2 * LANES * (hp + 1)] = ones

    def chunks_body(ci, carry):
        starts = [pl.multiple_of((ci * unroll + u) * chunk, chunk) for u in range(unroll)]
        for u, r0 in enumerate(starts):
            for hp in range(n_pairs):
                lanes = slice(LANES * hp, LANES * (hp + 1))
                q2 = _stack_pair(q_ref[pl.ds(r0, chunk), lanes] * (HD_A ** -0.5))
                s_scr[u * n_pairs + hp] = lax.dot_general(
                    q2, kwin[pl.ds(r0, band), lanes], CONTRACT_LAST, preferred_element_type=F32)
        for u, r0 in enumerate(starts):
            key_row = lax.broadcasted_iota(jnp.int32, (1, band), 1) + r0 + (step - 1) * prev_rows
            neg = jnp.where(key_row < 0, NEG, 0.0)
            for hp in range(n_pairs):
                s = s_scr[u * n_pairs + hp] + bias_ref[hp] + neg
                m = jnp.max(s, axis=-1, keepdims=True)
                p_scr[u * n_pairs + hp] = jnp.exp(s - m).astype(BF16)
        for u, r0 in enumerate(starts):
            for hp in range(n_pairs):
                o_ref[pl.ds(r0, chunk), LANES * hp:LANES * (hp + 1)] = _pair_values(
                    p_scr[u * n_pairs + hp],
                    vext[pl.ds(r0, band), 2 * LANES * hp:2 * LANES * (hp + 1)]
                ).astype(o_ref.dtype)
        return carry

    lax.fori_loop(0, n_chunks // unroll, chunks_body, 0)


def band_attn_prompt(q, k, v, bias, block_rows):
    rows = q.shape[0]
    n_chunks = block_rows // CHUNK
    n_pairs = H_A // 2
    band = PAST_BAND + CHUNK
    assert block_rows == PAST_BAND
    cur = pl.BlockSpec((block_rows, MIX_WIDTH), lambda i: (i, 0))
    return pl.pallas_call(
        functools.partial(_band_attn_kernel, n_chunks=n_chunks, unroll=BAND_UNROLL),
        grid=(rows // block_rows,),
        in_specs=[cur, cur, cur, pl.BlockSpec(bias.shape, lambda i: (0, 0, 0))],
        out_specs=cur,
        out_shape=jax.ShapeDtypeStruct((rows, MIX_WIDTH), BF16),
        scratch_shapes=[pltpu.VMEM((PAST_BAND + block_rows, MIX_WIDTH), BF16),
                        pltpu.VMEM((PAST_BAND + block_rows, 2 * MIX_WIDTH), BF16),
                        pltpu.VMEM((BAND_UNROLL * n_pairs, 2 * CHUNK, band), F32),
                        pltpu.VMEM((BAND_UNROLL * n_pairs, 2 * CHUNK, band), BF16)],
        name="band_attn_prompt",
        compiler_params=_params("arbitrary"),
    )(q, k, v, bias)


def _band_sample_kernel(q_ref, k_ref, v_ref, kct_ref, vct_ref, bias_ref, o_ref, s_scr, p_scr,
                        *, cache_len):
    n_pairs = H_A // 2
    n_par, s_len, _ = q_ref.shape
    items = [(b, hp) for b in range(n_par) for hp in range(n_pairs)]
    for i, (b, hp) in enumerate(items):
        lanes = slice(LANES * hp, LANES * (hp + 1))
        q2 = _stack_pair(q_ref[b, :, lanes] * (HD_A ** -0.5))
        kct = kct_ref[b, 2 * hp:2 * hp + 2].reshape(LANES, cache_len).astype(BF16)
        s_scr[i, :, 0:cache_len] = jnp.dot(q2, kct, preferred_element_type=F32)
        s_scr[i, :, cache_len:] = lax.dot_general(q2, k_ref[b, :, lanes], CONTRACT_LAST,
                                                  preferred_element_type=F32)
    for i, (b, hp) in enumerate(items):
        s = s_scr[i] + bias_ref[hp]
        p_scr[i] = jnp.exp(s - jnp.max(s, axis=-1, keepdims=True)).astype(BF16)
    ones_old = jnp.ones((LANES, cache_len), BF16)
    ones_new = jnp.ones((s_len, LANES), BF16)
    for i, (b, hp) in enumerate(items):
        lanes = slice(LANES * hp, LANES * (hp + 1))
        vct = vct_ref[b, 2 * hp:2 * hp + 2].reshape(LANES, cache_len).astype(BF16)
        r = (lax.dot_general(p_scr[i, :, 0:cache_len], jnp.concatenate([vct, ones_old], axis=0),
                             CONTRACT_LAST, preferred_element_type=F32)
             + jnp.dot(p_scr[i, :, cache_len:],
                       jnp.concatenate([v_ref[b, :, lanes], ones_new], axis=1),
                       preferred_element_type=F32))
        o_ref[b, :, lanes] = _unstack_pair(r[:, :LANES] / r[:, LANES:]).astype(o_ref.dtype)


def band_attn_sample(q, k, v, k_cache_t, v_cache_t, layer, bias, n_par):
    b, s_len, _ = q.shape
    cache_len = k_cache_t.shape[-1]
    n_items = n_par * (H_A // 2)
    new = pl.BlockSpec((n_par, s_len, MIX_WIDTH), lambda i: (i, 0, 0))
    old = pl.BlockSpec((None, n_par, H_A, HD_A, cache_len), lambda i: (layer, i, 0, 0, 0))
    return pl.pallas_call(
        functools.partial(_band_sample_kernel, cache_len=cache_len),
        grid=(b // n_par,),
        in_specs=[new, new, new, old, old, pl.BlockSpec(bias.shape, lambda i: (0, 0, 0))],
        out_specs=new,
        out_shape=jax.ShapeDtypeStruct((b, s_len, MIX_WIDTH), BF16),
        scratch_shapes=[pltpu.VMEM((n_items, 2 * s_len, cache_len + s_len), F32),
                        pltpu.VMEM((n_items, 2 * s_len, cache_len + s_len), BF16)],
        name="band_attn_sample",
        compiler_params=_params("arbitrary"),
    )(q, k, v, k_cache_t, v_cache_t, bias)


def _hgrn2_constants(t):
    halves = []
    h = t // 2
    while h >= 1:
        halves.append(h)
        h //= 2
    n_lvl = len(halves)
    w = np.zeros((n_lvl * t, t), np.float32)
    masks = np.zeros((n_lvl + 1, t, t), np.float32)
    for row in range(t):
        w[row, :row + 1] = 1.0
    for li, h in enumerate(halves):
        base = (1 + li) * t
        for row in range(t):
            r = (row // (2 * h)) * 2 * h + h - 1
            if h == 1:
                continue
            if row > r:
                w[base + row, r + 1:row + 1] = 1.0
            else:
                w[base + row, row + 1:r + 1] = 1.0
        for tq in range(t):
            for sk in range(t):
                if tq // (2 * h) == sk // (2 * h):
                    r = (tq // (2 * h)) * 2 * h + h - 1
                    if tq > r and sk <= r:
                        masks[li, tq, sk] = 1.0
    masks[n_lvl] = np.eye(t, dtype=np.float32)
    return jnp.asarray(np.tile(w, (1, 2)), BF16), jnp.asarray(masks, F32), n_lvl


def _hgrn2_kernel(a_ref, b_ref, c_ref, s0_ref, lbc_ref, g_ref, w_ref, m_ref, o_ref, sout_ref,
                st_ref, lf_scr, kk_scr, l2f_scr, sums_scr, qs_scr, ks_scr, attn_scr, u_scr, oi_scr,
                *, t, n_lvl, n_blk, chained):
    inter = n_lvl + 1
    ci = pl.program_id(0)

    if chained:
        @pl.when(ci == 0)
        def _():
            for h in range(H_B):
                st_ref[h] = s0_ref[h].T

    def blk(g):
        return slice(g * t, (g + 1) * t)

    def head(h):
        return slice(LANES * h, LANES * (h + 1))

    z = b_ref[...]
    e_neg = jnp.exp2(jnp.abs(z) * (-LOG2_E))
    u = lbc_ref[1:2, :] + jnp.minimum(z, 0.0) * LOG2_E - jnp.log2(1.0 + e_neg)
    log2_lb = lbc_ref[0:1, :]
    log2_f = jnp.maximum(log2_lb, u) + jnp.log2(1.0 + jnp.exp2(-jnp.abs(log2_lb - u)))
    kk_scr[...] = lbc_ref[2:3, :] * jnp.where(z > 0.0, e_neg, 1.0) / (1.0 + e_neg)
    l2f_scr[...] = log2_f
    hi = log2_f.astype(BF16)
    lo = (log2_f - hi.astype(F32)).astype(BF16)
    for g in range(n_blk):
        lf_scr[g, 0:t] = hi[blk(g)]
        lf_scr[g, t:2 * t] = lo[blk(g)]

    for g in range(n_blk):
        sums_scr[g] = jnp.dot(w_ref[...], lf_scr[g], preferred_element_type=F32)

    odd_row = (lax.broadcasted_iota(jnp.int32, (t, 1), 0) & 1) == 1
    for g in range(n_blk):
        q = a_ref[blk(g), :].astype(F32)
        kk = kk_scr[blk(g), :]

        def put(slot, e):
            qs_scr[g, slot] = (q * e).astype(BF16)
            ks_scr[g, slot] = (kk * e).astype(BF16)

        for li in range(n_lvl - 1):
            put(li, jnp.exp2(sums_scr[g, (1 + li) * t:(2 + li) * t]))
        put(n_lvl - 1, jnp.exp2(jnp.where(odd_row, l2f_scr[blk(g), :], 0.0)))
        qs_scr[g, n_lvl] = a_ref[blk(g), :]
        ks_scr[g, n_lvl] = kk.astype(BF16)
        a_cum = sums_scr[g, 0:t]
        qs_scr[g, inter] = (q * jnp.exp2(a_cum)).astype(BF16)
        ks_scr[g, inter] = (kk * jnp.exp2(a_cum[t - 1:t] - a_cum)).astype(BF16)

    in_level = [m_ref[li] != 0.0 for li in range(n_lvl + 1)]
    for g in range(n_blk):
        for h in range(H_B):
            acc = jnp.zeros((t, t), F32)
            for li in range(n_lvl + 1):
                part = lax.dot_general(
                    qs_scr[g, li, :, head(h)], ks_scr[g, li, :, head(h)],
                    CONTRACT_LAST, preferred_element_type=F32)
                acc = jnp.where(in_level[li], part, acc)
            attn_scr[g, h] = acc.astype(BF16)

    for g in range(n_blk):
        for h in range(H_B):
            iv = c_ref[blk(g), head(h)]
            oi_scr[blk(g), head(h)] = jnp.dot(attn_scr[g, h], iv, preferred_element_type=F32)
            u_scr[g, h] = jnp.dot(iv.astype(F32).T.astype(BF16), ks_scr[g, inter, :, head(h)],
                                  preferred_element_type=F32)

    for g in range(n_blk):
        e_last = jnp.exp2(sums_scr[g, t - 1:t, :])
        for h in range(H_B):
            st = st_ref[h] if chained else s0_ref[g, h].T
            o = oi_scr[blk(g), head(h)] + lax.dot_general(
                qs_scr[g, inter, :, head(h)], st.astype(BF16), CONTRACT_LAST,
                preferred_element_type=F32)
            st_new = e_last[:, head(h)] * st + u_scr[g, h]
            if chained:
                st_ref[h] = st_new
            else:
                sout_ref[g, h] = st_new.T
            ms = jnp.mean(o * o, axis=-1, keepdims=True)
            o_ref[blk(g), head(h)] = (o * lax.rsqrt(ms + EPS) * g_ref[:, head(h)]).astype(o_ref.dtype)

    if chained:
        @pl.when(ci == pl.num_programs(0) - 1)
        def _():
            for h in range(H_B):
                sout_ref[h] = st_ref[h].T


def _hgrn2_scratch(t, n_lvl, n_blk):
    step_rows = n_blk * t
    return [pltpu.VMEM((H_B, DK_B, DK_B), F32),
            pltpu.VMEM((n_blk, 2 * t, MIX_WIDTH), BF16),
            pltpu.VMEM((step_rows, MIX_WIDTH), F32),
            pltpu.VMEM((step_rows, MIX_WIDTH), F32),
            pltpu.VMEM((n_blk, n_lvl * t, MIX_WIDTH), F32),
            pltpu.VMEM((n_blk, n_lvl + 2, t, MIX_WIDTH), BF16),
            pltpu.VMEM((n_blk, n_lvl + 2, t, MIX_WIDTH), BF16),
            pltpu.VMEM((n_blk, H_B, t, t), BF16),
            pltpu.VMEM((n_blk, H_B, DK_B, DK_B), F32),
            pltpu.VMEM((step_rows, MIX_WIDTH), F32)]


def hgrn2(a, b, c, s0, layer, lbc, g, t, n_blk, chained):
    rows = a.shape[0]
    w, masks, n_lvl = _hgrn2_constants(t)
    step_rows = n_blk * t
    n_seq = 1 if chained else rows // t
    tok = pl.BlockSpec((step_rows, MIX_WIDTH), lambda i: (i, 0))
    if chained:
        state_in = pl.BlockSpec((None, None, H_B, DK_B, DK_B), lambda i: (layer, 0, 0, 0, 0))
        state_out = pl.BlockSpec((None, H_B, DK_B, DK_B), lambda i: (0, 0, 0, 0))
    else:
        state_in = pl.BlockSpec((None, n_blk, H_B, DK_B, DK_B), lambda i: (layer, i, 0, 0, 0))
        state_out = pl.BlockSpec((n_blk, H_B, DK_B, DK_B), lambda i: (i, 0, 0, 0))

    def whole(arr):
        return pl.BlockSpec(arr.shape, lambda i: (0,) * arr.ndim)

    return pl.pallas_call(
        functools.partial(_hgrn2_kernel, t=t, n_lvl=n_lvl, n_blk=n_blk, chained=chained),
        grid=(rows // step_rows,),
        in_specs=[tok, tok, tok, state_in, whole(lbc),
                  pl.BlockSpec((1, MIX_WIDTH), lambda i: (0, 0)), whole(w), whole(masks)],
        out_specs=[tok, state_out],
        out_shape=[jax.ShapeDtypeStruct((rows, MIX_WIDTH), BF16),
                   jax.ShapeDtypeStruct((n_seq, H_B, DK_B, DK_B), F32)],
        scratch_shapes=_hgrn2_scratch(t, n_lvl, n_blk),
        name=f"hgrn2_t{t}",
        compiler_params=_params("arbitrary"),
    )(a, b, c, s0, lbc, g.reshape(1, MIX_WIDTH), w, masks)


def _post_body(x_ref, mix_ref, xq_ref, gate_ref, mkt_ref, mvt_ref, w_ref, s_scr, p_scr, cross_scr,
               n_seg):
    seg_rows = x_ref.shape[0] // n_seg
    n_pairs = H_X // 2
    for seg in range(n_seg):
        rows = slice(seg * seg_rows, (seg + 1) * seg_rows)
        for hp in range(n_pairs):
            lanes = slice(LANES * hp, LANES * (hp + 1))
            q2 = _stack_pair(xq_ref[rows, lanes] * (HD_X ** -0.5))
            s_scr[seg * n_pairs + hp] = jnp.dot(q2, mkt_ref[seg, lanes, :].astype(BF16),
                                                preferred_element_type=F32)
    for i in range(n_seg * n_pairs):
        s = s_scr[i]
        p_scr[i] = jnp.exp(s - jnp.max(s, axis=-1, keepdims=True)).astype(BF16)
    ones = jnp.ones((LANES, N_MEM), BF16)
    for seg in range(n_seg):
        rows = slice(seg * seg_rows, (seg + 1) * seg_rows)
        for hp in range(n_pairs):
            lanes = slice(LANES * hp, LANES * (hp + 1))
            cross_scr[rows, lanes] = _pair_values_t(
                p_scr[seg * n_pairs + hp],
                jnp.concatenate([mvt_ref[seg, lanes, :].astype(BF16), ones], axis=0))

    gate = gate_ref[...]
    sg = gate / (1.0 + jnp.exp(-gate))
    y_mix = (mix_ref[...].astype(F32) * sg[:, 0:MIX_WIDTH]).astype(BF16)
    y_cross = (cross_scr[...] * sg[:, MIX_WIDTH:]).astype(BF16)
    return (x_ref[...]
            + jnp.dot(y_mix, w_ref[0:MIX_WIDTH, :].astype(BF16), preferred_element_type=F32)
            + jnp.dot(y_cross, w_ref[MIX_WIDTH:, :].astype(BF16), preferred_element_type=F32))


def _post_final_kernel(x_ref, mix_ref, xq_ref, gate_ref, mkt_ref, mvt_ref, w_ref, fg_ref, o_ref,
                       s_scr, p_scr, cross_scr, *, n_seg):
    acc = _post_body(x_ref, mix_ref, xq_ref, gate_ref, mkt_ref, mvt_ref, w_ref,
                     s_scr, p_scr, cross_scr, n_seg)
    ms = jnp.mean(acc * acc, axis=-1, keepdims=True)
    o_ref[...] = acc * lax.rsqrt(ms + EPS) * fg_ref[...]


def _post_pre_kernel(x_ref, mix_ref, xq_ref, gate_ref, mkt_ref, mvt_ref, wo_ref, g_ref, wi_ref,
                     xo_ref, *rest, n_seg, splits, tails):
    n_out = len(splits) + len(tails)
    s_scr, p_scr, cross_scr = rest[n_out:]
    acc = _post_body(x_ref, mix_ref, xq_ref, gate_ref, mkt_ref, mvt_ref, wo_ref,
                     s_scr, p_scr, cross_scr, n_seg)
    xo_ref[...] = acc
    _norm_proj_body(acc, g_ref, wi_ref, rest[:n_out], splits, tails, False)


def _post_specs(tm, n_seg, mem_layer, w_layer, mem_per_tile):
    def tok(n):
        return pl.BlockSpec((tm, n), lambda i: (i, 0))

    mem = pl.BlockSpec((None, n_seg, X_WIDTH, N_MEM),
                       lambda i: (mem_layer, i if mem_per_tile else 0, 0, 0))
    w_out = pl.BlockSpec((None, D_INNER, D_MODEL), lambda i: (w_layer, 0, 0),
                         pipeline_mode=pl.Buffered(1))
    in_specs = [tok(D_MODEL), tok(MIX_WIDTH), tok(X_WIDTH), tok(D_INNER), mem, mem, w_out]
    seg_rows = tm // n_seg
    n_items = n_seg * (H_X // 2)
    scratch = [pltpu.VMEM((n_items, 2 * seg_rows, N_MEM), F32),
               pltpu.VMEM((n_items, 2 * seg_rows, N_MEM), BF16),
               pltpu.VMEM((tm, X_WIDTH), F32)]
    return in_specs, scratch, tok(D_MODEL)


def post_final(x, mix, xq, gate, mkt, mvt, mem_layer, w_all, w_layer, final_g, tm, n_seg,
               mem_per_tile):
    rows = x.shape[0]
    in_specs, scratch, x_spec = _post_specs(tm, n_seg, mem_layer, w_layer, mem_per_tile)
    return pl.pallas_call(
        functools.partial(_post_final_kernel, n_seg=n_seg),
        grid=(rows // tm,),
        in_specs=in_specs + [pl.BlockSpec((1, D_MODEL), lambda i: (0, 0))],
        out_specs=x_spec,
        out_shape=jax.ShapeDtypeStruct((rows, D_MODEL), F32),
        scratch_shapes=scratch,
        name=f"post_final_{rows}",
        compiler_params=_params("arbitrary"),
    )(x, mix, xq, gate, mkt, mvt, w_all, final_g.reshape(1, D_MODEL))


def post_pre(x, mix, xq, gate, mkt, mvt, mem_layer, w_out_all, layer, g_next, w_in_all,
             dtypes, tails, tm, n_seg, mem_per_tile):
    rows = x.shape[0]
    in_specs, scratch, x_spec = _post_specs(tm, n_seg, mem_layer, layer, mem_per_tile)
    in_specs += [pl.BlockSpec((1, D_MODEL), lambda i: (0, 0)),
                 pl.BlockSpec((None, D_MODEL, w_in_all.shape[2]), lambda i: (layer + 1, 0, 0),
                              pipeline_mode=pl.Buffered(1))]
    proj_specs, proj_shapes = _proj_out_specs(rows, tm, PROJ_SPLITS, dtypes, tails)
    return pl.pallas_call(
        functools.partial(_post_pre_kernel, n_seg=n_seg, splits=PROJ_SPLITS, tails=tuple(tails)),
        grid=(rows // tm,),
        in_specs=in_specs,
        out_specs=[x_spec] + proj_specs,
        out_shape=[jax.ShapeDtypeStruct((rows, D_MODEL), F32)] + proj_shapes,
        scratch_shapes=scratch,
        name=f"post_pre_{rows}",
        compiler_params=_params("arbitrary"),
    )(x, mix, xq, gate, mkt, mvt, w_out_all, g_next.reshape(1, D_MODEL), w_in_all)


def _rel_bias(table):
    band = PAST_BAND + CHUNK
    n_diag = band + CHUNK - 1
    offs = np.arange(n_diag) - (CHUNK - 1)
    idx = np.clip(PAST_BAND - offs, -REL_CLIP, REL_CLIP) + REL_CLIP
    diag = jnp.pad(table[:, idx].astype(F32), ((0, 0), (0, 1)))
    skew = jnp.tile(diag, (1, CHUNK))[:, :CHUNK * n_diag].reshape(H_A, CHUNK, n_diag)
    return skew[:, :, CHUNK - 1:CHUNK - 1 + band]


def _per_head_transposed(cache):
    return jnp.moveaxis(cache, -3, -1)


def kernel(x_prompt, x_sample, cache_a_k, cache_a_v, state_b, cache_mem_k, cache_mem_v, mem_prompt,
           ln_g, w_in, w_out, rel_bias_table, lower_bounds, hgrn_norm_g, mem_norm_g, w_mem_kv, final_g):
    bp, seq, _ = x_prompt.shape
    bs, dec_seq, _ = x_sample.shape
    assert bp == 1
    cache_len = cache_a_k.shape[2]
    n_s = bs * dec_seq
    keep = min(PAST_BAND, seq)

    w_in_b = w_in.astype(BF16)

    lb_all = jnp.cumsum(jax.nn.softmax(lower_bounds.astype(F32), axis=0), axis=0)
    lb_all = lb_all - lb_all[:1]

    cache_a_kt = _per_head_transposed(cache_a_k)
    cache_a_vt = _per_head_transposed(cache_a_v)
    cache_mem_kt = _per_head_transposed(cache_mem_k).reshape(DEPTH, bs, X_WIDTH, N_MEM)
    cache_mem_vt = _per_head_transposed(cache_mem_v).reshape(DEPTH, bs, X_WIDTH, N_MEM)
    zero_state = jnp.zeros((1, 1, H_B, DK_B, DK_B), F32)

    sample_tiles = (n_s // 2, bs // 2)

    def per_batch(u):
        return u.reshape(bs, dec_seq, u.shape[-1])

    def layer_io(l):
        attn_layer = l % 2 == 0
        return (BF16, BF16 if attn_layer else F32, BF16, BF16, F32), (1, 2) if attn_layer else ()

    mkt_all, mvt_all = mem_kv(mem_prompt.reshape(N_MEM, D_MODEL), mem_norm_g, w_mem_kv)
    mem_k_prompt = jnp.moveaxis(mkt_all.reshape(DEPTH, 1, H_X, HD_X, N_MEM), -1, 2)
    mem_v_prompt = jnp.moveaxis(mvt_all.reshape(DEPTH, 1, H_X, HD_X, N_MEM), -1, 2)

    xp = x_prompt.reshape(seq, D_MODEL)
    xs = x_sample.reshape(n_s, D_MODEL)
    dtypes, tails = layer_io(0)
    outs_p = norm_proj(xp, ln_g[0], w_in_b, 0, PROJ_SPLITS, dtypes, PAST_BAND, tails)
    outs_s = norm_proj(xs, ln_g[0], w_in_b, 0, PROJ_SPLITS, dtypes, n_s, tails)
    ak_p, av_p, sb_p, ak_s, av_s, sb_s = [], [], [], [], [], []
    for l in range(DEPTH):
        j = l // 2
        a_p, b_p, c_p, xq_p, g_p = outs_p[:5]
        a_s, b_s, c_s, xq_s, g_s = outs_s[:5]

        if l % 2 == 0:
            bias = _rel_bias(rel_bias_table[j])
            band = PAST_BAND + CHUNK
            o_p = band_attn_prompt(a_p, b_p, c_p, bias.reshape(H_A // 2, 2 * CHUNK, band), PAST_BAND)
            o_s = band_attn_sample(per_batch(a_s), per_batch(b_s), per_batch(c_s),
                                   cache_a_kt, cache_a_vt, j,
                                   bias[:, :dec_seq, :cache_len + dec_seq].reshape(
                                       H_A // 2, 2 * dec_seq, cache_len + dec_seq),
                                   BAND_SEQS_PER_STEP).reshape(n_s, MIX_WIDTH)
            assert keep == PAST_BAND
            ak_p.append(outs_p[5].reshape(1, keep, H_A, HD_A))
            av_p.append(outs_p[6].reshape(1, keep, H_A, HD_A))
            ak_s.append(outs_s[5].reshape(bs, dec_seq, H_A, HD_A))
            av_s.append(outs_s[6].reshape(bs, dec_seq, H_A, HD_A))
        else:
            lb = lb_all[j]
            lbc = jnp.stack([jnp.log2(lb), jnp.log1p(-lb) * LOG2_E, 1.0 - lb])
            o_p, s_p = hgrn2(a_p, b_p, c_p, zero_state, 0, lbc, hgrn_norm_g[j],
                             CHUNK, HGRN_BLOCKS_PER_STEP, True)
            o_s, s_s = hgrn2(a_s, b_s, c_s, state_b.astype(F32), j, lbc, hgrn_norm_g[j],
                             dec_seq, HGRN_SEQS_PER_STEP, False)
            sb_p.append(s_p)
            sb_s.append(s_s)
        mem_p = (mkt_all, mvt_all, l)
        mem_s = (cache_mem_kt, cache_mem_vt, l)
        if l == DEPTH - 1:
            xp = post_final(xp, o_p, xq_p, g_p, *mem_p, w_out, l, final_g, PAST_BAND, 1, False)
            xs = post_final(xs, o_s, xq_s, g_s, *mem_s, w_out, l, final_g, *sample_tiles, True)
        else:
            dtypes, tails = layer_io(l + 1)
            xp, *outs_p = post_pre(xp, o_p, xq_p, g_p, *mem_p, w_out, l, ln_g[l + 1], w_in_b,
                                   dtypes, tails, PAST_BAND, 1, False)
            xs, *outs_s = post_pre(xs, o_s, xq_s, g_s, *mem_s, w_out, l, ln_g[l + 1], w_in_b,
                                   dtypes, tails, *sample_tiles, True)
    return (xp.reshape(1, seq, D_MODEL), xs.reshape(bs, dec_seq, D_MODEL),
            jnp.stack(ak_p), jnp.stack(av_p), jnp.stack(sb_p), mem_k_prompt, mem_v_prompt,
            jnp.stack(ak_s), jnp.stack(av_s), jnp.stack(sb_s))
```

```python
import functools

import numpy as np
import jax
import jax.numpy as jnp
from jax import lax
from jax.experimental import pallas as pl
from jax.experimental.pallas import tpu as pltpu

D_MODEL = 1024
DEPTH = 4
CHUNK = 64
N_PAST_CHUNKS = 8
PAST_BAND = N_PAST_CHUNKS * CHUNK
MIX_WIDTH = 768
X_WIDTH = 256
D_INNER = MIX_WIDTH + X_WIDTH
HD_A = 64
H_A = MIX_WIDTH // HD_A
REL_CLIP = 128
DK_B = 128
H_B = MIX_WIDTH // DK_B
H_X = 4
HD_X = 64
N_MEM = 256
EPS = 1e-6
NEG = -1e30
LOG2_E = 1.4426950408889634
F32 = jnp.float32
BF16 = jnp.bfloat16

LANES = 128
VMEM_LIMIT_BYTES = 56 * 1024 * 1024
PROJ_SPLITS = (MIX_WIDTH, MIX_WIDTH, MIX_WIDTH, X_WIDTH, D_INNER)
CONTRACT_LAST = (((1,), (1,)), ((), ()))
HGRN_BLOCKS_PER_STEP = 8
HGRN_SEQS_PER_STEP = 8
BAND_SEQS_PER_STEP = 4
BAND_GROUP = 2
BAND_UNROLL = 2


def _params(*sem):
    return pltpu.CompilerParams(dimension_semantics=sem, vmem_limit_bytes=VMEM_LIMIT_BYTES)


def _norm_proj_body(x, g_ref, w_ref, out_refs, splits, tails, transpose):
    ms = jnp.mean(x * x, axis=-1, keepdims=True)
    xn = (x * lax.rsqrt(ms + EPS) * g_ref[...]).astype(BF16)
    tail_refs = out_refs[len(splits):]
    off = 0
    for idx, (o_ref, n) in enumerate(zip(out_refs, splits)):
        r = jnp.dot(xn, w_ref[:, off:off + n].astype(BF16), preferred_element_type=F32)
        o_ref[...] = (r.T if transpose else r).astype(o_ref.dtype)
        if idx in tails:
            tail_refs[tails.index(idx)][...] = r
        off += n


def _norm_proj_kernel(x_ref, g_ref, w_ref, *out_refs, splits, tails):
    _norm_proj_body(x_ref[...], g_ref, w_ref, out_refs, splits, tails, False)


def _proj_out_specs(rows, tm, splits, dtypes, tails):
    out_specs = [pl.BlockSpec((tm, n), lambda i: (i, 0)) for n in splits]
    out_shape = [jax.ShapeDtypeStruct((rows, n), dt) for n, dt in zip(splits, dtypes)]
    tail_rows = min(PAST_BAND, rows)
    first_tail = (rows - tail_rows) // tm
    out_specs += [pl.BlockSpec((tm, splits[idx]), lambda i: (jnp.maximum(i - first_tail, 0), 0))
                  for idx in tails]
    out_shape += [jax.ShapeDtypeStruct((tail_rows, splits[idx]), F32) for idx in tails]
    return out_specs, out_shape


def norm_proj(x2d, g, w_all, layer, splits, dtypes, tm, tails=()):
    rows, d = x2d.shape
    n_total = w_all.shape[2]
    assert rows % tm == 0
    out_specs, out_shape = _proj_out_specs(rows, tm, splits, dtypes, tails)
    return pl.pallas_call(
        functools.partial(_norm_proj_kernel, splits=splits, tails=tuple(tails)),
        grid=(rows // tm,),
        in_specs=[
            pl.BlockSpec((tm, d), lambda i: (i, 0)),
            pl.BlockSpec((1, d), lambda i: (0, 0)),
            pl.BlockSpec((None, d, n_total), lambda i: (layer, 0, 0)),
        ],
        out_specs=out_specs,
        out_shape=out_shape,
        name=f"norm_proj_{rows}x{n_total}",
        compiler_params=_params("arbitrary"),
    )(x2d, g.reshape(1, d), w_all)


def _mem_kv_kernel(x_ref, g_ref, w_ref, kt_ref, vt_ref):
    _norm_proj_body(x_ref[...], g_ref, w_ref, (kt_ref, vt_ref), (X_WIDTH, X_WIDTH), (), True)


def mem_kv(mem, g_all, w_all):
    n_mem, d = mem.shape
    depth = w_all.shape[0]
    out = pl.BlockSpec((None, None, X_WIDTH, n_mem), lambda l: (l, 0, 0, 0))
    return pl.pallas_call(
        _mem_kv_kernel,
        grid=(depth,),
        in_specs=[pl.BlockSpec((n_mem, d), lambda l: (0, 0)),
                  pl.BlockSpec((None, 1, d), lambda l: (l, 0, 0)),
                  pl.BlockSpec((None, d, 2 * X_WIDTH), lambda l: (l, 0, 0))],
        out_specs=[out, out],
        out_shape=[jax.ShapeDtypeStruct((depth, 1, X_WIDTH, n_mem), F32)] * 2,
        name="mem_kv",
        compiler_params=_params("arbitrary"),
    )(mem, g_all.reshape(depth, 1, d), w_all)


def _stack_pair(q_pair):
    first = lax.broadcasted_iota(jnp.int32, (1, LANES), 1) < HD_A
    keep0 = jnp.where(first, 1.0, 0.0).astype(BF16)
    keep1 = jnp.where(first, 0.0, 1.0).astype(BF16)
    return jnp.concatenate([q_pair * keep0, q_pair * keep1], axis=0)


def _unstack_pair(o2):
    m = o2.shape[0] // 2
    first = lax.broadcasted_iota(jnp.int32, (m, LANES), 1) < HD_A
    return jnp.where(first, o2[:m], o2[m:])


def _pair_values(p, v_ext):
    r = jnp.dot(p, v_ext, preferred_element_type=F32)
    return _unstack_pair(r[:, :LANES] / r[:, LANES:])


def _pair_values_t(p, vt_ext):
    r = lax.dot_general(p, vt_ext, CONTRACT_LAST, preferred_element_type=F32)
    return _unstack_pair(r[:, :LANES] / r[:, LANES:])


def _band_attn_kernel(q_ref, k_ref, v_ref, bias_ref, o_ref,
                      kwin, vext, s_scr, p_scr, *, n_chunks, unroll):
    step = pl.program_id(0)
    chunk, prev_rows = BAND_GROUP * CHUNK, PAST_BAND
    cur_rows = n_chunks * chunk
    band = prev_rows + chunk
    n_pairs = H_A // 2
    assert cur_rows == prev_rows

    @pl.when(step == 0)
    def _():
        kwin[...] = jnp.zeros_like(kwin)
        vext[...] = jnp.zeros_like(vext)

    kwin[0:prev_rows, :] = kwin[prev_rows:prev_rows + cur_rows, :]
    kwin[prev_rows:prev_rows + cur_rows, :] = k_ref[...]
    ones = jnp.ones((prev_rows + cur_rows, LANES), BF16)
    for hp in range(n_pairs):
        lanes = slice(LANES * hp, LANES * (hp + 1))
        vcol = slice(2 * LANES * hp, 2 * LANES * hp + LANES)
        vext[0:prev_rows, vcol] = vext[prev_rows:prev_rows + cur_rows, vcol]
        vext[prev_rows:prev_rows + cur_rows, vcol] = v_ref[:, lanes]
        vext[:, 2 * LANES * hp + LANES:2 * LANES * (hp + 1)] = ones

    def chunks_body(ci, carry):
        starts = [pl.multiple_of((ci * unroll + u) * chunk, chunk) for u in range(unroll)]
        for u, r0 in enumerate(starts):
            for hp in range(n_pairs):
                lanes = slice(LANES * hp, LANES * (hp + 1))
                q2 = _stack_pair(q_ref[pl.ds(r0, chunk), lanes] * (HD_A ** -0.5))
                s_scr[u * n_pairs + hp] = lax.dot_general(
                    q2, kwin[pl.ds(r0, band), lanes], CONTRACT_LAST, preferred_element_type=F32)
        for u, r0 in enumerate(starts):
            key_row = lax.broadcasted_iota(jnp.int32, (1, band), 1) + r0 + (step - 1) * prev_rows
            neg = jnp.where(key_row < 0, NEG, 0.0)
            for hp in range(n_pairs):
                s = s_scr[u * n_pairs + hp] + bias_ref[hp] + neg
                m = jnp.max(s, axis=-1, keepdims=True)
                p_scr[u * n_pairs + hp] = jnp.exp(s - m).astype(BF16)
        for u, r0 in enumerate(starts):
            for hp in range(n_pairs):
                o_ref[pl.ds(r0, chunk), LANES * hp:LANES * (hp + 1)] = _pair_values(
                    p_scr[u * n_pairs + hp],
                    vext[pl.ds(r0, band), 2 * LANES * hp:2 * LANES * (hp + 1)]
                ).astype(o_ref.dtype)
        return carry

    lax.fori_loop(0, n_chunks // unroll, chunks_body, 0)


def _group_bias(bias):
    band = PAST_BAND + CHUNK
    rows = []
    for c in range(BAND_GROUP):
        pad = ((0, 0), (0, 0), (c * CHUNK, (BAND_GROUP - 1 - c) * CHUNK))
        rows.append(jnp.pad(bias, pad, constant_values=NEG))
    grouped = jnp.concatenate(rows, axis=1)
    return grouped.reshape(H_A // 2, 2 * BAND_GROUP * CHUNK, band + (BAND_GROUP - 1) * CHUNK)


def band_attn_prompt(q, k, v, bias, block_rows):
    rows = q.shape[0]
    group_rows = BAND_GROUP * CHUNK
    n_chunks = block_rows // group_rows
    n_pairs = H_A // 2
    band = PAST_BAND + group_rows
    assert block_rows == PAST_BAND and n_chunks % BAND_UNROLL == 0
    bias = _group_bias(bias)
    cur = pl.BlockSpec((block_rows, MIX_WIDTH), lambda i: (i, 0))
    return pl.pallas_call(
        functools.partial(_band_attn_kernel, n_chunks=n_chunks, unroll=BAND_UNROLL),
        grid=(rows // block_rows,),
        in_specs=[cur, cur, cur, pl.BlockSpec(bias.shape, lambda i: (0, 0, 0))],
        out_specs=cur,
        out_shape=jax.ShapeDtypeStruct((rows, MIX_WIDTH), BF16),
        scratch_shapes=[pltpu.VMEM((PAST_BAND + block_rows, MIX_WIDTH), BF16),
                        pltpu.VMEM((PAST_BAND + block_rows, 2 * MIX_WIDTH), BF16),
                        pltpu.VMEM((BAND_UNROLL * n_pairs, 2 * group_rows, band), F32),
                        pltpu.VMEM((BAND_UNROLL * n_pairs, 2 * group_rows, band), BF16)],
        name="band_attn_prompt",
        compiler_params=_params("arbitrary"),
    )(q, k, v, bias)


def _band_sample_kernel(q_ref, k_ref, v_ref, kct_ref, vct_ref, bias_ref, o_ref, s_scr, p_scr,
                        *, cache_len):
    n_pairs = H_A // 2
    n_par, s_len, _ = q_ref.shape
    items = [(b, hp) for b in range(n_par) for hp in range(n_pairs)]
    for i, (b, hp) in enumerate(items):
        lanes = slice(LANES * hp, LANES * (hp + 1))
        q2 = _stack_pair(q_ref[b, :, lanes] * (HD_A ** -0.5))
        kct = kct_ref[b, 2 * hp:2 * hp + 2].reshape(LANES, cache_len).astype(BF16)
        s_scr[i, :, 0:cache_len] = jnp.dot(q2, kct, preferred_element_type=F32)
        s_scr[i, :, cache_len:] = lax.dot_general(q2, k_ref[b, :, lanes], CONTRACT_LAST,
                                                  preferred_element_type=F32)
    for i, (b, hp) in enumerate(items):
        s = s_scr[i] + bias_ref[hp]
        p_scr[i] = jnp.exp(s - jnp.max(s, axis=-1, keepdims=True)).astype(BF16)
    ones_old = jnp.ones((LANES, cache_len), BF16)
    ones_new = jnp.ones((s_len, LANES), BF16)
    for i, (b, hp) in enumerate(items):
        lanes = slice(LANES * hp, LANES * (hp + 1))
        vct = vct_ref[b, 2 * hp:2 * hp + 2].reshape(LANES, cache_len).astype(BF16)
        r = (lax.dot_general(p_scr[i, :, 0:cache_len], jnp.concatenate([vct, ones_old], axis=0),
                             CONTRACT_LAST, preferred_element_type=F32)
             + jnp.dot(p_scr[i, :, cache_len:],
                       jnp.concatenate([v_ref[b, :, lanes], ones_new], axis=1),
                       preferred_element_type=F32))
        o_ref[b, :, lanes] = _unstack_pair(r[:, :LANES] / r[:, LANES:]).astype(o_ref.dtype)


def band_attn_sample(q, k, v, k_cache_t, v_cache_t, layer, bias, n_par):
    b, s_len, _ = q.shape
    cache_len = k_cache_t.shape[-1]
    n_items = n_par * (H_A // 2)
    new = pl.BlockSpec((n_par, s_len, MIX_WIDTH), lambda i: (i, 0, 0))
    old = pl.BlockSpec((None, n_par, H_A, HD_A, cache_len), lambda i: (layer, i, 0, 0, 0))
    return pl.pallas_call(
        functools.partial(_band_sample_kernel, cache_len=cache_len),
        grid=(b // n_par,),
        in_specs=[new, new, new, old, old, pl.BlockSpec(bias.shape, lambda i: (0, 0, 0))],
        out_specs=new,
        out_shape=jax.ShapeDtypeStruct((b, s_len, MIX_WIDTH), BF16),
        scratch_shapes=[pltpu.VMEM((n_items, 2 * s_len, cache_len + s_len), F32),
                        pltpu.VMEM((n_items, 2 * s_len, cache_len + s_len), BF16)],
        name="band_attn_sample",
        compiler_params=_params("arbitrary"),
    )(q, k, v, k_cache_t, v_cache_t, bias)


def _hgrn2_constants(t):
    halves = []
    h = t // 2
    while h >= 1:
        halves.append(h)
        h //= 2
    n_lvl = len(halves)
    w = np.zeros((n_lvl * t, t), np.float32)
    masks = np.zeros((n_lvl + 1, t, t), np.float32)
    for row in range(t):
        w[row, :row + 1] = 1.0
    for li, h in enumerate(halves):
        base = (1 + li) * t
        for row in range(t):
            r = (row // (2 * h)) * 2 * h + h - 1
            if h == 1:
                continue
            if row > r:
                w[base + row, r + 1:row + 1] = 1.0
            else:
                w[base + row, row + 1:r + 1] = 1.0
        for tq in range(t):
            for sk in range(t):
                if tq // (2 * h) == sk // (2 * h):
                    r = (tq // (2 * h)) * 2 * h + h - 1
                    if tq > r and sk <= r:
                        masks[li, tq, sk] = 1.0
    masks[n_lvl] = np.eye(t, dtype=np.float32)
    return jnp.asarray(np.tile(w, (1, 2)), BF16), jnp.asarray(masks, F32), n_lvl


def _hgrn2_kernel(a_ref, b_ref, c_ref, s0_ref, lbc_ref, g_ref, w_ref, m_ref, o_ref, sout_ref,
                st_ref, lf_scr, kk_scr, l2f_scr, sums_scr, qs_scr, ks_scr, attn_scr, u_scr, oi_scr,
                *, t, n_lvl, n_blk, chained):
    inter = n_lvl + 1
    ci = pl.program_id(0)

    if chained:
        @pl.when(ci == 0)
        def _():
            for h in range(H_B):
                st_ref[h] = s0_ref[h].T

    def blk(g):
        return slice(g * t, (g + 1) * t)

    def head(h):
        return slice(LANES * h, LANES * (h + 1))

    z = b_ref[...]
    e_neg = jnp.exp2(jnp.abs(z) * (-LOG2_E))
    u = lbc_ref[1:2, :] + jnp.minimum(z, 0.0) * LOG2_E - jnp.log2(1.0 + e_neg)
    log2_lb = lbc_ref[0:1, :]
    log2_f = jnp.maximum(log2_lb, u) + jnp.log2(1.0 + jnp.exp2(-jnp.abs(log2_lb - u)))
    kk_scr[...] = lbc_ref[2:3, :] * jnp.where(z > 0.0, e_neg, 1.0) / (1.0 + e_neg)
    l2f_scr[...] = log2_f
    hi = log2_f.astype(BF16)
    lo = (log2_f - hi.astype(F32)).astype(BF16)
    for g in range(n_blk):
        lf_scr[g, 0:t] = hi[blk(g)]
        lf_scr[g, t:2 * t] = lo[blk(g)]

    for g in range(n_blk):
        sums_scr[g] = jnp.dot(w_ref[...], lf_scr[g], preferred_element_type=F32)

    odd_row = (lax.broadcasted_iota(jnp.int32, (t, 1), 0) & 1) == 1
    for g in range(n_blk):
        q = a_ref[blk(g), :].astype(F32)
        kk = kk_scr[blk(g), :]

        def put(slot, e):
            qs_scr[g, slot] = (q * e).astype(BF16)
            ks_scr[g, slot] = (kk * e).astype(BF16)

        for li in range(n_lvl - 1):
            put(li, jnp.exp2(sums_scr[g, (1 + li) * t:(2 + li) * t]))
        put(n_lvl - 1, jnp.exp2(jnp.where(odd_row, l2f_scr[blk(g), :], 0.0)))
        qs_scr[g, n_lvl] = a_ref[blk(g), :]
        ks_scr[g, n_lvl] = kk.astype(BF16)
        a_cum = sums_scr[g, 0:t]
        qs_scr[g, inter] = (q * jnp.exp2(a_cum)).astype(BF16)
        ks_scr[g, inter] = (kk * jnp.exp2(a_cum[t - 1:t] - a_cum)).astype(BF16)

    in_level = [m_ref[li] != 0.0 for li in range(n_lvl + 1)]
    for g in range(n_blk):
        for h in range(H_B):
            acc = jnp.zeros((t, t), F32)
            for li in range(n_lvl + 1):
                part = lax.dot_general(
                    qs_scr[g, li, :, head(h)], ks_scr[g, li, :, head(h)],
                    CONTRACT_LAST, preferred_element_type=F32)
                acc = jnp.where(in_level[li], part, acc)
            attn_scr[g, h] = acc.astype(BF16)

    for g in range(n_blk):
        for h in range(H_B):
            iv = c_ref[blk(g), head(h)]
            oi_scr[blk(g), head(h)] = jnp.dot(attn_scr[g, h], iv, preferred_element_type=F32)
            u_scr[g, h] = jnp.dot(iv.astype(F32).T.astype(BF16), ks_scr[g, inter, :, head(h)],
                                  preferred_element_type=F32)

    for g in range(n_blk):
        e_last = jnp.exp2(sums_scr[g, t - 1:t, :])
        for h in range(H_B):
            st = st_ref[h] if chained else s0_ref[g, h].T
            o = oi_scr[blk(g), head(h)] + lax.dot_general(
                qs_scr[g, inter, :, head(h)], st.astype(BF16), CONTRACT_LAST,
                preferred_element_type=F32)
            st_new = e_last[:, head(h)] * st + u_scr[g, h]
            if chained:
                st_ref[h] = st_new
            else:
                sout_ref[g, h] = st_new.T
            ms = jnp.mean(o * o, axis=-1, keepdims=True)
            o_ref[blk(g), head(h)] = (o * lax.rsqrt(ms + EPS) * g_ref[:, head(h)]).astype(o_ref.dtype)

    if chained:
        @pl.when(ci == pl.num_programs(0) - 1)
        def _():
            for h in range(H_B):
                sout_ref[h] = st_ref[h].T


def _hgrn2_scratch(t, n_lvl, n_blk):
    step_rows = n_blk * t
    return [pltpu.VMEM((H_B, DK_B, DK_B), F32),
            pltpu.VMEM((n_blk, 2 * t, MIX_WIDTH), BF16),
            pltpu.VMEM((step_rows, MIX_WIDTH), F32),
            pltpu.VMEM((step_rows, MIX_WIDTH), F32),
            pltpu.VMEM((n_blk, n_lvl * t, MIX_WIDTH), F32),
            pltpu.VMEM((n_blk, n_lvl + 2, t, MIX_WIDTH), BF16),
            pltpu.VMEM((n_blk, n_lvl + 2, t, MIX_WIDTH), BF16),
            pltpu.VMEM((n_blk, H_B, t, t), BF16),
            pltpu.VMEM((n_blk, H_B, DK_B, DK_B), F32),
            pltpu.VMEM((step_rows, MIX_WIDTH), F32)]


def hgrn2(a, b, c, s0, layer, lbc, g, t, n_blk, chained):
    rows = a.shape[0]
    w, masks, n_lvl = _hgrn2_constants(t)
    step_rows = n_blk * t
    n_seq = 1 if chained else rows // t
    tok = pl.BlockSpec((step_rows, MIX_WIDTH), lambda i: (i, 0))
    if chained:
        state_in = pl.BlockSpec((None, None, H_B, DK_B, DK_B), lambda i: (layer, 0, 0, 0, 0))
        state_out = pl.BlockSpec((None, H_B, DK_B, DK_B), lambda i: (0, 0, 0, 0))
    else:
        state_in = pl.BlockSpec((None, n_blk, H_B, DK_B, DK_B), lambda i: (layer, i, 0, 0, 0))
        state_out = pl.BlockSpec((n_blk, H_B, DK_B, DK_B), lambda i: (i, 0, 0, 0))

    def whole(arr):
        return pl.BlockSpec(arr.shape, lambda i: (0,) * arr.ndim)

    return pl.pallas_call(
        functools.partial(_hgrn2_kernel, t=t, n_lvl=n_lvl, n_blk=n_blk, chained=chained),
        grid=(rows // step_rows,),
        in_specs=[tok, tok, tok, state_in, whole(lbc),
                  pl.BlockSpec((1, MIX_WIDTH), lambda i: (0, 0)), whole(w), whole(masks)],
        out_specs=[tok, state_out],
        out_shape=[jax.ShapeDtypeStruct((rows, MIX_WIDTH), BF16),
                   jax.ShapeDtypeStruct((n_seq, H_B, DK_B, DK_B), F32)],
        scratch_shapes=_hgrn2_scratch(t, n_lvl, n_blk),
        name=f"hgrn2_t{t}",
        compiler_params=_params("arbitrary"),
    )(a, b, c, s0, lbc, g.reshape(1, MIX_WIDTH), w, masks)


def _post_body(x_ref, mix_ref, xq_ref, gate_ref, mkt_ref, mvt_ref, w_ref, s_scr, p_scr, cross_scr,
               n_seg):
    seg_rows = x_ref.shape[0] // n_seg
    n_pairs = H_X // 2
    for seg in range(n_seg):
        rows = slice(seg * seg_rows, (seg + 1) * seg_rows)
        for hp in range(n_pairs):
            lanes = slice(LANES * hp, LANES * (hp + 1))
            q2 = _stack_pair(xq_ref[rows, lanes] * (HD_X ** -0.5))
            s_scr[seg * n_pairs + hp] = jnp.dot(q2, mkt_ref[seg, lanes, :].astype(BF16),
                                                preferred_element_type=F32)
    for i in range(n_seg * n_pairs):
        s = s_scr[i]
        p_scr[i] = jnp.exp(s - jnp.max(s, axis=-1, keepdims=True)).astype(BF16)
    ones = jnp.ones((LANES, N_MEM), BF16)
    for seg in range(n_seg):
        rows = slice(seg * seg_rows, (seg + 1) * seg_rows)
        for hp in range(n_pairs):
            lanes = slice(LANES * hp, LANES * (hp + 1))
            cross_scr[rows, lanes] = _pair_values_t(
                p_scr[seg * n_pairs + hp],
                jnp.concatenate([mvt_ref[seg, lanes, :].astype(BF16), ones], axis=0))

    gate = gate_ref[...]
    sg = gate / (1.0 + jnp.exp(-gate))
    y_mix = (mix_ref[...].astype(F32) * sg[:, 0:MIX_WIDTH]).astype(BF16)
    y_cross = (cross_scr[...] * sg[:, MIX_WIDTH:]).astype(BF16)
    return (x_ref[...]
            + jnp.dot(y_mix, w_ref[0:MIX_WIDTH, :].astype(BF16), preferred_element_type=F32)
            + jnp.dot(y_cross, w_ref[MIX_WIDTH:, :].astype(BF16), preferred_element_type=F32))


def _post_final_kernel(x_ref, mix_ref, xq_ref, gate_ref, mkt_ref, mvt_ref, w_ref, fg_ref, o_ref,
                       s_scr, p_scr, cross_scr, *, n_seg):
    acc = _post_body(x_ref, mix_ref, xq_ref, gate_ref, mkt_ref, mvt_ref, w_ref,
                     s_scr, p_scr, cross_scr, n_seg)
    ms = jnp.mean(acc * acc, axis=-1, keepdims=True)
    o_ref[...] = acc * lax.rsqrt(ms + EPS) * fg_ref[...]


def _post_pre_kernel(x_ref, mix_ref, xq_ref, gate_ref, mkt_ref, mvt_ref, wo_ref, g_ref, wi_ref,
                     xo_ref, *rest, n_seg, splits, tails):
    n_out = len(splits) + len(tails)
    s_scr, p_scr, cross_scr = rest[n_out:]
    acc = _post_body(x_ref, mix_ref, xq_ref, gate_ref, mkt_ref, mvt_ref, wo_ref,
                     s_scr, p_scr, cross_scr, n_seg)
    xo_ref[...] = acc
    _norm_proj_body(acc, g_ref, wi_ref, rest[:n_out], splits, tails, False)


def _post_specs(tm, n_seg, mem_layer, w_layer, mem_per_tile):
    def tok(n):
        return pl.BlockSpec((tm, n), lambda i: (i, 0))

    mem = pl.BlockSpec((None, n_seg, X_WIDTH, N_MEM),
                       lambda i: (mem_layer, i if mem_per_tile else 0, 0, 0))
    w_out = pl.BlockSpec((None, D_INNER, D_MODEL), lambda i: (w_layer, 0, 0),
                         pipeline_mode=pl.Buffered(1))
    in_specs = [tok(D_MODEL), tok(MIX_WIDTH), tok(X_WIDTH), tok(D_INNER), mem, mem, w_out]
    seg_rows = tm // n_seg
    n_items = n_seg * (H_X // 2)
    scratch = [pltpu.VMEM((n_items, 2 * seg_rows, N_MEM), F32),
               pltpu.VMEM((n_items, 2 * seg_rows, N_MEM), BF16),
               pltpu.VMEM((tm, X_WIDTH), F32)]
    return in_specs, scratch, tok(D_MODEL)


def post_final(x, mix, xq, gate, mkt, mvt, mem_layer, w_all, w_layer, final_g, tm, n_seg,
               mem_per_tile):
    rows = x.shape[0]
    in_specs, scratch, x_spec = _post_specs(tm, n_seg, mem_layer, w_layer, mem_per_tile)
    return pl.pallas_call(
        functools.partial(_post_final_kernel, n_seg=n_seg),
        grid=(rows // tm,),
        in_specs=in_specs + [pl.BlockSpec((1, D_MODEL), lambda i: (0, 0))],
        out_specs=x_spec,
        out_shape=jax.ShapeDtypeStruct((rows, D_MODEL), F32),
        scratch_shapes=scratch,
        name=f"post_final_{rows}",
        compiler_params=_params("arbitrary"),
    )(x, mix, xq, gate, mkt, mvt, w_all, final_g.reshape(1, D_MODEL))


def post_pre(x, mix, xq, gate, mkt, mvt, mem_layer, w_out_all, layer, g_next, w_in_all,
             dtypes, tails, tm, n_seg, mem_per_tile):
    rows = x.shape[0]
    in_specs, scratch, x_spec = _post_specs(tm, n_seg, mem_layer, layer, mem_per_tile)
    in_specs += [pl.BlockSpec((1, D_MODEL), lambda i: (0, 0)),
                 pl.BlockSpec((None, D_MODEL, w_in_all.shape[2]), lambda i: (layer + 1, 0, 0),
                              pipeline_mode=pl.Buffered(1))]
    proj_specs, proj_shapes = _proj_out_specs(rows, tm, PROJ_SPLITS, dtypes, tails)
    return pl.pallas_call(
        functools.partial(_post_pre_kernel, n_seg=n_seg, splits=PROJ_SPLITS, tails=tuple(tails)),
        grid=(rows // tm,),
        in_specs=in_specs,
        out_specs=[x_spec] + proj_specs,
        out_shape=[jax.ShapeDtypeStruct((rows, D_MODEL), F32)] + proj_shapes,
        scratch_shapes=scratch,
        name=f"post_pre_{rows}",
        compiler_params=_params("arbitrary"),
    )(x, mix, xq, gate, mkt, mvt, w_out_all, g_next.reshape(1, D_MODEL), w_in_all)


def _rel_bias(table):
    band = PAST_BAND + CHUNK
    n_diag = band + CHUNK - 1
    offs = np.arange(n_diag) - (CHUNK - 1)
    idx = np.clip(PAST_BAND - offs, -REL_CLIP, REL_CLIP) + REL_CLIP
    diag = jnp.pad(table[:, idx].astype(F32), ((0, 0), (0, 1)))
    skew = jnp.tile(diag, (1, CHUNK))[:, :CHUNK * n_diag].reshape(H_A, CHUNK, n_diag)
    return skew[:, :, CHUNK - 1:CHUNK - 1 + band]


def _per_head_transposed(cache):
    return jnp.moveaxis(cache, -3, -1)


def kernel(x_prompt, x_sample, cache_a_k, cache_a_v, state_b, cache_mem_k, cache_mem_v, mem_prompt,
           ln_g, w_in, w_out, rel_bias_table, lower_bounds, hgrn_norm_g, mem_norm_g, w_mem_kv, final_g):
    bp, seq, _ = x_prompt.shape
    bs, dec_seq, _ = x_sample.shape
    assert bp == 1
    cache_len = cache_a_k.shape[2]
    n_s = bs * dec_seq
    keep = min(PAST_BAND, seq)

    w_in_b = w_in.astype(BF16)

    lb_all = jnp.cumsum(jax.nn.softmax(lower_bounds.astype(F32), axis=0), axis=0)
    lb_all = lb_all - lb_all[:1]

    cache_a_kt = _per_head_transposed(cache_a_k)
    cache_a_vt = _per_head_transposed(cache_a_v)
    cache_mem_kt = _per_head_transposed(cache_mem_k).reshape(DEPTH, bs, X_WIDTH, N_MEM)
    cache_mem_vt = _per_head_transposed(cache_mem_v).reshape(DEPTH, bs, X_WIDTH, N_MEM)
    zero_state = jnp.zeros((1, 1, H_B, DK_B, DK_B), F32)

    sample_tiles = (n_s // 2, bs // 2)

    def per_batch(u):
        return u.reshape(bs, dec_seq, u.shape[-1])

    def layer_io(l):
        attn_layer = l % 2 == 0
        return (BF16, BF16 if attn_layer else F32, BF16, BF16, F32), (1, 2) if attn_layer else ()

    mkt_all, mvt_all = mem_kv(mem_prompt.reshape(N_MEM, D_MODEL), mem_norm_g, w_mem_kv)
    mem_k_prompt = jnp.moveaxis(mkt_all.reshape(DEPTH, 1, H_X, HD_X, N_MEM), -1, 2)
    mem_v_prompt = jnp.moveaxis(mvt_all.reshape(DEPTH, 1, H_X, HD_X, N_MEM), -1, 2)

    xp = x_prompt.reshape(seq, D_MODEL)
    xs = x_sample.reshape(n_s, D_MODEL)
    dtypes, tails = layer_io(0)
    outs_p = norm_proj(xp, ln_g[0], w_in_b, 0, PROJ_SPLITS, dtypes, PAST_BAND, tails)
    outs_s = norm_proj(xs, ln_g[0], w_in_b, 0, PROJ_SPLITS, dtypes, n_s, tails)
    ak_p, av_p, sb_p, ak_s, av_s, sb_s = [], [], [], [], [], []
    for l in range(DEPTH):
        j = l // 2
        a_p, b_p, c_p, xq_p, g_p = outs_p[:5]
        a_s, b_s, c_s, xq_s, g_s = outs_s[:5]

        if l % 2 == 0:
            bias = _rel_bias(rel_bias_table[j])
            o_p = band_attn_prompt(a_p, b_p, c_p, bias, PAST_BAND)
            o_s = band_attn_sample(per_batch(a_s), per_batch(b_s), per_batch(c_s),
                                   cache_a_kt, cache_a_vt, j,
                                   bias[:, :dec_seq, :cache_len + dec_seq].reshape(
                                       H_A // 2, 2 * dec_seq, cache_len + dec_seq),
                                   BAND_SEQS_PER_STEP).reshape(n_s, MIX_WIDTH)
            assert keep == PAST_BAND
            ak_p.append(outs_p[5].reshape(1, keep, H_A, HD_A))
            av_p.append(outs_p[6].reshape(1, keep, H_A, HD_A))
            ak_s.append(outs_s[5].reshape(bs, dec_seq, H_A, HD_A))
            av_s.append(outs_s[6].reshape(bs, dec_seq, H_A, HD_A))
        else:
            lb = lb_all[j]
            lbc = jnp.stack([jnp.log2(lb), jnp.log1p(-lb) * LOG2_E, 1.0 - lb])
            o_p, s_p = hgrn2(a_p, b_p, c_p, zero_state, 0, lbc, hgrn_norm_g[j],
                             CHUNK, HGRN_BLOCKS_PER_STEP, True)
            o_s, s_s = hgrn2(a_s, b_s, c_s, state_b.astype(F32), j, lbc, hgrn_norm_g[j],
                             dec_seq, HGRN_SEQS_PER_STEP, False)
            sb_p.append(s_p)
            sb_s.append(s_s)
        mem_p = (mkt_all, mvt_all, l)
        mem_s = (cache_mem_kt, cache_mem_vt, l)
        if l == DEPTH - 1:
            xp = post_final(xp, o_p, xq_p, g_p, *mem_p, w_out, l, final_g, PAST_BAND, 1, False)
            xs = post_final(xs, o_s, xq_s, g_s, *mem_s, w_out, l, final_g, *sample_tiles, True)
        else:
            dtypes, tails = layer_io(l + 1)
            xp, *outs_p = post_pre(xp, o_p, xq_p, g_p, *mem_p, w_out, l, ln_g[l + 1], w_in_b,
                                   dtypes, tails, PAST_BAND, 1, False)
            xs, *outs_s = post_pre(xs, o_s, xq_s, g_s, *mem_s, w_out, l, ln_g[l + 1], w_in_b,
                                   dtypes, tails, *sample_tiles, True)
    return (xp.reshape(1, seq, D_MODEL), xs.reshape(bs, dec_seq, D_MODEL),
            jnp.stack(ak_p), jnp.stack(av_p), jnp.stack(sb_p), mem_k_prompt, mem_v_prompt,
            jnp.stack(ak_s), jnp.stack(av_s), jnp.stack(sb_s))
```

```python
import functools

import numpy as np
import jax
import jax.numpy as jnp
from jax import lax
from jax.experimental import pallas as pl
from jax.experimental.pallas import tpu as pltpu

D_MODEL = 1024
DEPTH = 4
CHUNK = 64
N_PAST_CHUNKS = 8
PAST_BAND = N_PAST_CHUNKS * CHUNK
MIX_WIDTH = 768
X_WIDTH = 256
D_INNER = MIX_WIDTH + X_WIDTH
HD_A = 64
H_A = MIX_WIDTH // HD_A
REL_CLIP = 128
DK_B = 128
H_B = MIX_WIDTH // DK_B
H_X = 4
HD_X = 64
N_MEM = 256
EPS = 1e-6
NEG = -1e30
LOG2_E = 1.4426950408889634
F32 = jnp.float32
BF16 = jnp.bfloat16

LANES = 128
VMEM_LIMIT_BYTES = 56 * 1024 * 1024
PROJ_SPLITS = (MIX_WIDTH, MIX_WIDTH, MIX_WIDTH, X_WIDTH, D_INNER)
CONTRACT_LAST = (((1,), (1,)), ((), ()))
HGRN_BLOCKS_PER_STEP = 8
HGRN_SEQS_PER_STEP = 8
BAND_SEQS_PER_STEP = 4
BAND_GROUP = 2
BAND_UNROLL = 4


def _params(*sem):
    return pltpu.CompilerParams(dimension_semantics=sem, vmem_limit_bytes=VMEM_LIMIT_BYTES)


def _norm_proj_body(x, g_ref, w_ref, out_refs, splits, tails, transpose):
    ms = jnp.mean(x * x, axis=-1, keepdims=True)
    xn = (x * lax.rsqrt(ms + EPS) * g_ref[...]).astype(BF16)
    tail_refs = out_refs[len(splits):]
    off = 0
    for idx, (o_ref, n) in enumerate(zip(out_refs, splits)):
        r = jnp.dot(xn, w_ref[:, off:off + n].astype(BF16), preferred_element_type=F32)
        o_ref[...] = (r.T if transpose else r).astype(o_ref.dtype)
        if idx in tails:
            tail_refs[tails.index(idx)][...] = r
        off += n


def _norm_proj_kernel(x_ref, g_ref, w_ref, *out_refs, splits, tails):
    _norm_proj_body(x_ref[...], g_ref, w_ref, out_refs, splits, tails, False)


def _proj_out_specs(rows, tm, splits, dtypes, tails):
    out_specs = [pl.BlockSpec((tm, n), lambda i: (i, 0)) for n in splits]
    out_shape = [jax.ShapeDtypeStruct((rows, n), dt) for n, dt in zip(splits, dtypes)]
    tail_rows = min(PAST_BAND, rows)
    first_tail = (rows - tail_rows) // tm
    out_specs += [pl.BlockSpec((tm, splits[idx]), lambda i: (jnp.maximum(i - first_tail, 0), 0))
                  for idx in tails]
    out_shape += [jax.ShapeDtypeStruct((tail_rows, splits[idx]), F32) for idx in tails]
    return out_specs, out_shape


def norm_proj(x2d, g, w_all, layer, splits, dtypes, tm, tails=()):
    rows, d = x2d.shape
    n_total = w_all.shape[2]
    assert rows % tm == 0
    out_specs, out_shape = _proj_out_specs(rows, tm, splits, dtypes, tails)
    return pl.pallas_call(
        functools.partial(_norm_proj_kernel, splits=splits, tails=tuple(tails)),
        grid=(rows // tm,),
        in_specs=[
            pl.BlockSpec((tm, d), lambda i: (i, 0)),
            pl.BlockSpec((1, d), lambda i: (0, 0)),
            pl.BlockSpec((None, d, n_total), lambda i: (layer, 0, 0)),
        ],
        out_specs=out_specs,
        out_shape=out_shape,
        name=f"norm_proj_{rows}x{n_total}",
        compiler_params=_params("arbitrary"),
    )(x2d, g.reshape(1, d), w_all)


def _mem_kv_kernel(x_ref, g_ref, w_ref, kt_ref, vt_ref):
    _norm_proj_body(x_ref[...], g_ref, w_ref, (kt_ref, vt_ref), (X_WIDTH, X_WIDTH), (), True)


def mem_kv(mem, g_all, w_all):
    n_mem, d = mem.shape
    depth = w_all.shape[0]
    out = pl.BlockSpec((None, None, X_WIDTH, n_mem), lambda l: (l, 0, 0, 0))
    return pl.pallas_call(
        _mem_kv_kernel,
        grid=(depth,),
        in_specs=[pl.BlockSpec((n_mem, d), lambda l: (0, 0)),
                  pl.BlockSpec((None, 1, d), lambda l: (l, 0, 0)),
                  pl.BlockSpec((None, d, 2 * X_WIDTH), lambda l: (l, 0, 0))],
        out_specs=[out, out],
        out_shape=[jax.ShapeDtypeStruct((depth, 1, X_WIDTH, n_mem), F32)] * 2,
        name="mem_kv",
        compiler_params=_params("arbitrary"),
    )(mem, g_all.reshape(depth, 1, d), w_all)


def _stack_pair(q_pair):
    first = lax.broadcasted_iota(jnp.int32, (1, LANES), 1) < HD_A
    keep0 = jnp.where(first, 1.0, 0.0).astype(BF16)
    keep1 = jnp.where(first, 0.0, 1.0).astype(BF16)
    return jnp.concatenate([q_pair * keep0, q_pair * keep1], axis=0)


def _unstack_pair(o2):
    m = o2.shape[0] // 2
    first = lax.broadcasted_iota(jnp.int32, (m, LANES), 1) < HD_A
    return jnp.where(first, o2[:m], o2[m:])


def _pair_values(p, v_ext):
    r = jnp.dot(p, v_ext, preferred_element_type=F32)
    return _unstack_pair(r[:, :LANES] / r[:, LANES:])


def _pair_values_t(p, vt_ext):
    r = lax.dot_general(p, vt_ext, CONTRACT_LAST, preferred_element_type=F32)
    return _unstack_pair(r[:, :LANES] / r[:, LANES:])


def _band_attn_kernel(q_ref, k_ref, v_ref, bias_ref, o_ref,
                      kwin, vext, s_scr, p_scr, *, n_chunks, unroll):
    step = pl.program_id(0)
    chunk, prev_rows = BAND_GROUP * CHUNK, PAST_BAND
    cur_rows = n_chunks * chunk
    band = prev_rows + chunk
    n_pairs = H_A // 2
    assert cur_rows == prev_rows

    @pl.when(step == 0)
    def _():
        kwin[...] = jnp.zeros_like(kwin)
        vext[...] = jnp.zeros_like(vext)

    kwin[0:prev_rows, :] = kwin[prev_rows:prev_rows + cur_rows, :]
    kwin[prev_rows:prev_rows + cur_rows, :] = k_ref[...]
    ones = jnp.ones((prev_rows + cur_rows, LANES), BF16)
    for hp in range(n_pairs):
        lanes = slice(LANES * hp, LANES * (hp + 1))
        vcol = slice(2 * LANES * hp, 2 * LANES * hp + LANES)
        vext[0:prev_rows, vcol] = vext[prev_rows:prev_rows + cur_rows, vcol]
        vext[prev_rows:prev_rows + cur_rows, vcol] = v_ref[:, lanes]
        vext[:, 2 * LANES * hp + LANES:2 * LANES * (hp + 1)] = ones

    def chunks_body(ci, carry):
        starts = [pl.multiple_of((ci * unroll + u) * chunk, chunk) for u in range(unroll)]
        for u, r0 in enumerate(starts):
            for hp in range(n_pairs):
                lanes = slice(LANES * hp, LANES * (hp + 1))
                q2 = _stack_pair(q_ref[pl.ds(r0, chunk), lanes] * (HD_A ** -0.5))
                s_scr[u * n_pairs + hp] = lax.dot_general(
                    q2, kwin[pl.ds(r0, band), lanes], CONTRACT_LAST, preferred_element_type=F32)
        for u, r0 in enumerate(starts):
            key_row = lax.broadcasted_iota(jnp.int32, (1, band), 1) + r0 + (step - 1) * prev_rows
            neg = jnp.where(key_row < 0, NEG, 0.0)
            for hp in range(n_pairs):
                s = s_scr[u * n_pairs + hp] + bias_ref[hp] + neg
                m = jnp.max(s, axis=-1, keepdims=True)
                p_scr[u * n_pairs + hp] = jnp.exp(s - m).astype(BF16)
        for u, r0 in enumerate(starts):
            for hp in range(n_pairs):
                o_ref[pl.ds(r0, chunk), LANES * hp:LANES * (hp + 1)] = _pair_values(
                    p_scr[u * n_pairs + hp],
                    vext[pl.ds(r0, band), 2 * LANES * hp:2 * LANES * (hp + 1)]
                ).astype(o_ref.dtype)
        return carry

    lax.fori_loop(0, n_chunks // unroll, chunks_body, 0)


def _group_bias(bias):
    band = PAST_BAND + CHUNK
    rows = []
    for c in range(BAND_GROUP):
        pad = ((0, 0), (0, 0), (c * CHUNK, (BAND_GROUP - 1 - c) * CHUNK))
        rows.append(jnp.pad(bias, pad, constant_values=NEG))
    grouped = jnp.concatenate(rows, axis=1)
    return grouped.reshape(H_A // 2, 2 * BAND_GROUP * CHUNK, band + (BAND_GROUP - 1) * CHUNK)


def band_attn_prompt(q, k, v, bias, block_rows):
    rows = q.shape[0]
    group_rows = BAND_GROUP * CHUNK
    n_chunks = block_rows // group_rows
    n_pairs = H_A // 2
    band = PAST_BAND + group_rows
    assert block_rows == PAST_BAND and n_chunks % BAND_UNROLL == 0
    bias = _group_bias(bias)
    cur = pl.BlockSpec((block_rows, MIX_WIDTH), lambda i: (i, 0))
    return pl.pallas_call(
        functools.partial(_band_attn_kernel, n_chunks=n_chunks, unroll=BAND_UNROLL),
        grid=(rows // block_rows,),
        in_specs=[cur, cur, cur, pl.BlockSpec(bias.shape, lambda i: (0, 0, 0))],
        out_specs=cur,
        out_shape=jax.ShapeDtypeStruct((rows, MIX_WIDTH), BF16),
        scratch_shapes=[pltpu.VMEM((PAST_BAND + block_rows, MIX_WIDTH), BF16),
                        pltpu.VMEM((PAST_BAND + block_rows, 2 * MIX_WIDTH), BF16),
                        pltpu.VMEM((BAND_UNROLL * n_pairs, 2 * group_rows, band), F32),
                        pltpu.VMEM((BAND_UNROLL * n_pairs, 2 * group_rows, band), BF16)],
        name="band_attn_prompt",
        compiler_params=_params("arbitrary"),
    )(q, k, v, bias)


def _band_sample_kernel(q_ref, k_ref, v_ref, kct_ref, vct_ref, bias_ref, o_ref, s_scr, p_scr,
                        *, cache_len):
    n_pairs = H_A // 2
    n_par, s_len, _ = q_ref.shape
    items = [(b, hp) for b in range(n_par) for hp in range(n_pairs)]
    for i, (b, hp) in enumerate(items):
        lanes = slice(LANES * hp, LANES * (hp + 1))
        q2 = _stack_pair(q_ref[b, :, lanes] * (HD_A ** -0.5))
        kct = kct_ref[b, 2 * hp:2 * hp + 2].reshape(LANES, cache_len).astype(BF16)
        s_scr[i, :, 0:cache_len] = jnp.dot(q2, kct, preferred_element_type=F32)
        s_scr[i, :, cache_len:] = lax.dot_general(q2, k_ref[b, :, lanes], CONTRACT_LAST,
                                                  preferred_element_type=F32)
    for i, (b, hp) in enumerate(items):
        s = s_scr[i] + bias_ref[hp]
        p_scr[i] = jnp.exp(s - jnp.max(s, axis=-1, keepdims=True)).astype(BF16)
    ones_old = jnp.ones((LANES, cache_len), BF16)
    ones_new = jnp.ones((s_len, LANES), BF16)
    for i, (b, hp) in enumerate(items):
        lanes = slice(LANES * hp, LANES * (hp + 1))
        vct = vct_ref[b, 2 * hp:2 * hp + 2].reshape(LANES, cache_len).astype(BF16)
        r = (lax.dot_general(p_scr[i, :, 0:cache_len], jnp.concatenate([vct, ones_old], axis=0),
                             CONTRACT_LAST, preferred_element_type=F32)
             + jnp.dot(p_scr[i, :, cache_len:],
                       jnp.concatenate([v_ref[b, :, lanes], ones_new], axis=1),
                       preferred_element_type=F32))
        o_ref[b, :, lanes] = _unstack_pair(r[:, :LANES] / r[:, LANES:]).astype(o_ref.dtype)


def band_attn_sample(q, k, v, k_cache_t, v_cache_t, layer, bias, n_par):
    b, s_len, _ = q.shape
    cache_len = k_cache_t.shape[-1]
    n_items = n_par * (H_A // 2)
    new = pl.BlockSpec((n_par, s_len, MIX_WIDTH), lambda i: (i, 0, 0))
    old = pl.BlockSpec((None, n_par, H_A, HD_A, cache_len), lambda i: (layer, i, 0, 0, 0))
    return pl.pallas_call(
        functools.partial(_band_sample_kernel, cache_len=cache_len),
        grid=(b // n_par,),
        in_specs=[new, new, new, old, old, pl.BlockSpec(bias.shape, lambda i: (0, 0, 0))],
        out_specs=new,
        out_shape=jax.ShapeDtypeStruct((b, s_len, MIX_WIDTH), BF16),
        scratch_shapes=[pltpu.VMEM((n_items, 2 * s_len, cache_len + s_len), F32),
                        pltpu.VMEM((n_items, 2 * s_len, cache_len + s_len), BF16)],
        name="band_attn_sample",
        compiler_params=_params("arbitrary"),
    )(q, k, v, k_cache_t, v_cache_t, bias)


def _hgrn2_constants(t):
    halves = []
    h = t // 2
    while h >= 1:
        halves.append(h)
        h //= 2
    n_lvl = len(halves)
    w = np.zeros((n_lvl * t, t), np.float32)
    masks = np.zeros((n_lvl + 1, t, t), np.float32)
    for row in range(t):
        w[row, :row + 1] = 1.0
    for li, h in enumerate(halves):
        base = (1 + li) * t
        for row in range(t):
            r = (row // (2 * h)) * 2 * h + h - 1
            if h == 1:
                continue
            if row > r:
                w[base + row, r + 1:row + 1] = 1.0
            else:
                w[base + row, row + 1:r + 1] = 1.0
        for tq in range(t):
            for sk in range(t):
                if tq // (2 * h) == sk // (2 * h):
                    r = (tq // (2 * h)) * 2 * h + h - 1
                    if tq > r and sk <= r:
                        masks[li, tq, sk] = 1.0
    masks[n_lvl] = np.eye(t, dtype=np.float32)
    return jnp.asarray(np.tile(w, (1, 2)), BF16), jnp.asarray(masks, F32), n_lvl


def _hgrn2_kernel(a_ref, b_ref, c_ref, s0_ref, lbc_ref, g_ref, w_ref, m_ref, o_ref, sout_ref,
                st_ref, lf_scr, kk_scr, l2f_scr, sums_scr, qs_scr, ks_scr, attn_scr, u_scr, oi_scr,
                *, t, n_lvl, n_blk, chained):
    inter = n_lvl + 1
    ci = pl.program_id(0)

    if chained:
        @pl.when(ci == 0)
        def _():
            for h in range(H_B):
                st_ref[h] = s0_ref[h].T

    def blk(g):
        return slice(g * t, (g + 1) * t)

    def head(h):
        return slice(LANES * h, LANES * (h + 1))

    z = b_ref[...]
    e_neg = jnp.exp2(jnp.abs(z) * (-LOG2_E))
    u = lbc_ref[1:2, :] + jnp.minimum(z, 0.0) * LOG2_E - jnp.log2(1.0 + e_neg)
    log2_lb = lbc_ref[0:1, :]
    log2_f = jnp.maximum(log2_lb, u) + jnp.log2(1.0 + jnp.exp2(-jnp.abs(log2_lb - u)))
    kk_scr[...] = lbc_ref[2:3, :] * jnp.where(z > 0.0, e_neg, 1.0) / (1.0 + e_neg)
    l2f_scr[...] = log2_f
    hi = log2_f.astype(BF16)
    lo = (log2_f - hi.astype(F32)).astype(BF16)
    for g in range(n_blk):
        lf_scr[g, 0:t] = hi[blk(g)]
        lf_scr[g, t:2 * t] = lo[blk(g)]

    for g in range(n_blk):
        sums_scr[g] = jnp.dot(w_ref[...], lf_scr[g], preferred_element_type=F32)

    odd_row = (lax.broadcasted_iota(jnp.int32, (t, 1), 0) & 1) == 1
    for g in range(n_blk):
        q = a_ref[blk(g), :].astype(F32)
        kk = kk_scr[blk(g), :]

        def put(slot, e):
            qs_scr[g, slot] = (q * e).astype(BF16)
            ks_scr[g, slot] = (kk * e).astype(BF16)

        for li in range(n_lvl - 1):
            put(li, jnp.exp2(sums_scr[g, (1 + li) * t:(2 + li) * t]))
        put(n_lvl - 1, jnp.exp2(jnp.where(odd_row, l2f_scr[blk(g), :], 0.0)))
        qs_scr[g, n_lvl] = a_ref[blk(g), :]
        ks_scr[g, n_lvl] = kk.astype(BF16)
        a_cum = sums_scr[g, 0:t]
        qs_scr[g, inter] = (q * jnp.exp2(a_cum)).astype(BF16)
        ks_scr[g, inter] = (kk * jnp.exp2(a_cum[t - 1:t] - a_cum)).astype(BF16)

    in_level = [m_ref[li] != 0.0 for li in range(n_lvl + 1)]
    for g in range(n_blk):
        for h in range(H_B):
            acc = jnp.zeros((t, t), F32)
            for li in range(n_lvl + 1):
                part = lax.dot_general(
                    qs_scr[g, li, :, head(h)], ks_scr[g, li, :, head(h)],
                    CONTRACT_LAST, preferred_element_type=F32)
                acc = jnp.where(in_level[li], part, acc)
            attn_scr[g, h] = acc.astype(BF16)

    for g in range(n_blk):
        for h in range(H_B):
            iv = c_ref[blk(g), head(h)]
            oi_scr[blk(g), head(h)] = jnp.dot(attn_scr[g, h], iv, preferred_element_type=F32)
            u_scr[g, h] = jnp.dot(iv.astype(F32).T.astype(BF16), ks_scr[g, inter, :, head(h)],
                                  preferred_element_type=F32)

    for g in range(n_blk):
        e_last = jnp.exp2(sums_scr[g, t - 1:t, :])
        for h in range(H_B):
            st = st_ref[h] if chained else s0_ref[g, h].T
            o = oi_scr[blk(g), head(h)] + lax.dot_general(
                qs_scr[g, inter, :, head(h)], st.astype(BF16), CONTRACT_LAST,
                preferred_element_type=F32)
            st_new = e_last[:, head(h)] * st + u_scr[g, h]
            if chained:
                st_ref[h] = st_new
            else:
                sout_ref[g, h] = st_new.T
            ms = jnp.mean(o * o, axis=-1, keepdims=True)
            o_ref[blk(g), head(h)] = (o * lax.rsqrt(ms + EPS) * g_ref[:, head(h)]).astype(o_ref.dtype)

    if chained:
        @pl.when(ci == pl.num_programs(0) - 1)
        def _():
            for h in range(H_B):
                sout_ref[h] = st_ref[h].T


def _hgrn2_scratch(t, n_lvl, n_blk):
    step_rows = n_blk * t
    return [pltpu.VMEM((H_B, DK_B, DK_B), F32),
            pltpu.VMEM((n_blk, 2 * t, MIX_WIDTH), BF16),
            pltpu.VMEM((step_rows, MIX_WIDTH), F32),
            pltpu.VMEM((step_rows, MIX_WIDTH), F32),
            pltpu.VMEM((n_blk, n_lvl * t, MIX_WIDTH), F32),
            pltpu.VMEM((n_blk, n_lvl + 2, t, MIX_WIDTH), BF16),
            pltpu.VMEM((n_blk, n_lvl + 2, t, MIX_WIDTH), BF16),
            pltpu.VMEM((n_blk, H_B, t, t), BF16),
            pltpu.VMEM((n_blk, H_B, DK_B, DK_B), F32),
            pltpu.VMEM((step_rows, MIX_WIDTH), F32)]


def hgrn2(a, b, c, s0, layer, lbc, g, t, n_blk, chained):
    rows = a.shape[0]
    w, masks, n_lvl = _hgrn2_constants(t)
    step_rows = n_blk * t
    n_seq = 1 if chained else rows // t
    tok = pl.BlockSpec((step_rows, MIX_WIDTH), lambda i: (i, 0))
    if chained:
        state_in = pl.BlockSpec((None, None, H_B, DK_B, DK_B), lambda i: (layer, 0, 0, 0, 0))
        state_out = pl.BlockSpec((None, H_B, DK_B, DK_B), lambda i: (0, 0, 0, 0))
    else:
        state_in = pl.BlockSpec((None, n_blk, H_B, DK_B, DK_B), lambda i: (layer, i, 0, 0, 0))
        state_out = pl.BlockSpec((n_blk, H_B, DK_B, DK_B), lambda i: (i, 0, 0, 0))

    def whole(arr):
        return pl.BlockSpec(arr.shape, lambda i: (0,) * arr.ndim)

    return pl.pallas_call(
        functools.partial(_hgrn2_kernel, t=t, n_lvl=n_lvl, n_blk=n_blk, chained=chained),
        grid=(rows // step_rows,),
        in_specs=[tok, tok, tok, state_in, whole(lbc),
                  pl.BlockSpec((1, MIX_WIDTH), lambda i: (0, 0)), whole(w), whole(masks)],
        out_specs=[tok, state_out],
        out_shape=[jax.ShapeDtypeStruct((rows, MIX_WIDTH), BF16),
                   jax.ShapeDtypeStruct((n_seq, H_B, DK_B, DK_B), F32)],
        scratch_shapes=_hgrn2_scratch(t, n_lvl, n_blk),
        name=f"hgrn2_t{t}",
        compiler_params=_params("arbitrary"),
    )(a, b, c, s0, lbc, g.reshape(1, MIX_WIDTH), w, masks)


def _post_body(x_ref, mix_ref, xq_ref, gate_ref, mkt_ref, mvt_ref, w_ref, s_scr, p_scr, cross_scr,
               n_seg):
    seg_rows = x_ref.shape[0] // n_seg
    n_pairs = H_X // 2
    for seg in range(n_seg):
        rows = slice(seg * seg_rows, (seg + 1) * seg_rows)
        for hp in range(n_pairs):
            lanes = slice(LANES * hp, LANES * (hp + 1))
            q2 = _stack_pair(xq_ref[rows, lanes] * (HD_X ** -0.5))
            s_scr[seg * n_pairs + hp] = jnp.dot(q2, mkt_ref[seg, lanes, :].astype(BF16),
                                                preferred_element_type=F32)
    for i in range(n_seg * n_pairs):
        s = s_scr[i]
        p_scr[i] = jnp.exp(s - jnp.max(s, axis=-1, keepdims=True)).astype(BF16)
    ones = jnp.ones((LANES, N_MEM), BF16)
    for seg in range(n_seg):
        rows = slice(seg * seg_rows, (seg + 1) * seg_rows)
        for hp in range(n_pairs):
            lanes = slice(LANES * hp, LANES * (hp + 1))
            cross_scr[rows, lanes] = _pair_values_t(
                p_scr[seg * n_pairs + hp],
                jnp.concatenate([mvt_ref[seg, lanes, :].astype(BF16), ones], axis=0))

    gate = gate_ref[...]
    sg = gate / (1.0 + jnp.exp(-gate))
    y_mix = (mix_ref[...].astype(F32) * sg[:, 0:MIX_WIDTH]).astype(BF16)
    y_cross = (cross_scr[...] * sg[:, MIX_WIDTH:]).astype(BF16)
    return (x_ref[...]
            + jnp.dot(y_mix, w_ref[0:MIX_WIDTH, :].astype(BF16), preferred_element_type=F32)
            + jnp.dot(y_cross, w_ref[MIX_WIDTH:, :].astype(BF16), preferred_element_type=F32))


def _post_final_kernel(x_ref, mix_ref, xq_ref, gate_ref, mkt_ref, mvt_ref, w_ref, fg_ref, o_ref,
                       s_scr, p_scr, cross_scr, *, n_seg):
    acc = _post_body(x_ref, mix_ref, xq_ref, gate_ref, mkt_ref, mvt_ref, w_ref,
                     s_scr, p_scr, cross_scr, n_seg)
    ms = jnp.mean(acc * acc, axis=-1, keepdims=True)
    o_ref[...] = acc * lax.rsqrt(ms + EPS) * fg_ref[...]


def _post_pre_kernel(x_ref, mix_ref, xq_ref, gate_ref, mkt_ref, mvt_ref, wo_ref, g_ref, wi_ref,
                     xo_ref, *rest, n_seg, splits, tails):
    n_out = len(splits) + len(tails)
    s_scr, p_scr, cross_scr = rest[n_out:]
    acc = _post_body(x_ref, mix_ref, xq_ref, gate_ref, mkt_ref, mvt_ref, wo_ref,
                     s_scr, p_scr, cross_scr, n_seg)
    xo_ref[...] = acc
    _norm_proj_body(acc, g_ref, wi_ref, rest[:n_out], splits, tails, False)


def _post_specs(tm, n_seg, mem_layer, w_layer, mem_per_tile):
    def tok(n):
        return pl.BlockSpec((tm, n), lambda i: (i, 0))

    mem = pl.BlockSpec((None, n_seg, X_WIDTH, N_MEM),
                       lambda i: (mem_layer, i if mem_per_tile else 0, 0, 0))
    w_out = pl.BlockSpec((None, D_INNER, D_MODEL), lambda i: (w_layer, 0, 0),
                         pipeline_mode=pl.Buffered(1))
    in_specs = [tok(D_MODEL), tok(MIX_WIDTH), tok(X_WIDTH), tok(D_INNER), mem, mem, w_out]
    seg_rows = tm // n_seg
    n_items = n_seg * (H_X // 2)
    scratch = [pltpu.VMEM((n_items, 2 * seg_rows, N_MEM), F32),
               pltpu.VMEM((n_items, 2 * seg_rows, N_MEM), BF16),
               pltpu.VMEM((tm, X_WIDTH), F32)]
    return in_specs, scratch, tok(D_MODEL)


def post_final(x, mix, xq, gate, mkt, mvt, mem_layer, w_all, w_layer, final_g, tm, n_seg,
               mem_per_tile):
    rows = x.shape[0]
    in_specs, scratch, x_spec = _post_specs(tm, n_seg, mem_layer, w_layer, mem_per_tile)
    return pl.pallas_call(
        functools.partial(_post_final_kernel, n_seg=n_seg),
        grid=(rows // tm,),
        in_specs=in_specs + [pl.BlockSpec((1, D_MODEL), lambda i: (0, 0))],
        out_specs=x_spec,
        out_shape=jax.ShapeDtypeStruct((rows, D_MODEL), F32),
        scratch_shapes=scratch,
        name=f"post_final_{rows}",
        compiler_params=_params("arbitrary"),
    )(x, mix, xq, gate, mkt, mvt, w_all, final_g.reshape(1, D_MODEL))


def post_pre(x, mix, xq, gate, mkt, mvt, mem_layer, w_out_all, layer, g_next, w_in_all,
             dtypes, tails, tm, n_seg, mem_per_tile):
    rows = x.shape[0]
    in_specs, scratch, x_spec = _post_specs(tm, n_seg, mem_layer, layer, mem_per_tile)
    in_specs += [pl.BlockSpec((1, D_MODEL), lambda i: (0, 0)),
                 pl.BlockSpec((None, D_MODEL, w_in_all.shape[2]), lambda i: (layer + 1, 0, 0),
                              pipeline_mode=pl.Buffered(1))]
    proj_specs, proj_shapes = _proj_out_specs(rows, tm, PROJ_SPLITS, dtypes, tails)
    return pl.pallas_call(
        functools.partial(_post_pre_kernel, n_seg=n_seg, splits=PROJ_SPLITS, tails=tuple(tails)),
        grid=(rows // tm,),
        in_specs=in_specs,
        out_specs=[x_spec] + proj_specs,
        out_shape=[jax.ShapeDtypeStruct((rows, D_MODEL), F32)] + proj_shapes,
        scratch_shapes=scratch,
        name=f"post_pre_{rows}",
        compiler_params=_params("arbitrary"),
    )(x, mix, xq, gate, mkt, mvt, w_out_all, g_next.reshape(1, D_MODEL), w_in_all)


def _rel_bias(table):
    band = PAST_BAND + CHUNK
    n_diag = band + CHUNK - 1
    offs = np.arange(n_diag) - (CHUNK - 1)
    idx = np.clip(PAST_BAND - offs, -REL_CLIP, REL_CLIP) + REL_CLIP
    diag = jnp.pad(table[:, idx].astype(F32), ((0, 0), (0, 1)))
    skew = jnp.tile(diag, (1, CHUNK))[:, :CHUNK * n_diag].reshape(H_A, CHUNK, n_diag)
    return skew[:, :, CHUNK - 1:CHUNK - 1 + band]


def _per_head_transposed(cache):
    return jnp.moveaxis(cache, -3, -1)


def kernel(x_prompt, x_sample, cache_a_k, cache_a_v, state_b, cache_mem_k, cache_mem_v, mem_prompt,
           ln_g, w_in, w_out, rel_bias_table, lower_bounds, hgrn_norm_g, mem_norm_g, w_mem_kv, final_g):
    bp, seq, _ = x_prompt.shape
    bs, dec_seq, _ = x_sample.shape
    assert bp == 1
    cache_len = cache_a_k.shape[2]
    n_s = bs * dec_seq
    keep = min(PAST_BAND, seq)

    w_in_b = w_in.astype(BF16)

    lb_all = jnp.cumsum(jax.nn.softmax(lower_bounds.astype(F32), axis=0), axis=0)
    lb_all = lb_all - lb_all[:1]

    cache_a_kt = _per_head_transposed(cache_a_k)
    cache_a_vt = _per_head_transposed(cache_a_v)
    cache_mem_kt = _per_head_transposed(cache_mem_k).reshape(DEPTH, bs, X_WIDTH, N_MEM)
    cache_mem_vt = _per_head_transposed(cache_mem_v).reshape(DEPTH, bs, X_WIDTH, N_MEM)
    zero_state = jnp.zeros((1, 1, H_B, DK_B, DK_B), F32)

    sample_tiles = (n_s // 2, bs // 2)

    def per_batch(u):
        return u.reshape(bs, dec_seq, u.shape[-1])

    def layer_io(l):
        attn_layer = l % 2 == 0
        return (BF16, BF16 if attn_layer else F32, BF16, BF16, F32), (1, 2) if attn_layer else ()

    mkt_all, mvt_all = mem_kv(mem_prompt.reshape(N_MEM, D_MODEL), mem_norm_g, w_mem_kv)
    mem_k_prompt = jnp.moveaxis(mkt_all.reshape(DEPTH, 1, H_X, HD_X, N_MEM), -1, 2)
    mem_v_prompt = jnp.moveaxis(mvt_all.reshape(DEPTH, 1, H_X, HD_X, N_MEM), -1, 2)

    xp = x_prompt.reshape(seq, D_MODEL)
    xs = x_sample.reshape(n_s, D_MODEL)
    dtypes, tails = layer_io(0)
    outs_p = norm_proj(xp, ln_g[0], w_in_b, 0, PROJ_SPLITS, dtypes, PAST_BAND, tails)
    outs_s = norm_proj(xs, ln_g[0], w_in_b, 0, PROJ_SPLITS, dtypes, n_s, tails)
    ak_p, av_p, sb_p, ak_s, av_s, sb_s = [], [], [], [], [], []
    for l in range(DEPTH):
        j = l // 2
        a_p, b_p, c_p, xq_p, g_p = outs_p[:5]
        a_s, b_s, c_s, xq_s, g_s = outs_s[:5]

        if l % 2 == 0:
            bias = _rel_bias(rel_bias_table[j])
            o_p = band_attn_prompt(a_p, b_p, c_p, bias, PAST_BAND)
            o_s = band_attn_sample(per_batch(a_s), per_batch(b_s), per_batch(c_s),
                                   cache_a_kt, cache_a_vt, j,
                                   bias[:, :dec_seq, :cache_len + dec_seq].reshape(
                                       H_A // 2, 2 * dec_seq, cache_len + dec_seq),
                                   BAND_SEQS_PER_STEP).reshape(n_s, MIX_WIDTH)
            assert keep == PAST_BAND
            ak_p.append(outs_p[5].reshape(1, keep, H_A, HD_A))
            av_p.append(outs_p[6].reshape(1, keep, H_A, HD_A))
            ak_s.append(outs_s[5].reshape(bs, dec_seq, H_A, HD_A))
            av_s.append(outs_s[6].reshape(bs, dec_seq, H_A, HD_A))
        else:
            lb = lb_all[j]
            lbc = jnp.stack([jnp.log2(lb), jnp.log1p(-lb) * LOG2_E, 1.0 - lb])
            o_p, s_p = hgrn2(a_p, b_p, c_p, zero_state, 0, lbc, hgrn_norm_g[j],
                             CHUNK, HGRN_BLOCKS_PER_STEP, True)
            o_s, s_s = hgrn2(a_s, b_s, c_s, state_b.astype(F32), j, lbc, hgrn_norm_g[j],
                             dec_seq, HGRN_SEQS_PER_STEP, False)
            sb_p.append(s_p)
            sb_s.append(s_s)
        mem_p = (mkt_all, mvt_all, l)
        mem_s = (cache_mem_kt, cache_mem_vt, l)
        if l == DEPTH - 1:
            xp = post_final(xp, o_p, xq_p, g_p, *mem_p, w_out, l, final_g, PAST_BAND, 1, False)
            xs = post_final(xs, o_s, xq_s, g_s, *mem_s, w_out, l, final_g, *sample_tiles, True)
        else:
            dtypes, tails = layer_io(l + 1)
            xp, *outs_p = post_pre(xp, o_p, xq_p, g_p, *mem_p, w_out, l, ln_g[l + 1], w_in_b,
                                   dtypes, tails, PAST_BAND, 1, False)
            xs, *outs_s = post_pre(xs, o_s, xq_s, g_s, *mem_s, w_out, l, ln_g[l + 1], w_in_b,
                                   dtypes, tails, *sample_tiles, True)
    return (xp.reshape(1, seq, D_MODEL), xs.reshape(bs, dec_seq, D_MODEL),
            jnp.stack(ak_p), jnp.stack(av_p), jnp.stack(sb_p), mem_k_prompt, mem_v_prompt,
            jnp.stack(ak_s), jnp.stack(av_s), jnp.stack(sb_s))
```
